```python
import jax, jax.numpy as jnp
from jax import lax
import numpy as np

D_MODEL = 1024
BATCH = 8
SEQ = 2048
DEPTH = 2

N_EVEN = (DEPTH + 1) // 2
N_ODD = DEPTH // 2
PLE_DIM = 256
D_FF = -(-(8 * D_MODEL) // (3 * 256)) * 256
W_MIX = D_MODEL
W_A = W_MIX // 2
LRU_BLOCKS = 8
LRU_BLOCK_DIM = W_A // LRU_BLOCKS
CONV_W = 4
LRU_C = 8.0
W_B = W_MIX - W_A
GLA_HEADS = 4
GLA_DV = W_B // GLA_HEADS
GLA_DK = GLA_DV // 2
GLA_RANK = 16
GLA_GATE_NORM = 16.0
GLA_CHUNK = 16
W_C = W_MIX // 2
RET_HEADS = 4
RET_DK = W_C // RET_HEADS
RET_DV = W_C // RET_HEADS
RET_CHUNK = 128
ROPE_BASE = 10000.0
MAX_POS_OFFSET = 1024
W_D = W_MIX - W_C
RWKV_HEAD_DIM = 64
RWKV_HEADS = W_D // RWKV_HEAD_DIM
RWKV_W_RANK = 64
RWKV_A_RANK = 64
RWKV_G_RANK = 128
RWKV_LN_EPS = 64e-5
EVEN_WIDTHS = (W_A, W_A, GLA_HEADS * GLA_DK, GLA_HEADS * GLA_DK, W_B, W_B, GLA_RANK)
RET_WIDTHS = (W_C, W_C, W_C, W_C)
RWKV_WIDTHS = (W_D, W_D, W_D, RWKV_W_RANK, RWKV_A_RANK, RWKV_G_RANK)
EVEN_COLS = sum(EVEN_WIDTHS)
RET_COLS = sum(RET_WIDTHS)
RWKV_COLS = sum(RWKV_WIDTHS)
ODD_COLS = RET_COLS + RWKV_COLS
NORM_EPS = 1e-6

kernel_name = 'hybrid_lru_gla_retention_rwkv7_trunk'


def _split(x, widths):
    outs, start = [], 0
    for w in widths:
        outs.append(x[..., start:start + w])
        start += w
    return outs


def rmsnorm(x, g):
    x32 = x.astype(jnp.float32)
    y = x32 * lax.rsqrt(jnp.mean(jnp.square(x32), axis=-1, keepdims=True) + NORM_EPS)
    return (y * g.astype(jnp.float32)).astype(x.dtype)


def head_norm(x, g, center, eps):
    if center:
        x = x - jnp.mean(x, axis=-1, keepdims=True)
    x = x * lax.rsqrt(jnp.mean(jnp.square(x), axis=-1, keepdims=True) + eps)
    return x.reshape(x.shape[0], x.shape[1], -1) * g.astype(jnp.float32)


def rotary(x, positions):
    half = x.shape[-1] // 2
    inv_freq = ROPE_BASE ** (-jnp.arange(half, dtype=jnp.float32) / half)
    ang = positions.astype(jnp.float32)[..., None] * inv_freq
    cos, sin = jnp.cos(ang)[:, :, None, :], jnp.sin(ang)[:, :, None, :]
    x1, x2 = x[..., :half], x[..., half:]
    return jnp.concatenate([x1 * cos - x2 * sin, x2 * cos + x1 * sin], axis=-1)


def causal_depthwise_conv(x, w, b):
    out = lax.conv_general_dilated(x, w[:, None, :], window_strides=(1,), padding=[(CONV_W - 1, 0)],
                                   dimension_numbers=('NWC', 'WIO', 'NWC'), feature_group_count=x.shape[-1])
    return out + b


def _linear_combine(left, right):
    a_l, b_l = left
    a_r, b_r = right
    return a_l * a_r, a_r * b_l + b_r


def rg_lru(x, w_r, b_r, w_i, b_i, a_param):
    bsz, seq, _ = x.shape
    xb = x.reshape(bsz, seq, LRU_BLOCKS, LRU_BLOCK_DIM)
    gate_r = jnp.einsum('bsgi,gij->bsgj', xb, w_r).reshape(bsz, seq, W_A) + b_r
    gate_i = jnp.einsum('bsgi,gij->bsgj', xb, w_i).reshape(bsz, seq, W_A) + b_i
    r = jax.nn.sigmoid(gate_r.astype(jnp.float32))
    i = jax.nn.sigmoid(gate_i.astype(jnp.float32))
    log_a = LRU_C * r * jax.nn.log_sigmoid(a_param.astype(jnp.float32))
    a = jnp.exp(log_a)
    u = jnp.sqrt(-jnp.expm1(2.0 * log_a)) * (i * x.astype(jnp.float32))
    _, h = lax.associative_scan(_linear_combine, (a, u), axis=1)
    return h


def gla(q, k, v, g, gk_lr, w_gk, b_gk, norm_g):
    f32 = jnp.float32
    bsz, seq, _ = q.shape
    n, c, h = seq // GLA_CHUNK, GLA_CHUNK, GLA_HEADS
    log_f = jax.nn.log_sigmoid((gk_lr @ w_gk + b_gk).astype(f32)) / GLA_GATE_NORM
    q = q.astype(f32).reshape(bsz, n, c, h, GLA_DK) * GLA_DK ** -0.5
    k = k.astype(f32).reshape(bsz, n, c, h, GLA_DK)
    v = v.astype(f32).reshape(bsz, n, c, h, GLA_DV)
    cum = jnp.cumsum(log_f.reshape(bsz, n, c, h, GLA_DK), axis=2)
    mask = jnp.tril(jnp.ones((c, c), dtype=bool))[:, :, None, None]
    rel = jnp.minimum(cum[:, :, :, None] - cum[:, :, None, :], 0.0)
    decay = jnp.where(mask, jnp.exp(rel), 0.0)
    scores = jnp.einsum('bnihk,bnjhk,bnijhk->bnhij', q, k, decay)
    o_intra = jnp.einsum('bnhij,bnjhv->bnihv', scores, v)
    q_in = q * jnp.exp(cum)
    k_out = k * jnp.exp(cum[:, :, -1:] - cum)
    chunk_decay = jnp.exp(cum[:, :, -1])

    def step(state, inp):
        qc, kc, vc, dc = inp
        o = jnp.einsum('bchk,bhkv->bchv', qc, state)
        state = state * dc[..., None] + jnp.einsum('bchk,bchv->bhkv', kc, vc)
        return state, o

    xs = (jnp.moveaxis(q_in, 1, 0), jnp.moveaxis(k_out, 1, 0), jnp.moveaxis(v, 1, 0), jnp.moveaxis(chunk_decay, 1, 0))
    state0 = jnp.zeros((bsz, h, GLA_DK, GLA_DV), f32)
    _, o_inter = lax.scan(step, state0, xs)
    o = (o_intra + jnp.moveaxis(o_inter, 0, 1)).reshape(bsz, seq, h, GLA_DV)
    o = head_norm(o, norm_g, center=False, eps=1e-5)
    return o * jax.nn.silu(g.astype(f32))


def retention(q, k, v, g, positions, norm_g):
    f32 = jnp.float32
    bsz, seq, _ = q.shape
    n, c, h = seq // RET_CHUNK, RET_CHUNK, RET_HEADS
    q = rotary(q.astype(f32).reshape(bsz, seq, h, RET_DK), positions) * RET_DK ** -0.5
    k = rotary(k.astype(f32).reshape(bsz, seq, h, RET_DK), positions)
    q = q.reshape(bsz, n, c, h, RET_DK)
    k = k.reshape(bsz, n, c, h, RET_DK)
    v = v.astype(f32).reshape(bsz, n, c, h, RET_DV)
    log_gamma = jnp.log1p(-jnp.exp2(-5.0 - jnp.arange(h, dtype=f32)))
    idx = jnp.arange(c, dtype=f32)
    rel = idx[:, None] - idx[None, :]
    dmask = jnp.where(rel >= 0, jnp.exp(jnp.maximum(rel, 0.0) * log_gamma[:, None, None]), 0.0)
    scores = jnp.einsum('bnihd,bnjhd->bnhij', q, k) * dmask
    o_intra = jnp.einsum('bnhij,bnjhv->bnihv', scores, v)
    q_in = q * jnp.exp((idx + 1.0)[:, None] * log_gamma)[:, :, None]
    k_out = k * jnp.exp((c - 1.0 - idx)[:, None] * log_gamma)[:, :, None]
    chunk_decay = jnp.exp(c * log_gamma)

    def step(state, inp):
        qc, kc, vc = inp
        o = jnp.einsum('bchk,bhkv->bchv', qc, state)
        state = state * chunk_decay[:, None, None] + jnp.einsum('bchk,bchv->bhkv', kc, vc)
        return state, o

    xs = (jnp.moveaxis(q_in, 1, 0), jnp.moveaxis(k_out, 1, 0), jnp.moveaxis(v, 1, 0))
    state0 = jnp.zeros((bsz, h, RET_DK, RET_DV), f32)
    _, o_inter = lax.scan(step, state0, xs)
    o = (o_intra + jnp.moveaxis(o_inter, 0, 1)).reshape(bsz, seq, h, RET_DV)
    o = head_norm(o, norm_g, center=True, eps=1e-5)
    return o * jax.nn.silu(g.astype(f32))


def rwkv7(r, k, v, w_lr, a_lr, g_lr, w0, w_w2, a0, a_w2, g_w2, k_k, k_a, r_k, norm_g):
    f32 = jnp.float32
    bsz, seq, _ = r.shape
    h, d = RWKV_HEADS, RWKV_HEAD_DIM
    r, k, v = r.astype(f32), k.astype(f32), v.astype(f32)
    w = -jax.nn.softplus(-(w0 + jnp.tanh(w_lr) @ w_w2).astype(f32)) - 0.5
    decay = jnp.exp(-jnp.exp(w))
    a = jax.nn.sigmoid((a0 + a_lr @ a_w2).astype(f32))
    g = (jax.nn.sigmoid(g_lr) @ g_w2).astype(f32)
    kk = (k * k_k.astype(f32)).reshape(bsz, seq, h, d)
    kk = kk * lax.rsqrt(jnp.sum(jnp.square(kk), axis=-1, keepdims=True) + 1e-12)
    k = k * (1.0 + (a - 1.0) * k_a.astype(f32))
    rh, kh, vh = r.reshape(bsz, seq, h, d), k.reshape(bsz, seq, h, d), v.reshape(bsz, seq, h, d)
    dh, ah = decay.reshape(bsz, seq, h, d), a.reshape(bsz, seq, h, d)

    def step(state, inp):
        r_t, k_t, v_t, w_t, kk_t, b_t = inp
        sa = jnp.einsum('bhvk,bhk->bhv', state, -kk_t)
        state = state * w_t[:, :, None, :] + sa[..., :, None] * b_t[:, :, None, :] + v_t[..., :, None] * k_t[..., None, :]
        y = jnp.einsum('bhvk,bhk->bhv', state, r_t)
        return state, y

    xs = (jnp.moveaxis(rh, 1, 0), jnp.moveaxis(kh, 1, 0), jnp.moveaxis(vh, 1, 0),
          jnp.moveaxis(dh, 1, 0), jnp.moveaxis(kk, 1, 0), jnp.moveaxis(kk * ah, 1, 0))
    state0 = jnp.zeros((bsz, h, d, d), f32)
    _, y = lax.scan(step, state0, xs)
    y = head_norm(jnp.moveaxis(y, 0, 1), norm_g, center=True, eps=RWKV_LN_EPS)
    bonus = jnp.sum(rh * kh * r_k.astype(f32).reshape(h, d), axis=-1, keepdims=True) * vh
    return (y + bonus.reshape(bsz, seq, W_D)) * g


def even_mixer(xn, w_in, conv_w, conv_b, lru_wr, lru_br, lru_wi, lru_bi, lru_a, gla_wgk, gla_bgk, gla_norm, w_out):
    proj = xn @ w_in
    a_x, a_gate, b_q, b_k, b_v, b_g, b_gk = _split(proj, EVEN_WIDTHS)
    a_x = causal_depthwise_conv(a_x, conv_w, conv_b)
    y_a = rg_lru(a_x, lru_wr, lru_br, lru_wi, lru_bi, lru_a) * jax.nn.gelu(a_gate.astype(jnp.float32), approximate=True)
    y_b = gla(b_q, b_k, b_v, b_g, b_gk, gla_wgk, gla_bgk, gla_norm)
    y = jnp.concatenate([y_a, y_b], axis=-1).astype(xn.dtype)
    return y @ w_out


def odd_mixer(xn, positions, w_in, ret_norm, mu, w0, w_w2, a0, a_w2, g_w2, k_k, k_a, r_k, rwkv_norm, w_out):
    proj = xn @ w_in
    c_part, d_part = proj[..., :RET_COLS], proj[..., RET_COLS:]
    c_q, c_k, c_v, c_g = _split(c_part, RET_WIDTHS)
    y_c = retention(c_q, c_k, c_v, c_g, positions, ret_norm)
    d_prev = jnp.pad(d_part, ((0, 0), (1, 0), (0, 0)))[:, :-1]
    d_part = d_part + mu * (d_prev - d_part)
    d_r, d_k, d_v, d_w, d_a, d_g = _split(d_part, RWKV_WIDTHS)
    y_d = rwkv7(d_r, d_k, d_v, d_w, d_a, d_g, w0, w_w2, a0, a_w2, g_w2, k_k, k_a, r_k, rwkv_norm)
    y = jnp.concatenate([y_c, y_d], axis=-1).astype(xn.dtype)
    return y @ w_out


def swiglu(x, w_gate, w_up, w_down):
    return (jax.nn.silu(x @ w_gate) * (x @ w_up)) @ w_down


def setup_inputs(seed: int = 0) -> dict:
    key = jax.random.key(seed)
    ks = iter(jax.random.split(key, 64))
    f32 = jnp.float32

    def nrm(shape, scale):
        return jax.random.normal(next(ks), shape, f32) * scale

    def unif(shape, lo, hi):
        return jax.random.uniform(next(ks), shape, f32, lo, hi)

    def gain(shape):
        return 1.0 + nrm(shape, 0.02)

    x = nrm((BATCH, SEQ, D_MODEL), 1.0)
    p = nrm((DEPTH, BATCH, SEQ, PLE_DIM), 1.0)
    offset = jax.random.randint(next(ks), (BATCH, 1), 0, MAX_POS_OFFSET, dtype=jnp.int32)
    positions = offset + jnp.arange(SEQ, dtype=jnp.int32)[None, :]
    lru_target = unif((N_EVEN, W_A), 0.9, 0.999)
    lru_s = lru_target ** (1.0 / LRU_C)
    return {
        'x': x, 'p': p, 'positions': positions,
        'ev_w_in': nrm((N_EVEN, D_MODEL, EVEN_COLS), D_MODEL ** -0.5),
        'ev_conv_w': nrm((N_EVEN, CONV_W, W_A), CONV_W ** -0.5),
        'ev_conv_b': nrm((N_EVEN, W_A), 0.01),
        'ev_lru_wr': nrm((N_EVEN, LRU_BLOCKS, LRU_BLOCK_DIM, LRU_BLOCK_DIM), LRU_BLOCK_DIM ** -0.5),
        'ev_lru_br': nrm((N_EVEN, W_A), 0.01),
        'ev_lru_wi': nrm((N_EVEN, LRU_BLOCKS, LRU_BLOCK_DIM, LRU_BLOCK_DIM), LRU_BLOCK_DIM ** -0.5),
        'ev_lru_bi': nrm((N_EVEN, W_A), 0.01),
        'ev_lru_a': jnp.log(lru_s) - jnp.log1p(-lru_s),
        'ev_gla_wgk': nrm((N_EVEN, GLA_RANK, GLA_HEADS * GLA_DK), GLA_RANK ** -0.5),
        'ev_gla_bgk': nrm((N_EVEN, GLA_HEADS * GLA_DK), 0.1),
        'ev_gla_norm': gain((N_EVEN, W_B)),
        'ev_w_out': nrm((N_EVEN, W_MIX, D_MODEL), W_MIX ** -0.5),
        'od_w_in': nrm((N_ODD, D_MODEL, ODD_COLS), D_MODEL ** -0.5),
        'od_ret_norm': gain((N_ODD, W_C)),
        'od_rwkv_mu': unif((N_ODD, RWKV_COLS), 0.0, 1.0),
        'od_rwkv_w0': unif((N_ODD, W_D), -6.0, -1.0),
        'od_rwkv_ww2': nrm((N_ODD, RWKV_W_RANK, W_D), 0.1),
        'od_rwkv_a0': nrm((N_ODD, W_D), 0.1),
        'od_rwkv_aw2': nrm((N_ODD, RWKV_A_RANK, W_D), 0.1),
        'od_rwkv_gw2': nrm((N_ODD, RWKV_G_RANK, W_D), RWKV_G_RANK ** -0.5),
        'od_rwkv_kk': 0.85 + nrm((N_ODD, W_D), 0.02),
        'od_rwkv_ka': gain((N_ODD, W_D)),
        'od_rwkv_rk': nrm((N_ODD, W_D), 0.1),
        'od_rwkv_norm': gain((N_ODD, W_D)),
        'od_w_out': nrm((N_ODD, W_MIX, D_MODEL), W_MIX ** -0.5),
        'mix_norm': gain((DEPTH, D_MODEL)),
        'ffn_norm': gain((DEPTH, D_MODEL)),
        'ffn_w_gate': nrm((DEPTH, D_MODEL, D_FF), D_MODEL ** -0.5),
        'ffn_w_up': nrm((DEPTH, D_MODEL, D_FF), D_MODEL ** -0.5),
        'ffn_w_down': nrm((DEPTH, D_FF, D_MODEL), D_FF ** -0.5),
        'ple_norm': gain((DEPTH, D_MODEL)),
        'ple_w_gate': nrm((DEPTH, D_MODEL, D_MODEL), D_MODEL ** -0.5),
        'ple_b_gate': nrm((DEPTH, D_MODEL), 0.01),
        'ple_w_proj': nrm((DEPTH, PLE_DIM, D_MODEL), PLE_DIM ** -0.5),
        'final_norm': gain((D_MODEL,)),
    }


def reference(x, p, positions, ev_w_in, ev_conv_w, ev_conv_b, ev_lru_wr, ev_lru_br, ev_lru_wi, ev_lru_bi, ev_lru_a,
              ev_gla_wgk, ev_gla_bgk, ev_gla_norm, ev_w_out, od_w_in, od_ret_norm, od_rwkv_mu, od_rwkv_w0, od_rwkv_ww2,
              od_rwkv_a0, od_rwkv_aw2, od_rwkv_gw2, od_rwkv_kk, od_rwkv_ka, od_rwkv_rk, od_rwkv_norm, od_w_out,
              mix_norm, ffn_norm, ffn_w_gate, ffn_w_up, ffn_w_down, ple_norm, ple_w_gate, ple_b_gate, ple_w_proj,
              final_norm):
    h = x
    for i in range(DEPTH):
        j = i // 2
        xn = rmsnorm(h, mix_norm[i])
        if i % 2 == 0:
            mix = even_mixer(xn, ev_w_in[j], ev_conv_w[j], ev_conv_b[j], ev_lru_wr[j], ev_lru_br[j], ev_lru_wi[j],
                             ev_lru_bi[j], ev_lru_a[j], ev_gla_wgk[j], ev_gla_bgk[j], ev_gla_norm[j], ev_w_out[j])
        else:
            mix = odd_mixer(xn, positions, od_w_in[j], od_ret_norm[j], od_rwkv_mu[j], od_rwkv_w0[j], od_rwkv_ww2[j],
                            od_rwkv_a0[j], od_rwkv_aw2[j], od_rwkv_gw2[j], od_rwkv_kk[j], od_rwkv_ka[j], od_rwkv_rk[j],
                            od_rwkv_norm[j], od_w_out[j])
        h = h + mix.astype(h.dtype)
        h = h + swiglu(rmsnorm(h, ffn_norm[i]), ffn_w_gate[i], ffn_w_up[i], ffn_w_down[i]).astype(h.dtype)
        gate = jax.nn.sigmoid((rmsnorm(h, ple_norm[i]) @ ple_w_gate[i] + ple_b_gate[i]).astype(jnp.float32))
        h = h + (gate * (p[i] @ ple_w_proj[i]).astype(jnp.float32)).astype(h.dtype)
    return rmsnorm(h, final_norm)
```

```python
import functools
import math

import jax
import jax.numpy as jnp
from jax import lax
from jax.experimental import pallas as pl
from jax.experimental.pallas import tpu as pltpu

F32 = jnp.float32
BF16 = jnp.bfloat16
HIGHEST = lax.Precision.HIGHEST

NORM_EPS = 1e-6
LANES = 128
LRU_BLOCKS = 8
CONV_W = 4
LRU_C = 8.0
GLA_HEADS = 4
GLA_GATE_NORM = 16.0
GLA_SUB = 16
RET_HEADS = 4
RET_CHUNK = 128
ROPE_BASE = 10000.0
RWKV_HEAD_DIM = 64
RWKV_LN_EPS = 64e-5
HEAD_NORM_EPS = 1e-5

MIX_CHUNK = 64
VMEM_LIMIT = 56 * 1024 * 1024


def _log_sigmoid(x):
    return jnp.minimum(x, 0.0) - jnp.log1p(jnp.exp(-jnp.abs(x)))


def _softplus(x):
    return jnp.maximum(x, 0.0) + jnp.log1p(jnp.exp(-jnp.abs(x)))


def _silu(x):
    return x * jax.nn.sigmoid(x)


def _gelu_tanh(x):
    return 0.5 * x * (1.0 + jnp.tanh(math.sqrt(2.0 / math.pi) * (x + 0.044715 * (x * x * x))))


def _rms(x, g):
    return x * lax.rsqrt(jnp.mean(x * x, axis=-1, keepdims=True) + NORM_EPS) * g


def _dot(a, b, precision=None):
    return jnp.dot(a, b, preferred_element_type=F32, precision=precision)


def _dot_nt(a, b, precision=None):
    return lax.dot_general(a, b, (((1,), (1,)), ((), ())), preferred_element_type=F32, precision=precision)


def _dot_tn(a, b, precision=None):
    return lax.dot_general(a, b, (((0,), (0,)), ((), ())), preferred_element_type=F32, precision=precision)


def _tri(n, strict=False):
    row = lax.broadcasted_iota(jnp.int32, (n, n), 0)
    col = lax.broadcasted_iota(jnp.int32, (n, n), 1)
    return (row > col) if strict else (row >= col)


def _norm_matmul_kernel(x_ref, g_ref, w_ref, o_ref, xn_ref):
    @pl.when(pl.program_id(1) == 0)
    def _():
        xn_ref[...] = _rms(x_ref[...], g_ref[...]).astype(BF16)

    o_ref[...] = _dot(xn_ref[...], w_ref[...])


def _norm_matmul(x, g, w, *, tm, tn):
    m, d = x.shape
    n = w.shape[1]
    return pl.pallas_call(
        _norm_matmul_kernel,
        grid=(m // tm, n // tn),
        in_specs=[pl.BlockSpec((tm, d), lambda i, j: (i, 0)),
                  pl.BlockSpec((1, d), lambda i, j: (0, 0)),
                  pl.BlockSpec((d, tn), lambda i, j: (0, j))],
        out_specs=pl.BlockSpec((tm, tn), lambda i, j: (i, j)),
        out_shape=jax.ShapeDtypeStruct((m, n), F32),
        scratch_shapes=[pltpu.VMEM((tm, d), BF16)],
        compiler_params=pltpu.CompilerParams(dimension_semantics=("arbitrary", "arbitrary"),
                                             vmem_limit_bytes=VMEM_LIMIT),
        name="norm_in_proj",
    )(x, g, w)


def _even_mixer_kernel(proj_ref, convw_ref, convb_ref, wgate_ref, bgate_ref, lrua_ref, wgk_ref, bgk_ref, gnorm_ref,
                       y_ref, xtail_ref, hcar_ref, st_ref, *, ts, wa, dk, dv, rank):
    heads = GLA_HEADS
    wq = heads * dk
    wb = heads * dv
    o_ax, o_gate, o_q = 0, wa, 2 * wa
    o_k, o_v = o_q + wq, o_q + 2 * wq
    o_g, o_gk = o_v + wb, o_v + 2 * wb

    @pl.when(pl.program_id(1) == 0)
    def _():
        xtail_ref[...] = jnp.zeros_like(xtail_ref)
        hcar_ref[...] = jnp.zeros_like(hcar_ref)
        st_ref[...] = jnp.zeros_like(st_ref)

    ax = proj_ref[:, o_ax:o_ax + wa]
    xcat = jnp.concatenate([xtail_ref[...], ax], axis=0)
    xtail_ref[...] = ax[ts - 8:ts]
    cw = convw_ref[...]
    xc = convb_ref[...] + cw[CONV_W - 1:CONV_W] * ax
    for s in range(1, CONV_W):
        xc = xc + cw[CONV_W - 1 - s:CONV_W - s] * xcat[8 - s:8 - s + ts]
    gates = _dot(xc.astype(BF16), wgate_ref[...]) + bgate_ref[...]
    r = jax.nn.sigmoid(gates[:, :wa])
    i = jax.nn.sigmoid(gates[:, wa:])
    log_a = LRU_C * r * _log_sigmoid(lrua_ref[...])
    a = jnp.exp(log_a)
    t = jnp.tanh(log_a)
    u = jnp.sqrt(-2.0 * t / (1.0 - t)) * (i * xc)
    row = lax.broadcasted_iota(jnp.int32, (ts, wa), 0)
    d = 1
    while d < ts:
        keep = row >= d
        u = jnp.where(keep, a * pltpu.roll(u, d, 0) + u, u)
        a = jnp.where(keep, a * pltpu.roll(a, d, 0), a)
        d *= 2
    h = u + a * hcar_ref[...]
    hcar_ref[...] = h[ts - 1:ts]
    y_ref[:, 0:wa] = (h * _gelu_tanh(proj_ref[:, o_gate:o_gate + wa])).astype(y_ref.dtype)

    cg = MIX_CHUNK
    z = _dot(proj_ref[:, o_gk:o_gk + rank], wgk_ref[...], precision=HIGHEST) + bgk_ref[...]
    log_f = _log_sigmoid(z) * (1.0 / GLA_GATE_NORM)
    tri_incl = jnp.where(_tri(cg), 1.0, 0.0).astype(F32)
    causal = _tri(cg)
    rowc = lax.broadcasted_iota(jnp.int32, (cg, wq), 0)
    for c in range(ts // cg):
        r0 = c * cg
        cum = _dot(tri_incl, log_f[r0:r0 + cg], precision=HIGHEST)
        q = proj_ref[r0:r0 + cg, o_q:o_q + wq] * (dk ** -0.5)
        k = proj_ref[r0:r0 + cg, o_k:o_k + wq]
        v = proj_ref[r0:r0 + cg, o_v:o_v + wb].astype(BF16)
        cum_last = cum[cg - 1:cg]
        q_in = (q * jnp.exp(cum)).astype(BF16)
        k_out = (k * jnp.exp(cum_last - cum)).astype(BF16)
        chunk_decay = jnp.exp(cum_last)
        score_rows = [[] for _ in range(heads)]
        for ib in range(cg // GLA_SUB):
            b0 = ib * GLA_SUB
            base = cum[b0 - 1:b0] if ib > 0 else jnp.zeros((1, wq), F32)
            qb = (q[b0:b0 + GLA_SUB] * jnp.exp(cum[b0:b0 + GLA_SUB] - base)).astype(BF16)
            kb = jnp.where(rowc < b0 + GLA_SUB, k * jnp.exp(base - cum), 0.0).astype(BF16)
            for hd in range(heads):
                score_rows[hd].append(_dot_nt(qb[:, hd * dk:(hd + 1) * dk], kb[:, hd * dk:(hd + 1) * dk]))
        outs = []
        for hd in range(heads):
            ks = slice(hd * dk, (hd + 1) * dk)
            vs = slice(hd * dv, (hd + 1) * dv)
            scores = jnp.where(causal, jnp.concatenate(score_rows[hd], axis=0), 0.0).astype(BF16)
            st = st_ref[hd]
            o = _dot(scores, v[:, vs]) + _dot_nt(q_in[:, ks], st.astype(BF16))
            st_ref[hd] = st * chunk_decay[:, ks] + _dot_tn(v[:, vs], k_out[:, ks])
            o = o * lax.rsqrt(jnp.mean(o * o, axis=-1, keepdims=True) + HEAD_NORM_EPS)
            outs.append(o)
        o_all = jnp.concatenate(outs, axis=-1) * gnorm_ref[...]
        g = proj_ref[r0:r0 + cg, o_g:o_g + wb]
        y_ref[r0:r0 + cg, wa:wa + wb] = (o_all * _silu(g)).astype(y_ref.dtype)


def _even_mixer(proj, convw, convb, wgate, bgate, lrua, wgk, bgk, gnorm, *, ts, wa, dk, dv, rank):
    bsz, seq, cols = proj.shape
    heads = GLA_HEADS
    wb = heads * dv
    const = lambda shape: pl.BlockSpec(shape, lambda b, s: (0,) * len(shape))
    kern = functools.partial(_even_mixer_kernel, ts=ts, wa=wa, dk=dk, dv=dv, rank=rank)
    return pl.pallas_call(
        kern,
        grid=(bsz, seq // ts),
        in_specs=[pl.BlockSpec((None, ts, cols), lambda b, s: (b, s, 0)),
                  const(convw.shape), const(convb.shape), const(wgate.shape), const(bgate.shape),
                  const(lrua.shape), const(wgk.shape), const(bgk.shape), const(gnorm.shape)],
        out_specs=pl.BlockSpec((None, ts, wa + wb), lambda b, s: (b, s, 0)),
        out_shape=jax.ShapeDtypeStruct((bsz, seq, wa + wb), BF16),
        scratch_shapes=[pltpu.VMEM((8, wa), F32), pltpu.VMEM((1, wa), F32), pltpu.VMEM((heads, dv, dk), F32)],
        compiler_params=pltpu.CompilerParams(dimension_semantics=("arbitrary", "arbitrary"),
                                             vmem_limit_bytes=VMEM_LIMIT),
        name="even_mixer",
    )(proj, convw, convb, wgate, bgate, lrua, wgk, bgk, gnorm)


def _odd_mixer_kernel(proj_ref, pos_ref, freq_ref, rnorm_ref, mu_ref, w0_ref, ww2_ref, a0_ref, aw2_ref, gw2_ref,
                      kk_ref, ka_ref, rk_ref, wnorm_ref, seg_ref,
                      y_ref, rstate_ref, dprev_ref, wstate_ref, *, ts, wc, wd, rw, ra, rg):
    f32 = F32

    @pl.when(pl.program_id(1) == 0)
    def _():
        rstate_ref[...] = jnp.zeros_like(rstate_ref)
        dprev_ref[...] = jnp.zeros_like(dprev_ref)
        wstate_ref[...] = jnp.zeros_like(wstate_ref)

    c = ts
    dh = wc // RET_HEADS
    half = dh // 2
    ang = pos_ref[...] * freq_ref[...]
    cos, sin = jnp.cos(ang), jnp.sin(ang)
    cos2 = jnp.concatenate([cos, cos], axis=-1)
    sin2 = jnp.concatenate([-sin, sin], axis=-1)
    rowi = lax.broadcasted_iota(jnp.int32, (c, c), 0)
    coli = lax.broadcasted_iota(jnp.int32, (c, c), 1)
    rel = (rowi - coli).astype(f32)
    rowd = lax.broadcasted_iota(jnp.int32, (c, dh), 0).astype(f32)
    outs = []
    for hd in range(RET_HEADS):
        log_gamma = math.log1p(-2.0 ** (-5.0 - hd))
        hs = slice(hd * dh, (hd + 1) * dh)
        q = proj_ref[:, hd * dh:(hd + 1) * dh]
        k = proj_ref[:, wc + hd * dh:wc + (hd + 1) * dh]
        v = proj_ref[:, 2 * wc + hd * dh:2 * wc + (hd + 1) * dh].astype(BF16)
        q = (q * cos2 + pltpu.roll(q, half, 1) * sin2) * (dh ** -0.5)
        k = k * cos2 + pltpu.roll(k, half, 1) * sin2
        dmask = jnp.where(rel >= 0.0, jnp.exp(jnp.maximum(rel, 0.0) * log_gamma), 0.0)
        scores = (_dot_nt(q.astype(BF16), k.astype(BF16)) * dmask).astype(BF16)
        q_in = (q * jnp.exp((rowd + 1.0) * log_gamma)).astype(BF16)
        k_out = (k * jnp.exp((c - 1.0 - rowd) * log_gamma)).astype(BF16)
        st = rstate_ref[hd]
        o = _dot(scores, v) + _dot(q_in, st.astype(BF16))
        rstate_ref[hd] = st * math.exp(c * log_gamma) + _dot_tn(k_out, v)
        o = o - jnp.mean(o, axis=-1, keepdims=True)
        o = o * lax.rsqrt(jnp.mean(o * o, axis=-1, keepdims=True) + HEAD_NORM_EPS)
        outs.append(o)
    g = proj_ref[:, 3 * wc:4 * wc]
    y_ref[:, 0:wc] = (jnp.concatenate(outs, axis=-1) * rnorm_ref[...] * _silu(g)).astype(y_ref.dtype)

    o_d = 4 * wc
    dcols = 3 * wd + rw + ra + rg
    dpart = proj_ref[:, o_d:o_d + dcols]
    dshift = jnp.concatenate([dprev_ref[...], dpart], axis=0)[7:7 + ts]
    dprev_ref[...] = dpart[ts - 8:ts]
    dpart = dpart + mu_ref[...] * (dshift - dpart)
    r = dpart[:, 0:wd]
    k = dpart[:, wd:2 * wd]
    v = dpart[:, 2 * wd:3 * wd]
    w_lr = dpart[:, 3 * wd:3 * wd + rw]
    a_lr = dpart[:, 3 * wd + rw:3 * wd + rw + ra]
    g_lr = dpart[:, 3 * wd + rw + ra:dcols]
    w = -_softplus(-(w0_ref[...] + _dot(jnp.tanh(w_lr).astype(BF16), ww2_ref[...]))) - 0.5
    log_decay = -jnp.exp(w)
    a = jax.nn.sigmoid(a0_ref[...] + _dot(a_lr.astype(BF16), aw2_ref[...]))
    g = _dot(jax.nn.sigmoid(g_lr).astype(BF16), gw2_ref[...])
    seg = seg_ref[...]
    kk = k * kk_ref[...]
    kk = kk * lax.rsqrt(_dot(kk * kk, seg, precision=HIGHEST) + 1e-12)
    k = k * (1.0 + (a - 1.0) * ka_ref[...])
    bonus = _dot(r * k * rk_ref[...], seg, precision=HIGHEST) * v
    kb = kk * a
    hdim = RWKV_HEAD_DIM
    heads = wd // hdim
    cm = MIX_CHUNK
    tri_incl = jnp.where(_tri(cm), 1.0, 0.0).astype(f32)
    strict = _tri(cm, strict=True)
    incl = _tri(cm)
    eye = jnp.where(lax.broadcasted_iota(jnp.int32, (cm, cm), 0) == lax.broadcasted_iota(jnp.int32, (cm, cm), 1),
                    1.0, 0.0).astype(f32)
    for ch in range(ts // cm):
        rs = slice(ch * cm, (ch + 1) * cm)
        ld = log_decay[rs]
        cum = _dot(tri_incl, ld, precision=HIGHEST)
        cum_last = cum[cm - 1:cm]
        p_incl = jnp.exp(cum)
        p_inv = jnp.exp(-cum)
        a_t = -kk[rs] * jnp.exp(cum - ld)
        r_t = r[rs] * p_incl
        b_t = kb[rs] * p_inv
        k_t = k[rs] * p_inv
        p_out = jnp.exp(cum_last - cum)
        b_o = kb[rs] * p_out
        k_o = k[rs] * p_out
        p_last = jnp.exp(cum_last)
        ys = []
        for hd in range(heads):
            hs = slice(hd * hdim, (hd + 1) * hdim)
            st = wstate_ref[hd]
            vh = v[rs, hs]
            lhs = jnp.concatenate([a_t[:, hs], r_t[:, hs]], axis=0)
            rhs = jnp.concatenate([b_t[:, hs], k_t[:, hs]], axis=0)
            big = _dot_nt(lhs, rhs, precision=HIGHEST)
            a_ab = jnp.where(strict, big[:cm, :cm], 0.0)
            a_ak = jnp.where(strict, big[:cm, cm:], 0.0)
            a_rb = jnp.where(incl, big[cm:, :cm], 0.0)
            a_rk = jnp.where(incl, big[cm:, cm:], 0.0)
            x = a_ab
            tinv = eye + x
            span = 2
            while span < cm:
                x = _dot(x, x, precision=HIGHEST)
                tinv = tinv + _dot(tinv, x, precision=HIGHEST)
                span *= 2
            from_state = _dot_nt(lhs, st, precision=HIGHEST)
            u = _dot(tinv, from_state[:cm] + _dot(a_ak, vh, precision=HIGHEST), precision=HIGHEST)
            uv = jnp.concatenate([u, vh], axis=0)
            y = from_state[cm:] + _dot(jnp.concatenate([a_rb, a_rk], axis=1), uv, precision=HIGHEST)
            outk = jnp.concatenate([b_o[:, hs], k_o[:, hs]], axis=0)
            wstate_ref[hd] = st * p_last[:, hs] + _dot_tn(uv, outk, precision=HIGHEST)
            ys.append(y)
        y = jnp.concatenate(ys, axis=-1)
        inv_h = 1.0 / hdim
        y = y - _dot(y, seg, precision=HIGHEST) * inv_h
        y = y * lax.rsqrt(_dot(y * y, seg, precision=HIGHEST) * inv_h + RWKV_LN_EPS)
        y = (y * wnorm_ref[...] + bonus[rs]) * g[rs]
        y_ref[ch * cm:(ch + 1) * cm, wc:wc + wd] = y.astype(y_ref.dtype)


def _odd_mixer(proj, pos, freq, rnorm, mu, w0, ww2, a0, aw2, gw2, kk, ka, rk, wnorm, seg, *, ts, wc, wd, rw, ra, rg):
    bsz, seq, cols = proj.shape
    dh = wc // RET_HEADS
    const = lambda shape: pl.BlockSpec(shape, lambda b, s: (0,) * len(shape))
    smalls = (freq, rnorm, mu, w0, ww2, a0, aw2, gw2, kk, ka, rk, wnorm, seg)
    kern = functools.partial(_odd_mixer_kernel, ts=ts, wc=wc, wd=wd, rw=rw, ra=ra, rg=rg)
    dcols = 3 * wd + rw + ra + rg
    return pl.pallas_call(
        kern,
        grid=(bsz, seq // ts),
        in_specs=[pl.BlockSpec((None, ts, cols), lambda b, s: (b, s, 0)),
                  pl.BlockSpec((None, ts, 1), lambda b, s: (b, s, 0))] + [const(t.shape) for t in smalls],
        out_specs=pl.BlockSpec((None, ts, wc + wd), lambda b, s: (b, s, 0)),
        out_shape=jax.ShapeDtypeStruct((bsz, seq, wc + wd), BF16),
        scratch_shapes=[pltpu.VMEM((RET_HEADS, dh, dh), F32), pltpu.VMEM((8, dcols), F32),
                        pltpu.VMEM((wd // RWKV_HEAD_DIM, RWKV_HEAD_DIM, RWKV_HEAD_DIM), F32)],
        compiler_params=pltpu.CompilerParams(dimension_semantics=("arbitrary", "arbitrary"),
                                             vmem_limit_bytes=VMEM_LIMIT),
        name="odd_mixer",
    )(proj, pos, *smalls)


def _post_kernel(h_ref, y_ref, p_ref, wout_ref, fnorm_ref, wg_ref, wu_ref, wd_ref, pnorm_ref, pwg_ref, pbg_ref,
                 pwp_ref, onorm_ref, o_ref, *, final):
    h = h_ref[...] + _dot(y_ref[...], wout_ref[...])
    xn = _rms(h, fnorm_ref[...]).astype(BF16)
    act = (_silu(_dot(xn, wg_ref[...])) * _dot(xn, wu_ref[...])).astype(BF16)
    h = h + _dot(act, wd_ref[...])
    xg = _rms(h, pnorm_ref[...]).astype(BF16)
    gate = jax.nn.sigmoid(_dot(xg, pwg_ref[...]) + pbg_ref[...])
    h = h + gate * _dot(p_ref[...].astype(BF16), pwp_ref[...])
    if final:
        h = _rms(h, onorm_ref[...])
    o_ref[...] = h


def _post(h, y, p, wout, fnorm, wg, wu, wd, pnorm, pwg, pbg, pwp, onorm, *, tm, final):
    m, d = h.shape
    rows = lambda width: pl.BlockSpec((tm, width), lambda i: (i, 0))
    resident = lambda t: pl.BlockSpec(t.shape, lambda i: (0, 0), pipeline_mode=pl.Buffered(1))
    weights = (wout, fnorm, wg, wu, wd, pnorm, pwg, pbg, pwp, onorm)
    return pl.pallas_call(
        functools.partial(_post_kernel, final=final),
        grid=(m // tm,),
        in_specs=[rows(d), rows(y.shape[1]), rows(p.shape[1])] + [resident(t) for t in weights],
        out_specs=rows(d),
        out_shape=jax.ShapeDtypeStruct((m, d), F32),
        compiler_params=pltpu.CompilerParams(dimension_semantics=("arbitrary",), vmem_limit_bytes=VMEM_LIMIT),
        name="post_mixer",
    )(h, y, p, *weights)


def _block_diag(blocks):
    g, bi, bj = blocks.shape
    eye = jnp.eye(g, dtype=blocks.dtype)
    return (eye[:, None, :, None] * blocks[:, :, None, :]).reshape(g * bi, g * bj)


def kernel(x, p, positions, ev_w_in, ev_conv_w, ev_conv_b, ev_lru_wr, ev_lru_br, ev_lru_wi, ev_lru_bi, ev_lru_a, ev_gla_wgk, ev_gla_bgk, ev_gla_norm, ev_w_out, od_w_in, od_ret_norm, od_rwkv_mu, od_rwkv_w0, od_rwkv_ww2, od_rwkv_a0, od_rwkv_aw2, od_rwkv_gw2, od_rwkv_kk, od_rwkv_ka, od_rwkv_rk, od_rwkv_norm, od_w_out, mix_norm, ffn_norm, ffn_w_gate, ffn_w_up, ffn_w_down, ple_norm, ple_w_gate, ple_b_gate, ple_w_proj, final_norm):
    bsz, seq, d = x.shape
    depth = p.shape[0]
    m = bsz * seq
    row = lambda t: t.reshape(1, -1).astype(F32)

    wa = ev_conv_w.shape[-1]
    rank = ev_gla_wgk.shape[1]
    wq = ev_gla_wgk.shape[2]
    wb = ev_gla_norm.shape[-1]
    dk, dv = wq // GLA_HEADS, wb // GLA_HEADS
    wc = od_ret_norm.shape[-1]
    wd = od_rwkv_w0.shape[-1]
    rw, ra, rg = od_rwkv_ww2.shape[1], od_rwkv_aw2.shape[1], od_rwkv_gw2.shape[1]

    h = x.reshape(m, d)
    freq = (ROPE_BASE ** (-jnp.arange(wc // RET_HEADS // 2, dtype=F32) / (wc // RET_HEADS // 2))).reshape(1, -1)
    pos = positions.astype(F32).reshape(bsz, seq, 1)
    seg = _block_diag(jnp.ones((wd // RWKV_HEAD_DIM, RWKV_HEAD_DIM, RWKV_HEAD_DIM), F32))

    for i in range(depth):
        j = i // 2
        if i % 2 == 0:
            w_in = ev_w_in[j]
            cols = w_in.shape[1]
            cols_pad = -(-cols // (3 * LANES)) * (3 * LANES)
            w_in = jnp.pad(w_in, ((0, 0), (0, cols_pad - cols))).astype(BF16)
            proj = _norm_matmul(h, row(mix_norm[i]), w_in, tm=512, tn=cols_pad // 3)
            wgate = jnp.concatenate([_block_diag(ev_lru_wr[j]), _block_diag(ev_lru_wi[j])], axis=1).astype(BF16)
            bgate = jnp.concatenate([ev_lru_br[j], ev_lru_bi[j]]).reshape(1, -1)
            y = _even_mixer(proj.reshape(bsz, seq, cols_pad), ev_conv_w[j], row(ev_conv_b[j]), wgate, bgate,
                            row(ev_lru_a[j]), ev_gla_wgk[j], row(ev_gla_bgk[j]), row(ev_gla_norm[j]),
                            ts=2 * MIX_CHUNK, wa=wa, dk=dk, dv=dv, rank=rank)
            w_out = ev_w_out[j]
        else:
            w_in = od_w_in[j].astype(BF16)
            proj = _norm_matmul(h, row(mix_norm[i]), w_in, tm=512, tn=w_in.shape[1] // 3)
            y = _odd_mixer(proj.reshape(bsz, seq, -1), pos, freq, row(od_ret_norm[j]), row(od_rwkv_mu[j]),
                           row(od_rwkv_w0[j]), od_rwkv_ww2[j].astype(BF16), row(od_rwkv_a0[j]),
                           od_rwkv_aw2[j].astype(BF16), od_rwkv_gw2[j].astype(BF16), row(od_rwkv_kk[j]),
                           row(od_rwkv_ka[j]), row(od_rwkv_rk[j]), row(od_rwkv_norm[j]), seg,
                           ts=RET_CHUNK, wc=wc, wd=wd, rw=rw, ra=ra, rg=rg)
            w_out = od_w_out[j]
        h = _post(h, y.reshape(m, -1), p[i].reshape(m, -1), w_out.astype(BF16), row(ffn_norm[i]),
                  ffn_w_gate[i].astype(BF16), ffn_w_up[i].astype(BF16), ffn_w_down[i].astype(BF16),
                  row(ple_norm[i]), ple_w_gate[i].astype(BF16), row(ple_b_gate[i]), ple_w_proj[i].astype(BF16),
                  row(final_norm), tm=256, final=(i == depth - 1))
    return h.reshape(bsz, seq, d)
```

```python
import functools
import math

import jax
import jax.numpy as jnp
from jax import lax
from jax.experimental import pallas as pl
from jax.experimental.pallas import tpu as pltpu

F32 = jnp.float32
BF16 = jnp.bfloat16
HIGHEST = lax.Precision.HIGHEST

NORM_EPS = 1e-6
LANES = 128
LRU_BLOCKS = 8
CONV_W = 4
LRU_C = 8.0
GLA_HEADS = 4
GLA_GATE_NORM = 16.0
GLA_SUB = 16
RET_HEADS = 4
RET_CHUNK = 128
ROPE_BASE = 10000.0
RWKV_HEAD_DIM = 64
RWKV_LN_EPS = 64e-5
HEAD_NORM_EPS = 1e-5

MIX_CHUNK = 64
VMEM_LIMIT = 56 * 1024 * 1024


def _log_sigmoid(x):
    return jnp.minimum(x, 0.0) - jnp.log1p(jnp.exp(-jnp.abs(x)))


def _softplus(x):
    return jnp.maximum(x, 0.0) + jnp.log1p(jnp.exp(-jnp.abs(x)))


def _silu(x):
    return x * jax.nn.sigmoid(x)


def _gelu_tanh(x):
    return 0.5 * x * (1.0 + jnp.tanh(math.sqrt(2.0 / math.pi) * (x + 0.044715 * (x * x * x))))


def _rms(x, g):
    return x * lax.rsqrt(jnp.mean(x * x, axis=-1, keepdims=True) + NORM_EPS) * g


def _dot(a, b, precision=None):
    return jnp.dot(a, b, preferred_element_type=F32, precision=precision)


def _dot_nt(a, b, precision=None):
    return lax.dot_general(a, b, (((1,), (1,)), ((), ())), preferred_element_type=F32, precision=precision)


def _dot_tn(a, b, precision=None):
    return lax.dot_general(a, b, (((0,), (0,)), ((), ())), preferred_element_type=F32, precision=precision)


def _dot_split_lhs(x, m, passes):
    acc = None
    for _ in range(passes):
        piece = x.astype(BF16)
        acc = _dot(piece, m) if acc is None else acc + _dot(piece, m)
        x = x - piece.astype(F32)
    return acc


def _dot_split_rhs(m, x, passes):
    acc = None
    for _ in range(passes):
        piece = x.astype(BF16)
        acc = _dot(m, piece) if acc is None else acc + _dot(m, piece)
        x = x - piece.astype(F32)
    return acc


def _tri(n, strict=False):
    row = lax.broadcasted_iota(jnp.int32, (n, n), 0)
    col = lax.broadcasted_iota(jnp.int32, (n, n), 1)
    return (row > col) if strict else (row >= col)


def _norm_matmul_kernel(x_ref, g_ref, w_ref, o_ref, xn_ref):
    @pl.when(pl.program_id(1) == 0)
    def _():
        xn_ref[...] = _rms(x_ref[...], g_ref[...]).astype(BF16)

    o_ref[...] = _dot(xn_ref[...], w_ref[...])


def _norm_matmul(x, g, w, *, tm, tn):
    m, d = x.shape
    n = w.shape[1]
    return pl.pallas_call(
        _norm_matmul_kernel,
        grid=(m // tm, n // tn),
        in_specs=[pl.BlockSpec((tm, d), lambda i, j: (i, 0)),
                  pl.BlockSpec((1, d), lambda i, j: (0, 0)),
                  pl.BlockSpec((d, tn), lambda i, j: (0, j))],
        out_specs=pl.BlockSpec((tm, tn), lambda i, j: (i, j)),
        out_shape=jax.ShapeDtypeStruct((m, n), F32),
        scratch_shapes=[pltpu.VMEM((tm, d), BF16)],
        compiler_params=pltpu.CompilerParams(dimension_semantics=("arbitrary", "arbitrary"),
                                             vmem_limit_bytes=VMEM_LIMIT),
        name="norm_in_proj",
    )(x, g, w)


def _even_mixer_kernel(proj_ref, convw_ref, convb_ref, wgate_ref, bgate_ref, lrua_ref, wgk_ref, bgk_ref, gnorm_ref,
                       y_ref, xtail_ref, hcar_ref, st_ref, *, ts, wa, dk, dv, rank):
    heads = GLA_HEADS
    wq = heads * dk
    wb = heads * dv
    o_ax, o_gate, o_q = 0, wa, 2 * wa
    o_k, o_v = o_q + wq, o_q + 2 * wq
    o_g, o_gk = o_v + wb, o_v + 2 * wb

    @pl.when(pl.program_id(1) == 0)
    def _():
        xtail_ref[...] = jnp.zeros_like(xtail_ref)
        hcar_ref[...] = jnp.zeros_like(hcar_ref)
        st_ref[...] = jnp.zeros_like(st_ref)

    ax = proj_ref[:, o_ax:o_ax + wa]
    xcat = jnp.concatenate([xtail_ref[...], ax], axis=0)
    xtail_ref[...] = ax[ts - 8:ts]
    cw = convw_ref[...]
    xc = convb_ref[...] + cw[CONV_W - 1:CONV_W] * ax
    for s in range(1, CONV_W):
        xc = xc + cw[CONV_W - 1 - s:CONV_W - s] * xcat[8 - s:8 - s + ts]
    gates = _dot(xc.astype(BF16), wgate_ref[...]) + bgate_ref[...]
    r = jax.nn.sigmoid(gates[:, :wa])
    i = jax.nn.sigmoid(gates[:, wa:])
    log_a = LRU_C * r * _log_sigmoid(lrua_ref[...])
    a = jnp.exp(log_a)
    t = jnp.tanh(log_a)
    u = jnp.sqrt(-2.0 * t / (1.0 - t)) * (i * xc)
    row = lax.broadcasted_iota(jnp.int32, (ts, wa), 0)
    d = 1
    while d < ts:
        keep = row >= d
        u = jnp.where(keep, a * pltpu.roll(u, d, 0) + u, u)
        a = jnp.where(keep, a * pltpu.roll(a, d, 0), a)
        d *= 2
    h = u + a * hcar_ref[...]
    hcar_ref[...] = h[ts - 1:ts]
    y_ref[:, 0:wa] = (h * _gelu_tanh(proj_ref[:, o_gate:o_gate + wa])).astype(y_ref.dtype)

    cg = MIX_CHUNK
    z = _dot(proj_ref[:, o_gk:o_gk + rank], wgk_ref[...], precision=HIGHEST) + bgk_ref[...]
    log_f = _log_sigmoid(z) * (1.0 / GLA_GATE_NORM)
    tri_incl = jnp.where(_tri(cg), 1.0, 0.0).astype(BF16)
    causal = _tri(cg)
    rowc = lax.broadcasted_iota(jnp.int32, (cg, wq), 0)
    for c in range(ts // cg):
        r0 = c * cg
        cum = _dot_split_rhs(tri_incl, log_f[r0:r0 + cg], 3)
        q = proj_ref[r0:r0 + cg, o_q:o_q + wq] * (dk ** -0.5)
        k = proj_ref[r0:r0 + cg, o_k:o_k + wq]
        v = proj_ref[r0:r0 + cg, o_v:o_v + wb].astype(BF16)
        cum_last = cum[cg - 1:cg]
        q_in = (q * jnp.exp(cum)).astype(BF16)
        k_out = (k * jnp.exp(cum_last - cum)).astype(BF16)
        chunk_decay = jnp.exp(cum_last)
        score_rows = [[] for _ in range(heads)]
        for ib in range(cg // GLA_SUB):
            b0 = ib * GLA_SUB
            base = cum[b0 - 1:b0] if ib > 0 else jnp.zeros((1, wq), F32)
            qb = (q[b0:b0 + GLA_SUB] * jnp.exp(cum[b0:b0 + GLA_SUB] - base)).astype(BF16)
            kb = jnp.where(rowc < b0 + GLA_SUB, k * jnp.exp(base - cum), 0.0).astype(BF16)
            for hd in range(heads):
                score_rows[hd].append(_dot_nt(qb[:, hd * dk:(hd + 1) * dk], kb[:, hd * dk:(hd + 1) * dk]))
        outs = []
        for hd in range(heads):
            ks = slice(hd * dk, (hd + 1) * dk)
            vs = slice(hd * dv, (hd + 1) * dv)
            scores = jnp.where(causal, jnp.concatenate(score_rows[hd], axis=0), 0.0).astype(BF16)
            st = st_ref[hd]
            o = _dot(scores, v[:, vs]) + _dot_nt(q_in[:, ks], st.astype(BF16))
            st_ref[hd] = st * chunk_decay[:, ks] + _dot_tn(v[:, vs], k_out[:, ks])
            o = o * lax.rsqrt(jnp.mean(o * o, axis=-1, keepdims=True) + HEAD_NORM_EPS)
            outs.append(o)
        o_all = jnp.concatenate(outs, axis=-1) * gnorm_ref[...]
        g = proj_ref[r0:r0 + cg, o_g:o_g + wb]
        y_ref[r0:r0 + cg, wa:wa + wb] = (o_all * _silu(g)).astype(y_ref.dtype)


def _even_mixer(proj, convw, convb, wgate, bgate, lrua, wgk, bgk, gnorm, *, ts, wa, dk, dv, rank):
    bsz, seq, cols = proj.shape
    heads = GLA_HEADS
    wb = heads * dv
    const = lambda shape: pl.BlockSpec(shape, lambda b, s: (0,) * len(shape))
    kern = functools.partial(_even_mixer_kernel, ts=ts, wa=wa, dk=dk, dv=dv, rank=rank)
    return pl.pallas_call(
        kern,
        grid=(bsz, seq // ts),
        in_specs=[pl.BlockSpec((None, ts, cols), lambda b, s: (b, s, 0)),
                  const(convw.shape), const(convb.shape), const(wgate.shape), const(bgate.shape),
                  const(lrua.shape), const(wgk.shape), const(bgk.shape), const(gnorm.shape)],
        out_specs=pl.BlockSpec((None, ts, wa + wb), lambda b, s: (b, s, 0)),
        out_shape=jax.ShapeDtypeStruct((bsz, seq, wa + wb), BF16),
        scratch_shapes=[pltpu.VMEM((8, wa), F32), pltpu.VMEM((1, wa), F32), pltpu.VMEM((heads, dv, dk), F32)],
        compiler_params=pltpu.CompilerParams(dimension_semantics=("arbitrary", "arbitrary"),
                                             vmem_limit_bytes=VMEM_LIMIT),
        name="even_mixer",
    )(proj, convw, convb, wgate, bgate, lrua, wgk, bgk, gnorm)


def _odd_mixer_kernel(proj_ref, pos_ref, freq_ref, rnorm_ref, mu_ref, w0_ref, ww2_ref, a0_ref, aw2_ref, gw2_ref,
                      kk_ref, ka_ref, rk_ref, wnorm_ref, seg_ref,
                      y_ref, rstate_ref, dprev_ref, wstate_ref, *, ts, wc, wd, rw, ra, rg):
    f32 = F32

    @pl.when(pl.program_id(1) == 0)
    def _():
        rstate_ref[...] = jnp.zeros_like(rstate_ref)
        dprev_ref[...] = jnp.zeros_like(dprev_ref)
        wstate_ref[...] = jnp.zeros_like(wstate_ref)

    c = ts
    dh = wc // RET_HEADS
    half = dh // 2
    ang = pos_ref[...] * freq_ref[...]
    cos, sin = jnp.cos(ang), jnp.sin(ang)
    cos2 = jnp.concatenate([cos, cos], axis=-1)
    sin2 = jnp.concatenate([-sin, sin], axis=-1)
    rowi = lax.broadcasted_iota(jnp.int32, (c, c), 0)
    coli = lax.broadcasted_iota(jnp.int32, (c, c), 1)
    rel = (rowi - coli).astype(f32)
    rowd = lax.broadcasted_iota(jnp.int32, (c, dh), 0).astype(f32)
    outs = []
    for hd in range(RET_HEADS):
        log_gamma = math.log1p(-2.0 ** (-5.0 - hd))
        hs = slice(hd * dh, (hd + 1) * dh)
        q = proj_ref[:, hd * dh:(hd + 1) * dh]
        k = proj_ref[:, wc + hd * dh:wc + (hd + 1) * dh]
        v = proj_ref[:, 2 * wc + hd * dh:2 * wc + (hd + 1) * dh].astype(BF16)
        q = (q * cos2 + pltpu.roll(q, half, 1) * sin2) * (dh ** -0.5)
        k = k * cos2 + pltpu.roll(k, half, 1) * sin2
        dmask = jnp.where(rel >= 0.0, jnp.exp(jnp.maximum(rel, 0.0) * log_gamma), 0.0)
        scores = (_dot_nt(q.astype(BF16), k.astype(BF16)) * dmask).astype(BF16)
        q_in = (q * jnp.exp((rowd + 1.0) * log_gamma)).astype(BF16)
        k_out = (k * jnp.exp((c - 1.0 - rowd) * log_gamma)).astype(BF16)
        st = rstate_ref[hd]
        o = _dot(scores, v) + _dot(q_in, st.astype(BF16))
        rstate_ref[hd] = st * math.exp(c * log_gamma) + _dot_tn(k_out, v)
        o = o - jnp.mean(o, axis=-1, keepdims=True)
        o = o * lax.rsqrt(jnp.mean(o * o, axis=-1, keepdims=True) + HEAD_NORM_EPS)
        outs.append(o)
    g = proj_ref[:, 3 * wc:4 * wc]
    y_ref[:, 0:wc] = (jnp.concatenate(outs, axis=-1) * rnorm_ref[...] * _silu(g)).astype(y_ref.dtype)

    o_d = 4 * wc
    dcols = 3 * wd + rw + ra + rg
    dpart = proj_ref[:, o_d:o_d + dcols]
    dshift = jnp.concatenate([dprev_ref[...], dpart], axis=0)[7:7 + ts]
    dprev_ref[...] = dpart[ts - 8:ts]
    dpart = dpart + mu_ref[...] * (dshift - dpart)
    r = dpart[:, 0:wd]
    k = dpart[:, wd:2 * wd]
    v = dpart[:, 2 * wd:3 * wd]
    w_lr = dpart[:, 3 * wd:3 * wd + rw]
    a_lr = dpart[:, 3 * wd + rw:3 * wd + rw + ra]
    g_lr = dpart[:, 3 * wd + rw + ra:dcols]
    w = -_softplus(-(w0_ref[...] + _dot(jnp.tanh(w_lr).astype(BF16), ww2_ref[...]))) - 0.5
    log_decay = -jnp.exp(w)
    a = jax.nn.sigmoid(a0_ref[...] + _dot(a_lr.astype(BF16), aw2_ref[...]))
    g = _dot(jax.nn.sigmoid(g_lr).astype(BF16), gw2_ref[...])
    seg = seg_ref[...]
    kk = k * kk_ref[...]
    kk = kk * lax.rsqrt(_dot_split_lhs(kk * kk, seg, 2) + 1e-12)
    k = k * (1.0 + (a - 1.0) * ka_ref[...])
    bonus = _dot_split_lhs(r * k * rk_ref[...], seg, 2) * v
    kb = kk * a
    hdim = RWKV_HEAD_DIM
    cm = MIX_CHUNK
    n = 2 * cm
    lane = lax.broadcasted_iota(jnp.int32, (cm, LANES), 1)
    first_head = lane < hdim

    def stack_heads(t):
        return jnp.concatenate([jnp.where(first_head, t, 0.0), jnp.where(first_head, 0.0, t)], axis=0)

    rin = lax.broadcasted_iota(jnp.int32, (n, n), 0)
    cin = lax.broadcasted_iota(jnp.int32, (n, n), 1)
    strict = (rin & (cm - 1)) > (cin & (cm - 1))
    incl = (rin & (cm - 1)) >= (cin & (cm - 1))
    eye = jnp.where(rin == cin, 1.0, 0.0).astype(f32)
    tri_incl = jnp.where(_tri(cm), 1.0, 0.0).astype(BF16)
    for ch in range(ts // cm):
        rs = slice(ch * cm, (ch + 1) * cm)
        ld = log_decay[rs]
        cum = _dot_split_rhs(tri_incl, ld, 3)
        cum_last = cum[cm - 1:cm]
        p_incl = jnp.exp(cum)
        p_inv = jnp.exp(-cum)
        a_t = -kk[rs] * jnp.exp(cum - ld)
        r_t = r[rs] * p_incl
        b_t = kb[rs] * p_inv
        k_t = k[rs] * p_inv
        p_out = jnp.exp(cum_last - cum)
        b_o = kb[rs] * p_out
        k_o = k[rs] * p_out
        p_last = jnp.exp(cum_last)
        ys = []
        for pr in range(wd // LANES):
            ls = slice(pr * LANES, (pr + 1) * LANES)
            l4 = jnp.concatenate([stack_heads(a_t[:, ls]), stack_heads(r_t[:, ls])], axis=0).astype(BF16)
            r4 = jnp.concatenate([stack_heads(b_t[:, ls]), stack_heads(k_t[:, ls])], axis=0).astype(BF16)
            o4 = jnp.concatenate([stack_heads(b_o[:, ls]), stack_heads(k_o[:, ls])], axis=0).astype(BF16)
            vbd = stack_heads(v[rs, ls]).astype(BF16)
            big = _dot_nt(l4, r4)
            a_ab = jnp.where(strict, big[:n, :n], 0.0)
            a_ak = jnp.where(strict, big[:n, n:], 0.0).astype(BF16)
            a_r = jnp.concatenate([jnp.where(incl, big[n:, :n], 0.0), jnp.where(incl, big[n:, n:], 0.0)],
                                  axis=1).astype(BF16)
            x = a_ab.astype(BF16)
            tinv = eye + a_ab
            span = 2
            while span < cm:
                x = _dot(x, x).astype(BF16)
                tinv = tinv + _dot(tinv.astype(BF16), x)
                span *= 2
            st = wstate_ref[pr]
            from_state = _dot_nt(l4, st.astype(BF16))
            u = _dot(tinv.astype(BF16), (from_state[:n] + _dot(a_ak, vbd)).astype(BF16))
            uv = jnp.concatenate([u.astype(BF16), vbd], axis=0)
            ybd = from_state[n:] + _dot(a_r, uv)
            wstate_ref[pr] = st * p_last[:, ls] + _dot_tn(uv, o4)
            ys.append(ybd[:cm] + ybd[cm:])
        y = jnp.concatenate(ys, axis=-1)
        inv_h = 1.0 / hdim
        y = y - _dot_split_lhs(y, seg, 2) * inv_h
        y = y * lax.rsqrt(_dot_split_lhs(y * y, seg, 2) * inv_h + RWKV_LN_EPS)
        y = (y * wnorm_ref[...] + bonus[rs]) * g[rs]
        y_ref[ch * cm:(ch + 1) * cm, wc:wc + wd] = y.astype(y_ref.dtype)


def _odd_mixer(proj, pos, freq, rnorm, mu, w0, ww2, a0, aw2, gw2, kk, ka, rk, wnorm, seg, *, ts, wc, wd, rw, ra, rg):
    bsz, seq, cols = proj.shape
    dh = wc // RET_HEADS
    const = lambda shape: pl.BlockSpec(shape, lambda b, s: (0,) * len(shape))
    smalls = (freq, rnorm, mu, w0, ww2, a0, aw2, gw2, kk, ka, rk, wnorm, seg)
    kern = functools.partial(_odd_mixer_kernel, ts=ts, wc=wc, wd=wd, rw=rw, ra=ra, rg=rg)
    dcols = 3 * wd + rw + ra + rg
    return pl.pallas_call(
        kern,
        grid=(bsz, seq // ts),
        in_specs=[pl.BlockSpec((None, ts, cols), lambda b, s: (b, s, 0)),
                  pl.BlockSpec((None, ts, 1), lambda b, s: (b, s, 0))] + [const(t.shape) for t in smalls],
        out_specs=pl.BlockSpec((None, ts, wc + wd), lambda b, s: (b, s, 0)),
        out_shape=jax.ShapeDtypeStruct((bsz, seq, wc + wd), BF16),
        scratch_shapes=[pltpu.VMEM((RET_HEADS, dh, dh), F32), pltpu.VMEM((8, dcols), F32),
                        pltpu.VMEM((wd // LANES, LANES, LANES), F32)],
        compiler_params=pltpu.CompilerParams(dimension_semantics=("arbitrary", "arbitrary"),
                                             vmem_limit_bytes=VMEM_LIMIT),
        name="odd_mixer",
    )(proj, pos, *smalls)


def _post_kernel(h_ref, y_ref, p_ref, wout_ref, fnorm_ref, wg_ref, wu_ref, wd_ref, pnorm_ref, pwg_ref, pbg_ref,
                 pwp_ref, onorm_ref, o_ref, *, final):
    h = h_ref[...] + _dot(y_ref[...], wout_ref[...])
    xn = _rms(h, fnorm_ref[...]).astype(BF16)
    act = (_silu(_dot(xn, wg_ref[...])) * _dot(xn, wu_ref[...])).astype(BF16)
    h = h + _dot(act, wd_ref[...])
    xg = _rms(h, pnorm_ref[...]).astype(BF16)
    gate = jax.nn.sigmoid(_dot(xg, pwg_ref[...]) + pbg_ref[...])
    h = h + gate * _dot(p_ref[...].astype(BF16), pwp_ref[...])
    if final:
        h = _rms(h, onorm_ref[...])
    o_ref[...] = h


def _post(h, y, p, wout, fnorm, wg, wu, wd, pnorm, pwg, pbg, pwp, onorm, *, tm, final):
    m, d = h.shape
    rows = lambda width: pl.BlockSpec((tm, width), lambda i: (i, 0))
    resident = lambda t: pl.BlockSpec(t.shape, lambda i: (0, 0), pipeline_mode=pl.Buffered(1))
    weights = (wout, fnorm, wg, wu, wd, pnorm, pwg, pbg, pwp, onorm)
    return pl.pallas_call(
        functools.partial(_post_kernel, final=final),
        grid=(m // tm,),
        in_specs=[rows(d), rows(y.shape[1]), rows(p.shape[1])] + [resident(t) for t in weights],
        out_specs=rows(d),
        out_shape=jax.ShapeDtypeStruct((m, d), F32),
        compiler_params=pltpu.CompilerParams(dimension_semantics=("arbitrary",), vmem_limit_bytes=VMEM_LIMIT),
        name="post_mixer",
    )(h, y, p, *weights)


def _block_diag(blocks):
    g, bi, bj = blocks.shape
    eye = jnp.eye(g, dtype=blocks.dtype)
    return (eye[:, None, :, None] * blocks[:, :, None, :]).reshape(g * bi, g * bj)


def kernel(x, p, positions, ev_w_in, ev_conv_w, ev_conv_b, ev_lru_wr, ev_lru_br, ev_lru_wi, ev_lru_bi, ev_lru_a, ev_gla_wgk, ev_gla_bgk, ev_gla_norm, ev_w_out, od_w_in, od_ret_norm, od_rwkv_mu, od_rwkv_w0, od_rwkv_ww2, od_rwkv_a0, od_rwkv_aw2, od_rwkv_gw2, od_rwkv_kk, od_rwkv_ka, od_rwkv_rk, od_rwkv_norm, od_w_out, mix_norm, ffn_norm, ffn_w_gate, ffn_w_up, ffn_w_down, ple_norm, ple_w_gate, ple_b_gate, ple_w_proj, final_norm):
    bsz, seq, d = x.shape
    depth = p.shape[0]
    m = bsz * seq
    row = lambda t: t.reshape(1, -1).astype(F32)

    wa = ev_conv_w.shape[-1]
    rank = ev_gla_wgk.shape[1]
    wq = ev_gla_wgk.shape[2]
    wb = ev_gla_norm.shape[-1]
    dk, dv = wq // GLA_HEADS, wb // GLA_HEADS
    wc = od_ret_norm.shape[-1]
    wd = od_rwkv_w0.shape[-1]
    rw, ra, rg = od_rwkv_ww2.shape[1], od_rwkv_aw2.shape[1], od_rwkv_gw2.shape[1]

    h = x.reshape(m, d)
    freq = (ROPE_BASE ** (-jnp.arange(wc // RET_HEADS // 2, dtype=F32) / (wc // RET_HEADS // 2))).reshape(1, -1)
    pos = positions.astype(F32).reshape(bsz, seq, 1)
    seg = _block_diag(jnp.ones((wd // RWKV_HEAD_DIM, RWKV_HEAD_DIM, RWKV_HEAD_DIM), BF16))
    assert 2 * RWKV_HEAD_DIM == LANES and wd % LANES == 0

    for i in range(depth):
        j = i // 2
        if i % 2 == 0:
            w_in = ev_w_in[j]
            cols = w_in.shape[1]
            cols_pad = -(-cols // (3 * LANES)) * (3 * LANES)
            w_in = jnp.pad(w_in, ((0, 0), (0, cols_pad - cols))).astype(BF16)
            proj = _norm_matmul(h, row(mix_norm[i]), w_in, tm=512, tn=cols_pad // 3)
            wgate = jnp.concatenate([_block_diag(ev_lru_wr[j]), _block_diag(ev_lru_wi[j])], axis=1).astype(BF16)
            bgate = jnp.concatenate([ev_lru_br[j], ev_lru_bi[j]]).reshape(1, -1)
            y = _even_mixer(proj.reshape(bsz, seq, cols_pad), ev_conv_w[j], row(ev_conv_b[j]), wgate, bgate,
                            row(ev_lru_a[j]), ev_gla_wgk[j], row(ev_gla_bgk[j]), row(ev_gla_norm[j]),
                            ts=2 * MIX_CHUNK, wa=wa, dk=dk, dv=dv, rank=rank)
            w_out = ev_w_out[j]
        else:
            w_in = od_w_in[j].astype(BF16)
            proj = _norm_matmul(h, row(mix_norm[i]), w_in, tm=512, tn=w_in.shape[1] // 3)
            y = _odd_mixer(proj.reshape(bsz, seq, -1), pos, freq, row(od_ret_norm[j]), row(od_rwkv_mu[j]),
                           row(od_rwkv_w0[j]), od_rwkv_ww2[j].astype(BF16), row(od_rwkv_a0[j]),
                           od_rwkv_aw2[j].astype(BF16), od_rwkv_gw2[j].astype(BF16), row(od_rwkv_kk[j]),
                           row(od_rwkv_ka[j]), row(od_rwkv_rk[j]), row(od_rwkv_norm[j]), seg,
                           ts=RET_CHUNK, wc=wc, wd=wd, rw=rw, ra=ra, rg=rg)
            w_out = od_w_out[j]
        h = _post(h, y.reshape(m, -1), p[i].reshape(m, -1), w_out.astype(BF16), row(ffn_norm[i]),
                  ffn_w_gate[i].astype(BF16), ffn_w_up[i].astype(BF16), ffn_w_down[i].astype(BF16),
                  row(ple_norm[i]), ple_w_gate[i].astype(BF16), row(ple_b_gate[i]), ple_w_proj[i].astype(BF16),
                  row(final_norm), tm=256, final=(i == depth - 1))
    return h.reshape(bsz, seq, d)
```

```python
import functools
import math

import jax
import jax.numpy as jnp
from jax import lax
from jax.experimental import pallas as pl
from jax.experimental.pallas import tpu as pltpu

F32 = jnp.float32
BF16 = jnp.bfloat16
HIGHEST = lax.Precision.HIGHEST

NORM_EPS = 1e-6
LANES = 128
LRU_BLOCKS = 8
CONV_W = 4
LRU_C = 8.0
GLA_HEADS = 4
GLA_GATE_NORM = 16.0
GLA_SUB = 16
RET_HEADS = 4
RET_CHUNK = 128
ROPE_BASE = 10000.0
RWKV_HEAD_DIM = 64
RWKV_LN_EPS = 64e-5
HEAD_NORM_EPS = 1e-5

MIX_CHUNK = 64
VMEM_LIMIT = 56 * 1024 * 1024


def _log_sigmoid(x):
    return jnp.minimum(x, 0.0) - jnp.log1p(jnp.exp(-jnp.abs(x)))


def _softplus(x):
    return jnp.maximum(x, 0.0) + jnp.log1p(jnp.exp(-jnp.abs(x)))


def _silu(x):
    return x * jax.nn.sigmoid(x)


def _gelu_tanh(x):
    return 0.5 * x * (1.0 + jnp.tanh(math.sqrt(2.0 / math.pi) * (x + 0.044715 * (x * x * x))))


def _rms(x, g):
    return x * lax.rsqrt(jnp.mean(x * x, axis=-1, keepdims=True) + NORM_EPS) * g


def _dot(a, b, precision=None):
    return jnp.dot(a, b, preferred_element_type=F32, precision=precision)


def _dot_nt(a, b, precision=None):
    return lax.dot_general(a, b, (((1,), (1,)), ((), ())), preferred_element_type=F32, precision=precision)


def _dot_tn(a, b, precision=None):
    return lax.dot_general(a, b, (((0,), (0,)), ((), ())), preferred_element_type=F32, precision=precision)


def _dot_split_lhs(x, m, passes):
    acc = None
    for _ in range(passes):
        piece = x.astype(BF16)
        acc = _dot(piece, m) if acc is None else acc + _dot(piece, m)
        x = x - piece.astype(F32)
    return acc


def _dot_split_rhs(m, x, passes):
    acc = None
    for _ in range(passes):
        piece = x.astype(BF16)
        acc = _dot(m, piece) if acc is None else acc + _dot(m, piece)
        x = x - piece.astype(F32)
    return acc


def _tri(n, strict=False):
    row = lax.broadcasted_iota(jnp.int32, (n, n), 0)
    col = lax.broadcasted_iota(jnp.int32, (n, n), 1)
    return (row > col) if strict else (row >= col)


PROJ_GROUP = 512


def _project(h_ref, mnorm_ref, win_ref, proj_ref):
    xn = _rms(h_ref[...], mnorm_ref[...]).astype(BF16)
    cols = proj_ref.shape[1]
    for c0 in range(0, cols, PROJ_GROUP):
        c1 = min(c0 + PROJ_GROUP, cols)
        proj_ref[:, c0:c1] = _dot(xn, win_ref[:, c0:c1])


def _even_mixer_kernel(h_ref, mnorm_ref, win_ref, convw_ref, convb_ref, wgate_ref, bgate_ref, lrua_ref, wgk_ref,
                       bgk_ref, gnorm_ref, y_ref, proj_ref, xtail_ref, hcar_ref, st_ref, *, ts, wa, dk, dv, rank):
    heads = GLA_HEADS
    wq = heads * dk
    wb = heads * dv
    o_ax, o_gate, o_q = 0, wa, 2 * wa
    o_k, o_v = o_q + wq, o_q + 2 * wq
    o_g, o_gk = o_v + wb, o_v + 2 * wb

    @pl.when(pl.program_id(1) == 0)
    def _():
        xtail_ref[...] = jnp.zeros_like(xtail_ref)
        hcar_ref[...] = jnp.zeros_like(hcar_ref)
        st_ref[...] = jnp.zeros_like(st_ref)

    _project(h_ref, mnorm_ref, win_ref, proj_ref)

    ax = proj_ref[:, o_ax:o_ax + wa]
    xcat = jnp.concatenate([xtail_ref[...], ax], axis=0)
    xtail_ref[...] = ax[ts - 8:ts]
    cw = convw_ref[...]
    xc = convb_ref[...] + cw[CONV_W - 1:CONV_W] * ax
    for s in range(1, CONV_W):
        xc = xc + cw[CONV_W - 1 - s:CONV_W - s] * xcat[8 - s:8 - s + ts]
    gates = _dot(xc.astype(BF16), wgate_ref[...]) + bgate_ref[...]
    r = jax.nn.sigmoid(gates[:, :wa])
    i = jax.nn.sigmoid(gates[:, wa:])
    log_a = LRU_C * r * _log_sigmoid(lrua_ref[...])
    a = jnp.exp(log_a)
    t = jnp.tanh(log_a)
    u = jnp.sqrt(-2.0 * t / (1.0 - t)) * (i * xc)
    row = lax.broadcasted_iota(jnp.int32, (ts, wa), 0)
    d = 1
    while d < ts:
        keep = row >= d
        u = jnp.where(keep, a * pltpu.roll(u, d, 0) + u, u)
        a = jnp.where(keep, a * pltpu.roll(a, d, 0), a)
        d *= 2
    h = u + a * hcar_ref[...]
    hcar_ref[...] = h[ts - 1:ts]
    y_ref[:, 0:wa] = (h * _gelu_tanh(proj_ref[:, o_gate:o_gate + wa])).astype(y_ref.dtype)

    cg = MIX_CHUNK
    z = _dot(proj_ref[:, o_gk:o_gk + rank], wgk_ref[...], precision=HIGHEST) + bgk_ref[...]
    log_f = _log_sigmoid(z) * (1.0 / GLA_GATE_NORM)
    tri_incl = jnp.where(_tri(cg), 1.0, 0.0).astype(BF16)
    causal = _tri(cg)
    rowc = lax.broadcasted_iota(jnp.int32, (cg, wq), 0)
    for c in range(ts // cg):
        r0 = c * cg
        cum = _dot_split_rhs(tri_incl, log_f[r0:r0 + cg], 3)
        q = proj_ref[r0:r0 + cg, o_q:o_q + wq] * (dk ** -0.5)
        k = proj_ref[r0:r0 + cg, o_k:o_k + wq]
        v = proj_ref[r0:r0 + cg, o_v:o_v + wb].astype(BF16)
        cum_last = cum[cg - 1:cg]
        q_in = (q * jnp.exp(cum)).astype(BF16)
        k_out = (k * jnp.exp(cum_last - cum)).astype(BF16)
        chunk_decay = jnp.exp(cum_last)
        score_rows = [[] for _ in range(heads)]
        for ib in range(cg // GLA_SUB):
            b0 = ib * GLA_SUB
            base = cum[b0 - 1:b0] if ib > 0 else jnp.zeros((1, wq), F32)
            qb = (q[b0:b0 + GLA_SUB] * jnp.exp(cum[b0:b0 + GLA_SUB] - base)).astype(BF16)
            kb = jnp.where(rowc < b0 + GLA_SUB, k * jnp.exp(base - cum), 0.0).astype(BF16)
            for hd in range(heads):
                score_rows[hd].append(_dot_nt(qb[:, hd * dk:(hd + 1) * dk], kb[:, hd * dk:(hd + 1) * dk]))
        outs = []
        for hd in range(heads):
            ks = slice(hd * dk, (hd + 1) * dk)
            vs = slice(hd * dv, (hd + 1) * dv)
            scores = jnp.where(causal, jnp.concatenate(score_rows[hd], axis=0), 0.0).astype(BF16)
            st = st_ref[hd]
            o = _dot(scores, v[:, vs]) + _dot_nt(q_in[:, ks], st.astype(BF16))
            st_ref[hd] = st * chunk_decay[:, ks] + _dot_tn(v[:, vs], k_out[:, ks])
            o = o * lax.rsqrt(jnp.mean(o * o, axis=-1, keepdims=True) + HEAD_NORM_EPS)
            outs.append(o)
        o_all = jnp.concatenate(outs, axis=-1) * gnorm_ref[...]
        g = proj_ref[r0:r0 + cg, o_g:o_g + wb]
        y_ref[r0:r0 + cg, wa:wa + wb] = (o_all * _silu(g)).astype(y_ref.dtype)


def _resident(t):
    return pl.BlockSpec(t.shape, lambda b, s: (0,) * t.ndim, pipeline_mode=pl.Buffered(1))


def _even_mixer(h, mnorm, win, convw, convb, wgate, bgate, lrua, wgk, bgk, gnorm, *, ts, wa, dk, dv, rank):
    bsz, seq, d = h.shape
    cols = win.shape[1]
    heads = GLA_HEADS
    wb = heads * dv
    consts = (mnorm, win, convw, convb, wgate, bgate, lrua, wgk, bgk, gnorm)
    kern = functools.partial(_even_mixer_kernel, ts=ts, wa=wa, dk=dk, dv=dv, rank=rank)
    return pl.pallas_call(
        kern,
        grid=(bsz, seq // ts),
        in_specs=[pl.BlockSpec((None, ts, d), lambda b, s: (b, s, 0))] + [_resident(t) for t in consts],
        out_specs=pl.BlockSpec((None, ts, wa + wb), lambda b, s: (b, s, 0)),
        out_shape=jax.ShapeDtypeStruct((bsz, seq, wa + wb), BF16),
        scratch_shapes=[pltpu.VMEM((ts, cols), F32), pltpu.VMEM((8, wa), F32), pltpu.VMEM((1, wa), F32),
                        pltpu.VMEM((heads, dv, dk), F32)],
        compiler_params=pltpu.CompilerParams(dimension_semantics=("arbitrary", "arbitrary"),
                                             vmem_limit_bytes=VMEM_LIMIT),
        name="even_mixer",
    )(h, *consts)


def _odd_mixer_kernel(h_ref, pos_ref, mnorm_ref, win_ref, freq_ref, rnorm_ref, mu_ref, w0_ref, ww2_ref, a0_ref, aw2_ref,
                      gw2_ref, kk_ref, ka_ref, rk_ref, wnorm_ref, seg_ref,
                      y_ref, proj_ref, rstate_ref, dprev_ref, wstate_ref, *, ts, wc, wd, rw, ra, rg):
    f32 = F32

    @pl.when(pl.program_id(1) == 0)
    def _():
        rstate_ref[...] = jnp.zeros_like(rstate_ref)
        dprev_ref[...] = jnp.zeros_like(dprev_ref)
        wstate_ref[...] = jnp.zeros_like(wstate_ref)

    _project(h_ref, mnorm_ref, win_ref, proj_ref)

    c = ts
    dh = wc // RET_HEADS
    half = dh // 2
    ang = pos_ref[...] * freq_ref[...]
    cos, sin = jnp.cos(ang), jnp.sin(ang)
    cos2 = jnp.concatenate([cos, cos], axis=-1)
    sin2 = jnp.concatenate([-sin, sin], axis=-1)
    rowi = lax.broadcasted_iota(jnp.int32, (c, c), 0)
    coli = lax.broadcasted_iota(jnp.int32, (c, c), 1)
    rel = (rowi - coli).astype(f32)
    rowd = lax.broadcasted_iota(jnp.int32, (c, dh), 0).astype(f32)
    outs = []
    for hd in range(RET_HEADS):
        log_gamma = math.log1p(-2.0 ** (-5.0 - hd))
        hs = slice(hd * dh, (hd + 1) * dh)
        q = proj_ref[:, hd * dh:(hd + 1) * dh]
        k = proj_ref[:, wc + hd * dh:wc + (hd + 1) * dh]
        v = proj_ref[:, 2 * wc + hd * dh:2 * wc + (hd + 1) * dh].astype(BF16)
        q = (q * cos2 + pltpu.roll(q, half, 1) * sin2) * (dh ** -0.5)
        k = k * cos2 + pltpu.roll(k, half, 1) * sin2
        dmask = jnp.where(rel >= 0.0, jnp.exp(jnp.maximum(rel, 0.0) * log_gamma), 0.0)
        scores = (_dot_nt(q.astype(BF16), k.astype(BF16)) * dmask).astype(BF16)
        q_in = (q * jnp.exp((rowd + 1.0) * log_gamma)).astype(BF16)
        k_out = (k * jnp.exp((c - 1.0 - rowd) * log_gamma)).astype(BF16)
        st = rstate_ref[hd]
        o = _dot(scores, v) + _dot(q_in, st.astype(BF16))
        rstate_ref[hd] = st * math.exp(c * log_gamma) + _dot_tn(k_out, v)
        o = o - jnp.mean(o, axis=-1, keepdims=True)
        o = o * lax.rsqrt(jnp.mean(o * o, axis=-1, keepdims=True) + HEAD_NORM_EPS)
        outs.append(o)
    g = proj_ref[:, 3 * wc:4 * wc]
    y_ref[:, 0:wc] = (jnp.concatenate(outs, axis=-1) * rnorm_ref[...] * _silu(g)).astype(y_ref.dtype)

    o_d = 4 * wc
    dcols = 3 * wd + rw + ra + rg
    dpart = proj_ref[:, o_d:o_d + dcols]
    dshift = jnp.concatenate([dprev_ref[...], dpart], axis=0)[7:7 + ts]
    dprev_ref[...] = dpart[ts - 8:ts]
    dpart = dpart + mu_ref[...] * (dshift - dpart)
    r = dpart[:, 0:wd]
    k = dpart[:, wd:2 * wd]
    v = dpart[:, 2 * wd:3 * wd]
    w_lr = dpart[:, 3 * wd:3 * wd + rw]
    a_lr = dpart[:, 3 * wd + rw:3 * wd + rw + ra]
    g_lr = dpart[:, 3 * wd + rw + ra:dcols]
    w = -_softplus(-(w0_ref[...] + _dot(jnp.tanh(w_lr).astype(BF16), ww2_ref[...]))) - 0.5
    log_decay = -jnp.exp(w)
    a = jax.nn.sigmoid(a0_ref[...] + _dot(a_lr.astype(BF16), aw2_ref[...]))
    g = _dot(jax.nn.sigmoid(g_lr).astype(BF16), gw2_ref[...])
    seg = seg_ref[...]
    kk = k * kk_ref[...]
    kk = kk * lax.rsqrt(_dot_split_lhs(kk * kk, seg, 2) + 1e-12)
    k = k * (1.0 + (a - 1.0) * ka_ref[...])
    bonus = _dot_split_lhs(r * k * rk_ref[...], seg, 2) * v
    kb = kk * a
    hdim = RWKV_HEAD_DIM
    cm = MIX_CHUNK
    n = 2 * cm
    lane = lax.broadcasted_iota(jnp.int32, (cm, LANES), 1)
    first_head = lane < hdim

    def stack_heads(t):
        return jnp.concatenate([jnp.where(first_head, t, 0.0), jnp.where(first_head, 0.0, t)], axis=0)

    rin = lax.broadcasted_iota(jnp.int32, (n, n), 0)
    cin = lax.broadcasted_iota(jnp.int32, (n, n), 1)
    strict = (rin & (cm - 1)) > (cin & (cm - 1))
    incl = (rin & (cm - 1)) >= (cin & (cm - 1))
    eye = jnp.where(rin == cin, 1.0, 0.0).astype(f32)
    tri_incl = jnp.where(_tri(cm), 1.0, 0.0).astype(BF16)
    groups = wd // LANES
    work = []
    for ch in range(ts // cm):
        rs = slice(ch * cm, (ch + 1) * cm)
        ld = log_decay[rs]
        cum = _dot_split_rhs(tri_incl, ld, 3)
        cum_last = cum[cm - 1:cm]
        p_incl = jnp.exp(cum)
        p_inv = jnp.exp(-cum)
        a_t = -kk[rs] * jnp.exp(cum - ld)
        r_t = r[rs] * p_incl
        b_t = kb[rs] * p_inv
        k_t = k[rs] * p_inv
        p_out = jnp.exp(cum_last - cum)
        b_o = kb[rs] * p_out
        k_o = k[rs] * p_out
        p_last = jnp.exp(cum_last)
        for pr in range(groups):
            ls = slice(pr * LANES, (pr + 1) * LANES)
            l4 = jnp.concatenate([stack_heads(a_t[:, ls]), stack_heads(r_t[:, ls])], axis=0).astype(BF16)
            r4 = jnp.concatenate([stack_heads(b_t[:, ls]), stack_heads(k_t[:, ls])], axis=0).astype(BF16)
            o4 = jnp.concatenate([stack_heads(b_o[:, ls]), stack_heads(k_o[:, ls])], axis=0).astype(BF16)
            vbd = stack_heads(v[rs, ls]).astype(BF16)
            work.append(dict(l4=l4, r4=r4, o4=o4, vbd=vbd, p_last=p_last[:, ls]))
    for wk in work:
        big = _dot_nt(wk["l4"], wk["r4"])
        a_ab = jnp.where(strict, big[:n, :n], 0.0)
        wk["a_ak"] = jnp.where(strict, big[:n, n:], 0.0).astype(BF16)
        wk["a_r"] = jnp.concatenate([jnp.where(incl, big[n:, :n], 0.0), jnp.where(incl, big[n:, n:], 0.0)],
                                    axis=1).astype(BF16)
        wk["x"] = a_ab.astype(BF16)
        wk["tinv"] = eye + a_ab
    span = 2
    while span < cm:
        for wk in work:
            wk["x"] = _dot(wk["x"], wk["x"]).astype(BF16)
        for wk in work:
            wk["tinv"] = wk["tinv"] + _dot(wk["tinv"].astype(BF16), wk["x"])
        span *= 2
    for ch in range(ts // cm):
        rs = slice(ch * cm, (ch + 1) * cm)
        wks = work[ch * groups:(ch + 1) * groups]
        sts = [wstate_ref[pr] for pr in range(groups)]
        fss = [_dot_nt(wk["l4"], st.astype(BF16)) for wk, st in zip(wks, sts)]
        rhs = [(fs[:n] + _dot(wk["a_ak"], wk["vbd"])).astype(BF16) for wk, fs in zip(wks, fss)]
        us = [_dot(wk["tinv"].astype(BF16), rh) for wk, rh in zip(wks, rhs)]
        uvs = [jnp.concatenate([u.astype(BF16), wk["vbd"]], axis=0) for wk, u in zip(wks, us)]
        for pr in range(groups):
            wstate_ref[pr] = sts[pr] * wks[pr]["p_last"] + _dot_tn(uvs[pr], wks[pr]["o4"])
        ys = []
        for wk, fs, uv in zip(wks, fss, uvs):
            ybd = fs[n:] + _dot(wk["a_r"], uv)
            ys.append(ybd[:cm] + ybd[cm:])
        y = jnp.concatenate(ys, axis=-1)
        inv_h = 1.0 / hdim
        y = y - _dot_split_lhs(y, seg, 2) * inv_h
        y = y * lax.rsqrt(_dot_split_lhs(y * y, seg, 2) * inv_h + RWKV_LN_EPS)
        y = (y * wnorm_ref[...] + bonus[rs]) * g[rs]
        y_ref[ch * cm:(ch + 1) * cm, wc:wc + wd] = y.astype(y_ref.dtype)


def _odd_mixer(h, pos, mnorm, win, freq, rnorm, mu, w0, ww2, a0, aw2, gw2, kk, ka, rk, wnorm, seg, *, ts, wc, wd, rw, ra,
               rg):
    bsz, seq, d = h.shape
    cols = win.shape[1]
    dh = wc // RET_HEADS
    consts = (mnorm, win, freq, rnorm, mu, w0, ww2, a0, aw2, gw2, kk, ka, rk, wnorm, seg)
    kern = functools.partial(_odd_mixer_kernel, ts=ts, wc=wc, wd=wd, rw=rw, ra=ra, rg=rg)
    dcols = 3 * wd + rw + ra + rg
    return pl.pallas_call(
        kern,
        grid=(bsz, seq // ts),
        in_specs=[pl.BlockSpec((None, ts, d), lambda b, s: (b, s, 0)),
                  pl.BlockSpec((None, ts, 1), lambda b, s: (b, s, 0))] + [_resident(t) for t in consts],
        out_specs=pl.BlockSpec((None, ts, wc + wd), lambda b, s: (b, s, 0)),
        out_shape=jax.ShapeDtypeStruct((bsz, seq, wc + wd), BF16),
        scratch_shapes=[pltpu.VMEM((ts, cols), F32), pltpu.VMEM((RET_HEADS, dh, dh), F32),
                        pltpu.VMEM((8, dcols), F32), pltpu.VMEM((wd // LANES, LANES, LANES), F32)],
        compiler_params=pltpu.CompilerParams(dimension_semantics=("arbitrary", "arbitrary"),
                                             vmem_limit_bytes=VMEM_LIMIT),
        name="odd_mixer",
    )(h, pos, *consts)


def _post_kernel(h_ref, y_ref, p_ref, wout_ref, fnorm_ref, wg_ref, wu_ref, wd_ref, pnorm_ref, pwg_ref, pbg_ref,
                 pwp_ref, onorm_ref, o_ref, *, final):
    h = h_ref[...] + _dot(y_ref[...], wout_ref[...])
    xn = _rms(h, fnorm_ref[...]).astype(BF16)
    act = (_silu(_dot(xn, wg_ref[...])) * _dot(xn, wu_ref[...])).astype(BF16)
    h = h + _dot(act, wd_ref[...])
    xg = _rms(h, pnorm_ref[...]).astype(BF16)
    gate = jax.nn.sigmoid(_dot(xg, pwg_ref[...]) + pbg_ref[...])
    h = h + gate * _dot(p_ref[...].astype(BF16), pwp_ref[...])
    if final:
        h = _rms(h, onorm_ref[...])
    o_ref[...] = h


def _post(h, y, p, wout, fnorm, wg, wu, wd, pnorm, pwg, pbg, pwp, onorm, *, tm, final):
    m, d = h.shape
    rows = lambda width: pl.BlockSpec((tm, width), lambda i: (i, 0))
    resident = lambda t: pl.BlockSpec(t.shape, lambda i: (0, 0), pipeline_mode=pl.Buffered(1))
    weights = (wout, fnorm, wg, wu, wd, pnorm, pwg, pbg, pwp, onorm)
    return pl.pallas_call(
        functools.partial(_post_kernel, final=final),
        grid=(m // tm,),
        in_specs=[rows(d), rows(y.shape[1]), rows(p.shape[1])] + [resident(t) for t in weights],
        out_specs=rows(d),
        out_shape=jax.ShapeDtypeStruct((m, d), F32),
        compiler_params=pltpu.CompilerParams(dimension_semantics=("arbitrary",), vmem_limit_bytes=VMEM_LIMIT),
        name="post_mixer",
    )(h, y, p, *weights)


def _block_diag(blocks):
    g, bi, bj = blocks.shape
    eye = jnp.eye(g, dtype=blocks.dtype)
    return (eye[:, None, :, None] * blocks[:, :, None, :]).reshape(g * bi, g * bj)


def kernel(x, p, positions, ev_w_in, ev_conv_w, ev_conv_b, ev_lru_wr, ev_lru_br, ev_lru_wi, ev_lru_bi, ev_lru_a, ev_gla_wgk, ev_gla_bgk, ev_gla_norm, ev_w_out, od_w_in, od_ret_norm, od_rwkv_mu, od_rwkv_w0, od_rwkv_ww2, od_rwkv_a0, od_rwkv_aw2, od_rwkv_gw2, od_rwkv_kk, od_rwkv_ka, od_rwkv_rk, od_rwkv_norm, od_w_out, mix_norm, ffn_norm, ffn_w_gate, ffn_w_up, ffn_w_down, ple_norm, ple_w_gate, ple_b_gate, ple_w_proj, final_norm):
    bsz, seq, d = x.shape
    depth = p.shape[0]
    m = bsz * seq
    row = lambda t: t.reshape(1, -1).astype(F32)

    wa = ev_conv_w.shape[-1]
    rank = ev_gla_wgk.shape[1]
    wq = ev_gla_wgk.shape[2]
    wb = ev_gla_norm.shape[-1]
    dk, dv = wq // GLA_HEADS, wb // GLA_HEADS
    wc = od_ret_norm.shape[-1]
    wd = od_rwkv_w0.shape[-1]
    rw, ra, rg = od_rwkv_ww2.shape[1], od_rwkv_aw2.shape[1], od_rwkv_gw2.shape[1]

    h = x.reshape(m, d)
    freq = (ROPE_BASE ** (-jnp.arange(wc // RET_HEADS // 2, dtype=F32) / (wc // RET_HEADS // 2))).reshape(1, -1)
    pos = positions.astype(F32).reshape(bsz, seq, 1)
    seg = _block_diag(jnp.ones((wd // RWKV_HEAD_DIM, RWKV_HEAD_DIM, RWKV_HEAD_DIM), BF16))
    assert 2 * RWKV_HEAD_DIM == LANES and wd % LANES == 0

    for i in range(depth):
        j = i // 2
        if i % 2 == 0:
            w_in = ev_w_in[j]
            cols = w_in.shape[1]
            cols_pad = -(-cols // LANES) * LANES
            w_in = jnp.pad(w_in, ((0, 0), (0, cols_pad - cols))).astype(BF16)
            wgate = jnp.concatenate([_block_diag(ev_lru_wr[j]), _block_diag(ev_lru_wi[j])], axis=1).astype(BF16)
            bgate = jnp.concatenate([ev_lru_br[j], ev_lru_bi[j]]).reshape(1, -1)
            y = _even_mixer(h.reshape(bsz, seq, d), row(mix_norm[i]), w_in, ev_conv_w[j], row(ev_conv_b[j]), wgate, bgate,
                            row(ev_lru_a[j]), ev_gla_wgk[j], row(ev_gla_bgk[j]), row(ev_gla_norm[j]),
                            ts=2 * MIX_CHUNK, wa=wa, dk=dk, dv=dv, rank=rank)
            w_out = ev_w_out[j]
        else:
            y = _odd_mixer(h.reshape(bsz, seq, d), pos, row(mix_norm[i]), od_w_in[j].astype(BF16), freq,
                           row(od_ret_norm[j]), row(od_rwkv_mu[j]),
                           row(od_rwkv_w0[j]), od_rwkv_ww2[j].astype(BF16), row(od_rwkv_a0[j]),
                           od_rwkv_aw2[j].astype(BF16), od_rwkv_gw2[j].astype(BF16), row(od_rwkv_kk[j]),
                           row(od_rwkv_ka[j]), row(od_rwkv_rk[j]), row(od_rwkv_norm[j]), seg,
                           ts=RET_CHUNK, wc=wc, wd=wd, rw=rw, ra=ra, rg=rg)
            w_out = od_w_out[j]
        h = _post(h, y.reshape(m, -1), p[i].reshape(m, -1), w_out.astype(BF16), row(ffn_norm[i]),
                  ffn_w_gate[i].astype(BF16), ffn_w_up[i].astype(BF16), ffn_w_down[i].astype(BF16),
                  row(ple_norm[i]), ple_w_gate[i].astype(BF16), row(ple_b_gate[i]), ple_w_proj[i].astype(BF16),
                  row(final_norm), tm=256, final=(i == depth - 1))
    return h.reshape(bsz, seq, d)
```

```python
import functools
import math

import jax
import jax.numpy as jnp
from jax import lax
from jax.experimental import pallas as pl
from jax.experimental.pallas import tpu as pltpu

F32 = jnp.float32
BF16 = jnp.bfloat16
HIGHEST = lax.Precision.HIGHEST

NORM_EPS = 1e-6
LANES = 128
LRU_BLOCKS = 8
CONV_W = 4
LRU_C = 8.0
GLA_HEADS = 4
GLA_GATE_NORM = 16.0
GLA_SUB = 16
RET_HEADS = 4
RET_CHUNK = 128
ROPE_BASE = 10000.0
RWKV_HEAD_DIM = 64
RWKV_LN_EPS = 64e-5
HEAD_NORM_EPS = 1e-5

MIX_CHUNK = 64
MIX_BATCH = 2
VMEM_LIMIT = 56 * 1024 * 1024


def _log_sigmoid(x):
    return jnp.minimum(x, 0.0) - jnp.log1p(jnp.exp(-jnp.abs(x)))


def _softplus(x):
    return jnp.maximum(x, 0.0) + jnp.log1p(jnp.exp(-jnp.abs(x)))


def _silu(x):
    return x * jax.nn.sigmoid(x)


def _gelu_tanh(x):
    return 0.5 * x * (1.0 + jnp.tanh(math.sqrt(2.0 / math.pi) * (x + 0.044715 * (x * x * x))))


def _rms(x, g):
    return x * lax.rsqrt(jnp.mean(x * x, axis=-1, keepdims=True) + NORM_EPS) * g


def _dot(a, b, precision=None):
    return jnp.dot(a, b, preferred_element_type=F32, precision=precision)


def _dot_nt(a, b, precision=None):
    return lax.dot_general(a, b, (((1,), (1,)), ((), ())), preferred_element_type=F32, precision=precision)


def _dot_tn(a, b, precision=None):
    return lax.dot_general(a, b, (((0,), (0,)), ((), ())), preferred_element_type=F32, precision=precision)


def _dot_split_lhs(x, m, passes):
    acc = None
    for _ in range(passes):
        piece = x.astype(BF16)
        acc = _dot(piece, m) if acc is None else acc + _dot(piece, m)
        x = x - piece.astype(F32)
    return acc


def _dot_split_rhs(m, x, passes):
    acc = None
    for _ in range(passes):
        piece = x.astype(BF16)
        acc = _dot(m, piece) if acc is None else acc + _dot(m, piece)
        x = x - piece.astype(F32)
    return acc


def _tri(n, strict=False):
    row = lax.broadcasted_iota(jnp.int32, (n, n), 0)
    col = lax.broadcasted_iota(jnp.int32, (n, n), 1)
    return (row > col) if strict else (row >= col)


PROJ_GROUP = 512


def _project(h_ref, mnorm_ref, win_ref, proj_ref):
    xn = _rms(h_ref[...], mnorm_ref[...]).astype(BF16)
    cols = proj_ref.shape[1]
    for c0 in range(0, cols, PROJ_GROUP):
        c1 = min(c0 + PROJ_GROUP, cols)
        proj_ref[:, c0:c1] = _dot(xn, win_ref[:, c0:c1])


def _resident(t):
    return pl.BlockSpec(t.shape, lambda b, s: (0,) * t.ndim, pipeline_mode=pl.Buffered(1))


def _rg_lru(proj_ref, y_ref, xtail_ref, hcar_ref, convw_ref, convb_ref, wgate_ref, bgate_ref, lrua_ref, *, ts, wa):
    ax = proj_ref[:, 0:wa]
    xcat = jnp.concatenate([xtail_ref[...], ax], axis=0)
    xtail_ref[...] = ax[ts - 8:ts]
    cw = convw_ref[...]
    xc = convb_ref[...] + cw[CONV_W - 1:CONV_W] * ax
    for s in range(1, CONV_W):
        xc = xc + cw[CONV_W - 1 - s:CONV_W - s] * xcat[8 - s:8 - s + ts]
    gates = _dot(xc.astype(BF16), wgate_ref[...]) + bgate_ref[...]
    r = jax.nn.sigmoid(gates[:, :wa])
    i = jax.nn.sigmoid(gates[:, wa:])
    log_a = LRU_C * r * _log_sigmoid(lrua_ref[...])
    a = jnp.exp(log_a)
    t = jnp.tanh(log_a)
    u = jnp.sqrt(-2.0 * t / (1.0 - t)) * (i * xc)
    row = lax.broadcasted_iota(jnp.int32, (ts, wa), 0)
    d = 1
    while d < ts:
        keep = row >= d
        u = jnp.where(keep, a * pltpu.roll(u, d, 0) + u, u)
        a = jnp.where(keep, a * pltpu.roll(a, d, 0), a)
        d *= 2
    h = u + a * hcar_ref[...]
    hcar_ref[...] = h[ts - 1:ts]
    y_ref[:, 0:wa] = (h * _gelu_tanh(proj_ref[:, wa:2 * wa])).astype(y_ref.dtype)


def _even_mixer_kernel(h_ref, mnorm_ref, win_ref, convw_ref, convb_ref, wgate_ref, bgate_ref, lrua_ref, wgk_ref,
                       bgk_ref, gnorm_ref, y_ref, proj_ref, xtail_ref, hcar_ref, st_ref, *, nb, ts, wa, dk, dv, rank):
    heads = GLA_HEADS
    wq = heads * dk
    wb = heads * dv
    o_q = 2 * wa
    o_k, o_v = o_q + wq, o_q + 2 * wq
    o_g, o_gk = o_v + wb, o_v + 2 * wb

    @pl.when(pl.program_id(1) == 0)
    def _():
        xtail_ref[...] = jnp.zeros_like(xtail_ref)
        hcar_ref[...] = jnp.zeros_like(hcar_ref)
        st_ref[...] = jnp.zeros_like(st_ref)

    for b in range(nb):
        _project(h_ref.at[b], mnorm_ref, win_ref, proj_ref.at[b])
    for b in range(nb):
        _rg_lru(proj_ref.at[b], y_ref.at[b], xtail_ref.at[b], hcar_ref.at[b], convw_ref, convb_ref, wgate_ref,
                bgate_ref, lrua_ref, ts=ts, wa=wa)

    cg = MIX_CHUNK
    tri_incl = jnp.where(_tri(cg), 1.0, 0.0).astype(BF16)
    causal = _tri(cg)
    rowc = lax.broadcasted_iota(jnp.int32, (cg, wq), 0)
    items = []
    for b in range(nb):
        z = _dot(proj_ref[b, :, o_gk:o_gk + rank], wgk_ref[...], precision=HIGHEST) + bgk_ref[...]
        log_f = _log_sigmoid(z) * (1.0 / GLA_GATE_NORM)
        for c in range(ts // cg):
            r0 = c * cg
            cum = _dot_split_rhs(tri_incl, log_f[r0:r0 + cg], 3)
            q = proj_ref[b, r0:r0 + cg, o_q:o_q + wq] * (dk ** -0.5)
            k = proj_ref[b, r0:r0 + cg, o_k:o_k + wq]
            v = proj_ref[b, r0:r0 + cg, o_v:o_v + wb].astype(BF16)
            cum_last = cum[cg - 1:cg]
            it = dict(b=b, c=c, v=v, q_in=(q * jnp.exp(cum)).astype(BF16),
                      k_out=(k * jnp.exp(cum_last - cum)).astype(BF16), chunk_decay=jnp.exp(cum_last), qb=[], kb=[])
            for ib in range(cg // GLA_SUB):
                b0 = ib * GLA_SUB
                base = cum[b0 - 1:b0] if ib > 0 else jnp.zeros((1, wq), F32)
                it["qb"].append((q[b0:b0 + GLA_SUB] * jnp.exp(cum[b0:b0 + GLA_SUB] - base)).astype(BF16))
                it["kb"].append(jnp.where(rowc < b0 + GLA_SUB, k * jnp.exp(base - cum), 0.0).astype(BF16))
            items.append(it)
    for it in items:
        it["scores"] = [
            jnp.where(causal, jnp.concatenate(
                [_dot_nt(qb[:, hd * dk:(hd + 1) * dk], kb[:, hd * dk:(hd + 1) * dk])
                 for qb, kb in zip(it["qb"], it["kb"])], axis=0), 0.0).astype(BF16)
            for hd in range(heads)]
    for it in items:
        it["intra"] = [_dot(it["scores"][hd], it["v"][:, hd * dv:(hd + 1) * dv]) for hd in range(heads)]
    for c in range(ts // cg):
        r0 = c * cg
        for it in [t for t in items if t["c"] == c]:
            b = it["b"]
            outs = []
            for hd in range(heads):
                ks = slice(hd * dk, (hd + 1) * dk)
                vs = slice(hd * dv, (hd + 1) * dv)
                st = st_ref[b * heads + hd]
                o = it["intra"][hd] + _dot_nt(it["q_in"][:, ks], st.astype(BF16))
                st_ref[b * heads + hd] = st * it["chunk_decay"][:, ks] + _dot_tn(it["v"][:, vs], it["k_out"][:, ks])
                outs.append(o * lax.rsqrt(jnp.mean(o * o, axis=-1, keepdims=True) + HEAD_NORM_EPS))
            o_all = jnp.concatenate(outs, axis=-1) * gnorm_ref[...]
            g = proj_ref[b, r0:r0 + cg, o_g:o_g + wb]
            y_ref[b, r0:r0 + cg, wa:wa + wb] = (o_all * _silu(g)).astype(y_ref.dtype)


def _even_mixer(h, mnorm, win, convw, convb, wgate, bgate, lrua, wgk, bgk, gnorm, *, nb, ts, wa, dk, dv, rank):
    bsz, seq, d = h.shape
    cols = win.shape[1]
    heads = GLA_HEADS
    wb = heads * dv
    consts = (mnorm, win, convw, convb, wgate, bgate, lrua, wgk, bgk, gnorm)
    kern = functools.partial(_even_mixer_kernel, nb=nb, ts=ts, wa=wa, dk=dk, dv=dv, rank=rank)
    return pl.pallas_call(
        kern,
        grid=(bsz // nb, seq // ts),
        in_specs=[pl.BlockSpec((nb, ts, d), lambda b, s: (b, s, 0))] + [_resident(t) for t in consts],
        out_specs=pl.BlockSpec((nb, ts, wa + wb), lambda b, s: (b, s, 0)),
        out_shape=jax.ShapeDtypeStruct((bsz, seq, wa + wb), BF16),
        scratch_shapes=[pltpu.VMEM((nb, ts, cols), F32), pltpu.VMEM((nb, 8, wa), F32), pltpu.VMEM((nb, 1, wa), F32),
                        pltpu.VMEM((nb * heads, dv, dk), F32)],
        compiler_params=pltpu.CompilerParams(dimension_semantics=("arbitrary", "arbitrary"),
                                             vmem_limit_bytes=VMEM_LIMIT),
        name="even_mixer",
    )(h, *consts)


def _odd_mixer_kernel(h_ref, pos_ref, mnorm_ref, win_ref, freq_ref, rnorm_ref, mu_ref, w0_ref, ww2_ref, a0_ref, aw2_ref,
                      gw2_ref, kk_ref, ka_ref, rk_ref, wnorm_ref, seg_ref,
                      y_ref, proj_ref, rstate_ref, dprev_ref, wstate_ref, *, nb, ts, wc, wd, rw, ra, rg):
    f32 = F32

    @pl.when(pl.program_id(1) == 0)
    def _():
        rstate_ref[...] = jnp.zeros_like(rstate_ref)
        dprev_ref[...] = jnp.zeros_like(dprev_ref)
        wstate_ref[...] = jnp.zeros_like(wstate_ref)

    for b in range(nb):
        _project(h_ref.at[b], mnorm_ref, win_ref, proj_ref.at[b])

    o_d = 4 * wc
    dcols = 3 * wd + rw + ra + rg
    hdim = RWKV_HEAD_DIM
    cm = MIX_CHUNK
    n = 2 * cm
    groups = wd // LANES
    seg = seg_ref[...]
    lane = lax.broadcasted_iota(jnp.int32, (cm, LANES), 1)
    first_head = lane < hdim

    def stack_heads(t):
        return jnp.concatenate([jnp.where(first_head, t, 0.0), jnp.where(first_head, 0.0, t)], axis=0)

    rin = lax.broadcasted_iota(jnp.int32, (n, n), 0)
    cin = lax.broadcasted_iota(jnp.int32, (n, n), 1)
    strict = (rin & (cm - 1)) > (cin & (cm - 1))
    incl = (rin & (cm - 1)) >= (cin & (cm - 1))
    eye = jnp.where(rin == cin, 1.0, 0.0).astype(f32)
    tri_incl = jnp.where(_tri(cm), 1.0, 0.0).astype(BF16)

    work = []
    tails = []
    for b in range(nb):
        dpart = proj_ref[b, :, o_d:o_d + dcols]
        dshift = jnp.concatenate([dprev_ref[b], dpart], axis=0)[7:7 + ts]
        dprev_ref[b] = dpart[ts - 8:ts]
        dpart = dpart + mu_ref[...] * (dshift - dpart)
        r = dpart[:, 0:wd]
        k = dpart[:, wd:2 * wd]
        v = dpart[:, 2 * wd:3 * wd]
        w_lr = dpart[:, 3 * wd:3 * wd + rw]
        a_lr = dpart[:, 3 * wd + rw:3 * wd + rw + ra]
        g_lr = dpart[:, 3 * wd + rw + ra:dcols]
        w = -_softplus(-(w0_ref[...] + _dot(jnp.tanh(w_lr).astype(BF16), ww2_ref[...]))) - 0.5
        log_decay = -jnp.exp(w)
        a = jax.nn.sigmoid(a0_ref[...] + _dot(a_lr.astype(BF16), aw2_ref[...]))
        g = _dot(jax.nn.sigmoid(g_lr).astype(BF16), gw2_ref[...])
        kk = k * kk_ref[...]
        kk = kk * lax.rsqrt(_dot_split_lhs(kk * kk, seg, 2) + 1e-12)
        k = k * (1.0 + (a - 1.0) * ka_ref[...])
        tails.append((_dot_split_lhs(r * k * rk_ref[...], seg, 2) * v, g))
        kb = kk * a
        for ch in range(ts // cm):
            rs = slice(ch * cm, (ch + 1) * cm)
            ld = log_decay[rs]
            cum = _dot_split_rhs(tri_incl, ld, 3)
            cum_last = cum[cm - 1:cm]
            p_incl = jnp.exp(cum)
            p_inv = jnp.exp(-cum)
            a_t = -kk[rs] * jnp.exp(cum - ld)
            r_t = r[rs] * p_incl
            b_t = kb[rs] * p_inv
            k_t = k[rs] * p_inv
            p_out = jnp.exp(cum_last - cum)
            b_o = kb[rs] * p_out
            k_o = k[rs] * p_out
            p_last = jnp.exp(cum_last)
            for pr in range(groups):
                ls = slice(pr * LANES, (pr + 1) * LANES)
                l4 = jnp.concatenate([stack_heads(a_t[:, ls]), stack_heads(r_t[:, ls])], axis=0).astype(BF16)
                r4 = jnp.concatenate([stack_heads(b_t[:, ls]), stack_heads(k_t[:, ls])], axis=0).astype(BF16)
                o4 = jnp.concatenate([stack_heads(b_o[:, ls]), stack_heads(k_o[:, ls])], axis=0).astype(BF16)
                vbd = stack_heads(v[rs, ls]).astype(BF16)
                work.append(dict(b=b, ch=ch, pr=pr, l4=l4, r4=r4, o4=o4, vbd=vbd, p_last=p_last[:, ls]))
    for wk in work:
        big = _dot_nt(wk["l4"], wk["r4"])
        a_ab = jnp.where(strict, big[:n, :n], 0.0)
        wk["a_ak"] = jnp.where(strict, big[:n, n:], 0.0).astype(BF16)
        wk["a_r"] = jnp.concatenate([jnp.where(incl, big[n:, :n], 0.0), jnp.where(incl, big[n:, n:], 0.0)],
                                    axis=1).astype(BF16)
        wk["x"] = a_ab.astype(BF16)
        wk["tinv"] = eye + a_ab
    span = 2
    while span < cm:
        for wk in work:
            wk["x"] = _dot(wk["x"], wk["x"]).astype(BF16)
        for wk in work:
            wk["tinv"] = wk["tinv"] + _dot(wk["tinv"].astype(BF16), wk["x"])
        span *= 2

    c = ts
    dh = wc // RET_HEADS
    half = dh // 2
    rowi = lax.broadcasted_iota(jnp.int32, (c, c), 0)
    coli = lax.broadcasted_iota(jnp.int32, (c, c), 1)
    rel = (rowi - coli).astype(f32)
    rowd = lax.broadcasted_iota(jnp.int32, (c, dh), 0).astype(f32)
    ret = []
    for b in range(nb):
        ang = pos_ref[b] * freq_ref[...]
        cos, sin = jnp.cos(ang), jnp.sin(ang)
        cos2 = jnp.concatenate([cos, cos], axis=-1)
        sin2 = jnp.concatenate([-sin, sin], axis=-1)
        for hd in range(RET_HEADS):
            log_gamma = math.log1p(-2.0 ** (-5.0 - hd))
            q = proj_ref[b, :, hd * dh:(hd + 1) * dh]
            k = proj_ref[b, :, wc + hd * dh:wc + (hd + 1) * dh]
            v = proj_ref[b, :, 2 * wc + hd * dh:2 * wc + (hd + 1) * dh].astype(BF16)
            q = (q * cos2 + pltpu.roll(q, half, 1) * sin2) * (dh ** -0.5)
            k = k * cos2 + pltpu.roll(k, half, 1) * sin2
            dmask = jnp.where(rel >= 0.0, jnp.exp(jnp.maximum(rel, 0.0) * log_gamma), 0.0)
            ret.append(dict(b=b, hd=hd, v=v, q=q.astype(BF16), k=k.astype(BF16), dmask=dmask,
                            q_in=(q * jnp.exp((rowd + 1.0) * log_gamma)).astype(BF16),
                            k_out=(k * jnp.exp((c - 1.0 - rowd) * log_gamma)).astype(BF16),
                            decay=math.exp(c * log_gamma)))
    for rt in ret:
        rt["scores"] = (_dot_nt(rt["q"], rt["k"]) * rt["dmask"]).astype(BF16)
    for rt in ret:
        st = rstate_ref[rt["b"] * RET_HEADS + rt["hd"]]
        o = _dot(rt["scores"], rt["v"]) + _dot(rt["q_in"], st.astype(BF16))
        rstate_ref[rt["b"] * RET_HEADS + rt["hd"]] = st * rt["decay"] + _dot_tn(rt["k_out"], rt["v"])
        o = o - jnp.mean(o, axis=-1, keepdims=True)
        rt["o"] = o * lax.rsqrt(jnp.mean(o * o, axis=-1, keepdims=True) + HEAD_NORM_EPS)
    for b in range(nb):
        g = proj_ref[b, :, 3 * wc:4 * wc]
        o_all = jnp.concatenate([rt["o"] for rt in ret if rt["b"] == b], axis=-1)
        y_ref[b, :, 0:wc] = (o_all * rnorm_ref[...] * _silu(g)).astype(y_ref.dtype)

    for ch in range(ts // cm):
        rs = slice(ch * cm, (ch + 1) * cm)
        wks = [wk for wk in work if wk["ch"] == ch]
        sidx = [wk["b"] * groups + wk["pr"] for wk in wks]
        sts = [wstate_ref[i] for i in sidx]
        fss = [_dot_nt(wk["l4"], st.astype(BF16)) for wk, st in zip(wks, sts)]
        rhs = [(fs[:n] + _dot(wk["a_ak"], wk["vbd"])).astype(BF16) for wk, fs in zip(wks, fss)]
        us = [_dot(wk["tinv"].astype(BF16), rh) for wk, rh in zip(wks, rhs)]
        uvs = [jnp.concatenate([u.astype(BF16), wk["vbd"]], axis=0) for wk, u in zip(wks, us)]
        for i, st, wk, uv in zip(sidx, sts, wks, uvs):
            wstate_ref[i] = st * wk["p_last"] + _dot_tn(uv, wk["o4"])
        ybds = [fs[n:] + _dot(wk["a_r"], uv) for wk, fs, uv in zip(wks, fss, uvs)]
        for b in range(nb):
            y = jnp.concatenate([yb[:cm] + yb[cm:] for wk, yb in zip(wks, ybds) if wk["b"] == b], axis=-1)
            inv_h = 1.0 / hdim
            y = y - _dot_split_lhs(y, seg, 2) * inv_h
            y = y * lax.rsqrt(_dot_split_lhs(y * y, seg, 2) * inv_h + RWKV_LN_EPS)
            bonus, g = tails[b]
            y = (y * wnorm_ref[...] + bonus[rs]) * g[rs]
            y_ref[b, ch * cm:(ch + 1) * cm, wc:wc + wd] = y.astype(y_ref.dtype)


def _odd_mixer(h, pos, mnorm, win, freq, rnorm, mu, w0, ww2, a0, aw2, gw2, kk, ka, rk, wnorm, seg, *, nb, ts, wc, wd, rw,
               ra, rg):
    bsz, seq, d = h.shape
    cols = win.shape[1]
    dh = wc // RET_HEADS
    consts = (mnorm, win, freq, rnorm, mu, w0, ww2, a0, aw2, gw2, kk, ka, rk, wnorm, seg)
    kern = functools.partial(_odd_mixer_kernel, nb=nb, ts=ts, wc=wc, wd=wd, rw=rw, ra=ra, rg=rg)
    dcols = 3 * wd + rw + ra + rg
    return pl.pallas_call(
        kern,
        grid=(bsz // nb, seq // ts),
        in_specs=[pl.BlockSpec((nb, ts, d), lambda b, s: (b, s, 0)),
                  pl.BlockSpec((nb, ts, 1), lambda b, s: (b, s, 0))] + [_resident(t) for t in consts],
        out_specs=pl.BlockSpec((nb, ts, wc + wd), lambda b, s: (b, s, 0)),
        out_shape=jax.ShapeDtypeStruct((bsz, seq, wc + wd), BF16),
        scratch_shapes=[pltpu.VMEM((nb, ts, cols), F32), pltpu.VMEM((nb * RET_HEADS, dh, dh), F32),
                        pltpu.VMEM((nb, 8, dcols), F32), pltpu.VMEM((nb * (wd // LANES), LANES, LANES), F32)],
        compiler_params=pltpu.CompilerParams(dimension_semantics=("arbitrary", "arbitrary"),
                                             vmem_limit_bytes=VMEM_LIMIT),
        name="odd_mixer",
    )(h, pos, *consts)


def _post_kernel(h_ref, y_ref, p_ref, wout_ref, fnorm_ref, wg_ref, wu_ref, wd_ref, pnorm_ref, pwg_ref, pbg_ref,
                 pwp_ref, onorm_ref, o_ref, *, final):
    h = h_ref[...] + _dot(y_ref[...], wout_ref[...])
    xn = _rms(h, fnorm_ref[...]).astype(BF16)
    act = (_silu(_dot(xn, wg_ref[...])) * _dot(xn, wu_ref[...])).astype(BF16)
    h = h + _dot(act, wd_ref[...])
    xg = _rms(h, pnorm_ref[...]).astype(BF16)
    gate = jax.nn.sigmoid(_dot(xg, pwg_ref[...]) + pbg_ref[...])
    h = h + gate * _dot(p_ref[...].astype(BF16), pwp_ref[...])
    if final:
        h = _rms(h, onorm_ref[...])
    o_ref[...] = h


def _post(h, y, p, wout, fnorm, wg, wu, wd, pnorm, pwg, pbg, pwp, onorm, *, tm, final):
    m, d = h.shape
    rows = lambda width: pl.BlockSpec((tm, width), lambda i: (i, 0))
    resident = lambda t: pl.BlockSpec(t.shape, lambda i: (0, 0), pipeline_mode=pl.Buffered(1))
    weights = (wout, fnorm, wg, wu, wd, pnorm, pwg, pbg, pwp, onorm)
    return pl.pallas_call(
        functools.partial(_post_kernel, final=final),
        grid=(m // tm,),
        in_specs=[rows(d), rows(y.shape[1]), rows(p.shape[1])] + [resident(t) for t in weights],
        out_specs=rows(d),
        out_shape=jax.ShapeDtypeStruct((m, d), F32),
        compiler_params=pltpu.CompilerParams(dimension_semantics=("arbitrary",), vmem_limit_bytes=VMEM_LIMIT),
        name="post_mixer",
    )(h, y, p, *weights)


def _block_diag(blocks):
    g, bi, bj = blocks.shape
    eye = jnp.eye(g, dtype=blocks.dtype)
    return (eye[:, None, :, None] * blocks[:, :, None, :]).reshape(g * bi, g * bj)


def kernel(x, p, positions, ev_w_in, ev_conv_w, ev_conv_b, ev_lru_wr, ev_lru_br, ev_lru_wi, ev_lru_bi, ev_lru_a, ev_gla_wgk, ev_gla_bgk, ev_gla_norm, ev_w_out, od_w_in, od_ret_norm, od_rwkv_mu, od_rwkv_w0, od_rwkv_ww2, od_rwkv_a0, od_rwkv_aw2, od_rwkv_gw2, od_rwkv_kk, od_rwkv_ka, od_rwkv_rk, od_rwkv_norm, od_w_out, mix_norm, ffn_norm, ffn_w_gate, ffn_w_up, ffn_w_down, ple_norm, ple_w_gate, ple_b_gate, ple_w_proj, final_norm):
    bsz, seq, d = x.shape
    depth = p.shape[0]
    m = bsz * seq
    row = lambda t: t.reshape(1, -1).astype(F32)

    wa = ev_conv_w.shape[-1]
    rank = ev_gla_wgk.shape[1]
    wq = ev_gla_wgk.shape[2]
    wb = ev_gla_norm.shape[-1]
    dk, dv = wq // GLA_HEADS, wb // GLA_HEADS
    wc = od_ret_norm.shape[-1]
    wd = od_rwkv_w0.shape[-1]
    rw, ra, rg = od_rwkv_ww2.shape[1], od_rwkv_aw2.shape[1], od_rwkv_gw2.shape[1]
    nb = MIX_BATCH if bsz % MIX_BATCH == 0 else 1

    h = x.reshape(m, d)
    freq = (ROPE_BASE ** (-jnp.arange(wc // RET_HEADS // 2, dtype=F32) / (wc // RET_HEADS // 2))).reshape(1, -1)
    pos = positions.astype(F32).reshape(bsz, seq, 1)
    seg = _block_diag(jnp.ones((wd // RWKV_HEAD_DIM, RWKV_HEAD_DIM, RWKV_HEAD_DIM), BF16))
    assert 2 * RWKV_HEAD_DIM == LANES and wd % LANES == 0

    for i in range(depth):
        j = i // 2
        if i % 2 == 0:
            w_in = ev_w_in[j]
            cols = w_in.shape[1]
            cols_pad = -(-cols // LANES) * LANES
            w_in = jnp.pad(w_in, ((0, 0), (0, cols_pad - cols))).astype(BF16)
            wgate = jnp.concatenate([_block_diag(ev_lru_wr[j]), _block_diag(ev_lru_wi[j])], axis=1).astype(BF16)
            bgate = jnp.concatenate([ev_lru_br[j], ev_lru_bi[j]]).reshape(1, -1)
            y = _even_mixer(h.reshape(bsz, seq, d), row(mix_norm[i]), w_in, ev_conv_w[j], row(ev_conv_b[j]), wgate, bgate,
                            row(ev_lru_a[j]), ev_gla_wgk[j], row(ev_gla_bgk[j]), row(ev_gla_norm[j]),
                            nb=nb, ts=2 * MIX_CHUNK, wa=wa, dk=dk, dv=dv, rank=rank)
            w_out = ev_w_out[j]
        else:
            y = _odd_mixer(h.reshape(bsz, seq, d), pos, row(mix_norm[i]), od_w_in[j].astype(BF16), freq,
                           row(od_ret_norm[j]), row(od_rwkv_mu[j]),
                           row(od_rwkv_w0[j]), od_rwkv_ww2[j].astype(BF16), row(od_rwkv_a0[j]),
                           od_rwkv_aw2[j].astype(BF16), od_rwkv_gw2[j].astype(BF16), row(od_rwkv_kk[j]),
                           row(od_rwkv_ka[j]), row(od_rwkv_rk[j]), row(od_rwkv_norm[j]), seg,
                           nb=nb, ts=RET_CHUNK, wc=wc, wd=wd, rw=rw, ra=ra, rg=rg)
            w_out = od_w_out[j]
        h = _post(h, y.reshape(m, -1), p[i].reshape(m, -1), w_out.astype(BF16), row(ffn_norm[i]),
                  ffn_w_gate[i].astype(BF16), ffn_w_up[i].astype(BF16), ffn_w_down[i].astype(BF16),
                  row(ple_norm[i]), ple_w_gate[i].astype(BF16), row(ple_b_gate[i]), ple_w_proj[i].astype(BF16),
                  row(final_norm), tm=256, final=(i == depth - 1))
    return h.reshape(bsz, seq, d)
```

```python
import functools
import math

import jax
import jax.numpy as jnp
from jax import lax
from jax.experimental import pallas as pl
from jax.experimental.pallas import tpu as pltpu

F32 = jnp.float32
BF16 = jnp.bfloat16
HIGHEST = lax.Precision.HIGHEST

NORM_EPS = 1e-6
LANES = 128
SUBLANES = 8
LRU_BLOCKS = 8
CONV_W = 4
LRU_C = 8.0
GLA_HEADS = 4
GLA_GATE_NORM = 16.0
GLA_SUB = 16
RET_HEADS = 4
RET_CHUNK = 128
ROPE_BASE = 10000.0
RWKV_HEAD_DIM = 64
RWKV_LN_EPS = 64e-5
HEAD_NORM_EPS = 1e-5

MIX_CHUNK = 64
MIX_BATCH = 2
VMEM_LIMIT = 56 * 1024 * 1024


def _log_sigmoid(x):
    return jnp.minimum(x, 0.0) - jnp.log1p(jnp.exp(-jnp.abs(x)))


def _softplus(x):
    return jnp.maximum(x, 0.0) + jnp.log1p(jnp.exp(-jnp.abs(x)))


def _silu(x):
    return x * jax.nn.sigmoid(x)


def _gelu_tanh(x):
    return 0.5 * x * (1.0 + jnp.tanh(math.sqrt(2.0 / math.pi) * (x + 0.044715 * (x * x * x))))


def _rms(x, g):
    return x * lax.rsqrt(jnp.mean(x * x, axis=-1, keepdims=True) + NORM_EPS) * g


def _dot(a, b, precision=None):
    return jnp.dot(a, b, preferred_element_type=F32, precision=precision)


def _dot_nt(a, b, precision=None):
    return lax.dot_general(a, b, (((1,), (1,)), ((), ())), preferred_element_type=F32, precision=precision)


def _dot_tn(a, b, precision=None):
    return lax.dot_general(a, b, (((0,), (0,)), ((), ())), preferred_element_type=F32, precision=precision)


def _dot_split_lhs(x, m, passes):
    acc = None
    for _ in range(passes):
        piece = x.astype(BF16)
        acc = _dot(piece, m) if acc is None else acc + _dot(piece, m)
        x = x - piece.astype(F32)
    return acc


def _dot_split_rhs(m, x, passes):
    acc = None
    for _ in range(passes):
        piece = x.astype(BF16)
        acc = _dot(m, piece) if acc is None else acc + _dot(m, piece)
        x = x - piece.astype(F32)
    return acc


def _tri(n, strict=False):
    row = lax.broadcasted_iota(jnp.int32, (n, n), 0)
    col = lax.broadcasted_iota(jnp.int32, (n, n), 1)
    return (row > col) if strict else (row >= col)


PROJ_GROUP = 512


def _project_steps(h_ref, mnorm_ref, win_ref, proj_ref):
    xn = _rms(h_ref[...], mnorm_ref[...]).astype(BF16)
    cols = proj_ref.shape[1]
    for c0 in range(0, cols, PROJ_GROUP):
        c1 = min(c0 + PROJ_GROUP, cols)
        proj_ref[:, c0:c1] = _dot(xn, win_ref[:, c0:c1])
        yield


def _interleave(*gens):
    gens = list(gens)
    while gens:
        for g in list(gens):
            try:
                next(g)
            except StopIteration:
                gens.remove(g)


def _resident(t):
    return pl.BlockSpec(t.shape, lambda b, s: (0,) * t.ndim, pipeline_mode=pl.Buffered(1))


def _rg_lru_steps(proj_ref, y_ref, xtail_ref, hcar_ref, convw_ref, convb_ref, wgate_ref, bgate_ref, lrua_ref, *, ts, wa):
    ax = proj_ref[:, 0:wa]
    xcat = jnp.concatenate([xtail_ref[...], ax], axis=0)
    xtail_ref[...] = ax[ts - 8:ts]
    cw = convw_ref[...]
    xc = convb_ref[...] + cw[CONV_W - 1:CONV_W] * ax
    for s in range(1, CONV_W):
        xc = xc + cw[CONV_W - 1 - s:CONV_W - s] * xcat[8 - s:8 - s + ts]
    yield
    gates = _dot(xc.astype(BF16), wgate_ref[...]) + bgate_ref[...]
    r = jax.nn.sigmoid(gates[:, :wa])
    i = jax.nn.sigmoid(gates[:, wa:])
    yield
    log_a = LRU_C * r * _log_sigmoid(lrua_ref[...])
    a = jnp.exp(log_a)
    t = jnp.tanh(log_a)
    u = jnp.sqrt(-2.0 * t / (1.0 - t)) * (i * xc)
    yield
    groups = ts // SUBLANES
    u = u.reshape(groups, SUBLANES, wa)
    a = a.reshape(groups, SUBLANES, wa)
    row = lax.broadcasted_iota(jnp.int32, (groups, SUBLANES, wa), 1)
    d = 1
    while d < SUBLANES:
        keep = row >= d
        u = jnp.where(keep, a * pltpu.roll(u, d, 1) + u, u)
        a = jnp.where(keep, a * pltpu.roll(a, d, 1), a)
        d *= 2
        yield
    carry = hcar_ref[...]
    pieces = []
    for j in range(groups):
        hj = u[j] + a[j] * carry
        carry = hj[SUBLANES - 1:SUBLANES]
        pieces.append(hj)
    hcar_ref[...] = carry
    h = jnp.concatenate(pieces, axis=0)
    yield
    y_ref[:, 0:wa] = (h * _gelu_tanh(proj_ref[:, wa:2 * wa])).astype(y_ref.dtype)
    yield


def _even_mixer_kernel(h_ref, mnorm_ref, win_ref, convw_ref, convb_ref, wgate_ref, bgate_ref, lrua_ref,
                       wgk_ref, bgk_ref, gnorm_ref, y_ref, proj_ref, xtail_ref, hcar_ref, st_ref,
                       *, nb, ts, wa, dk, dv, rank):
    heads = GLA_HEADS
    wq = heads * dk
    wb = heads * dv
    o_q = 2 * wa
    o_k, o_v = o_q + wq, o_q + 2 * wq
    o_g, o_gk = o_v + wb, o_v + 2 * wb
    step = pl.program_id(1)

    @pl.when(step == 0)
    def _():
        xtail_ref[...] = jnp.zeros_like(xtail_ref)
        hcar_ref[...] = jnp.zeros_like(hcar_ref)
        st_ref[...] = jnp.zeros_like(st_ref)

    project = [_project_steps(h_ref.at[b], mnorm_ref, win_ref, proj_ref.at[b]) for b in range(nb)]
    rg_lru = [_rg_lru_steps(proj_ref.at[b], y_ref.at[b], xtail_ref.at[b], hcar_ref.at[b], convw_ref, convb_ref,
                            wgate_ref, bgate_ref, lrua_ref, ts=ts, wa=wa) for b in range(nb)]
    _interleave(project[0])
    for b in range(nb):
        _interleave(rg_lru[b], *project[b + 1:b + 2])

    cg = MIX_CHUNK
    tri_incl = jnp.where(_tri(cg), 1.0, 0.0).astype(BF16)
    causal = _tri(cg)
    rowc = lax.broadcasted_iota(jnp.int32, (cg, wq), 0)
    items = []
    for b in range(nb):
        z = _dot(proj_ref[b, :, o_gk:o_gk + rank], wgk_ref[...], precision=HIGHEST) + bgk_ref[...]
        log_f = _log_sigmoid(z) * (1.0 / GLA_GATE_NORM)
        for c in range(ts // cg):
            r0 = c * cg
            cum = _dot_split_rhs(tri_incl, log_f[r0:r0 + cg], 3)
            q = proj_ref[b, r0:r0 + cg, o_q:o_q + wq] * (dk ** -0.5)
            k = proj_ref[b, r0:r0 + cg, o_k:o_k + wq]
            v = proj_ref[b, r0:r0 + cg, o_v:o_v + wb].astype(BF16)
            cum_last = cum[cg - 1:cg]
            it = dict(b=b, c=c, v=v, q_in=(q * jnp.exp(cum)).astype(BF16),
                      k_out=(k * jnp.exp(cum_last - cum)).astype(BF16), chunk_decay=jnp.exp(cum_last), qb=[], kb=[])
            for ib in range(cg // GLA_SUB):
                b0 = ib * GLA_SUB
                base = cum[b0 - 1:b0] if ib > 0 else jnp.zeros((1, wq), F32)
                it["qb"].append((q[b0:b0 + GLA_SUB] * jnp.exp(cum[b0:b0 + GLA_SUB] - base)).astype(BF16))
                it["kb"].append(jnp.where(rowc < b0 + GLA_SUB, k * jnp.exp(base - cum), 0.0).astype(BF16))
            items.append(it)
    for it in items:
        it["scores"] = [
            jnp.where(causal, jnp.concatenate(
                [_dot_nt(qb[:, hd * dk:(hd + 1) * dk], kb[:, hd * dk:(hd + 1) * dk])
                 for qb, kb in zip(it["qb"], it["kb"])], axis=0), 0.0).astype(BF16)
            for hd in range(heads)]
    for it in items:
        it["intra"] = [_dot(it["scores"][hd], it["v"][:, hd * dv:(hd + 1) * dv]) for hd in range(heads)]
    for c in range(ts // cg):
        r0 = c * cg
        for it in [t for t in items if t["c"] == c]:
            b = it["b"]
            outs = []
            for hd in range(heads):
                ks = slice(hd * dk, (hd + 1) * dk)
                vs = slice(hd * dv, (hd + 1) * dv)
                st = st_ref[b * heads + hd]
                o = it["intra"][hd] + _dot_nt(it["q_in"][:, ks], st.astype(BF16))
                st_ref[b * heads + hd] = st * it["chunk_decay"][:, ks] + _dot_tn(it["v"][:, vs], it["k_out"][:, ks])
                outs.append(o * lax.rsqrt(jnp.mean(o * o, axis=-1, keepdims=True) + HEAD_NORM_EPS))
            o_all = jnp.concatenate(outs, axis=-1) * gnorm_ref[...]
            g = proj_ref[b, r0:r0 + cg, o_g:o_g + wb]
            y_ref[b, r0:r0 + cg, wa:wa + wb] = (o_all * _silu(g)).astype(y_ref.dtype)


def _even_mixer(h, mnorm, win, convw, convb, wgate, bgate, lrua, wgk, bgk, gnorm, *, nb, ts, wa, dk, dv, rank):
    bsz, seq, d = h.shape
    cols = win.shape[1]
    heads = GLA_HEADS
    wb = heads * dv
    consts = (mnorm, win, convw, convb, wgate, bgate, lrua, wgk, bgk, gnorm)
    kern = functools.partial(_even_mixer_kernel, nb=nb, ts=ts, wa=wa, dk=dk, dv=dv, rank=rank)
    return pl.pallas_call(
        kern,
        grid=(bsz // nb, seq // ts),
        in_specs=[pl.BlockSpec((nb, ts, d), lambda b, s: (b, s, 0))] + [_resident(t) for t in consts],
        out_specs=pl.BlockSpec((nb, ts, wa + wb), lambda b, s: (b, s, 0)),
        out_shape=jax.ShapeDtypeStruct((bsz, seq, wa + wb), BF16),
        scratch_shapes=[pltpu.VMEM((nb, ts, cols), F32), pltpu.VMEM((nb, 8, wa), F32),
                        pltpu.VMEM((nb, 1, wa), F32), pltpu.VMEM((nb * heads, dv, dk), F32)],
        compiler_params=pltpu.CompilerParams(dimension_semantics=("arbitrary", "arbitrary"),
                                             vmem_limit_bytes=VMEM_LIMIT),
        name="even_mixer",
    )(h, *consts)


def _odd_mixer_kernel(h_ref, pos_ref, mnorm_ref, win_ref, freq_ref, rnorm_ref, mu_ref, w0_ref, ww2_ref, a0_ref,
                      aw2_ref, gw2_ref, kk_ref, ka_ref, rk_ref, wnorm_ref, seg_ref,
                      y_ref, proj_ref, rstate_ref, dprev_ref, wstate_ref, *, nb, ts, wc, wd, rw, ra, rg):
    f32 = F32

    @pl.when(pl.program_id(1) == 0)
    def _():
        rstate_ref[...] = jnp.zeros_like(rstate_ref)
        dprev_ref[...] = jnp.zeros_like(dprev_ref)
        wstate_ref[...] = jnp.zeros_like(wstate_ref)

    for b in range(nb):
        _interleave(_project_steps(h_ref.at[b], mnorm_ref, win_ref, proj_ref.at[b]))

    o_d = 4 * wc
    dcols = 3 * wd + rw + ra + rg
    hdim = RWKV_HEAD_DIM
    cm = MIX_CHUNK
    n = 2 * cm
    groups = wd // LANES
    seg = seg_ref[...]
    lane = lax.broadcasted_iota(jnp.int32, (cm, LANES), 1)
    first_head = lane < hdim

    def stack_heads(t):
        return jnp.concatenate([jnp.where(first_head, t, 0.0), jnp.where(first_head, 0.0, t)], axis=0)

    rin = lax.broadcasted_iota(jnp.int32, (n, n), 0)
    cin = lax.broadcasted_iota(jnp.int32, (n, n), 1)
    strict = (rin & (cm - 1)) > (cin & (cm - 1))
    incl = (rin & (cm - 1)) >= (cin & (cm - 1))
    eye = jnp.where(rin == cin, 1.0, 0.0).astype(f32)
    tri_incl = jnp.where(_tri(cm), 1.0, 0.0).astype(BF16)

    work = []
    tails = {}

    def prepare_steps(b):
        dpart = proj_ref[b, :, o_d:o_d + dcols]
        dshift = jnp.concatenate([dprev_ref[b], dpart], axis=0)[7:7 + ts]
        dprev_ref[b] = dpart[ts - 8:ts]
        dpart = dpart + mu_ref[...] * (dshift - dpart)
        yield
        r = dpart[:, 0:wd]
        k = dpart[:, wd:2 * wd]
        v = dpart[:, 2 * wd:3 * wd]
        w_lr = dpart[:, 3 * wd:3 * wd + rw]
        a_lr = dpart[:, 3 * wd + rw:3 * wd + rw + ra]
        g_lr = dpart[:, 3 * wd + rw + ra:dcols]
        w = -_softplus(-(w0_ref[...] + _dot(jnp.tanh(w_lr).astype(BF16), ww2_ref[...]))) - 0.5
        log_decay = -jnp.exp(w)
        a = jax.nn.sigmoid(a0_ref[...] + _dot(a_lr.astype(BF16), aw2_ref[...]))
        g = _dot(jax.nn.sigmoid(g_lr).astype(BF16), gw2_ref[...])
        yield
        kk = k * kk_ref[...]
        kk = kk * lax.rsqrt(_dot_split_lhs(kk * kk, seg, 2) + 1e-12)
        k = k * (1.0 + (a - 1.0) * ka_ref[...])
        tails[b] = (_dot_split_lhs(r * k * rk_ref[...], seg, 2) * v, g)
        kb = kk * a
        yield
        for ch in range(ts // cm):
            rs = slice(ch * cm, (ch + 1) * cm)
            ld = log_decay[rs]
            cum = _dot_split_rhs(tri_incl, ld, 3)
            cum_last = cum[cm - 1:cm]
            p_incl = jnp.exp(cum)
            p_inv = jnp.exp(-cum)
            a_t = -kk[rs] * jnp.exp(cum - ld)
            r_t = r[rs] * p_incl
            b_t = kb[rs] * p_inv
            k_t = k[rs] * p_inv
            p_out = jnp.exp(cum_last - cum)
            b_o = kb[rs] * p_out
            k_o = k[rs] * p_out
            p_last = jnp.exp(cum_last)
            yield
            for pr in range(groups):
                ls = slice(pr * LANES, (pr + 1) * LANES)
                l4 = jnp.concatenate([stack_heads(a_t[:, ls]), stack_heads(r_t[:, ls])], axis=0).astype(BF16)
                r4 = jnp.concatenate([stack_heads(b_t[:, ls]), stack_heads(k_t[:, ls])], axis=0).astype(BF16)
                o4 = jnp.concatenate([stack_heads(b_o[:, ls]), stack_heads(k_o[:, ls])], axis=0).astype(BF16)
                vbd = stack_heads(v[rs, ls]).astype(BF16)
                work.append(dict(b=b, ch=ch, pr=pr, l4=l4, r4=r4, o4=o4, vbd=vbd, p_last=p_last[:, ls]))
                yield

    for b in range(nb):
        _interleave(prepare_steps(b))
    for wk in work:
        big = _dot_nt(wk["l4"], wk["r4"])
        a_ab = jnp.where(strict, big[:n, :n], 0.0)
        wk["a_ak"] = jnp.where(strict, big[:n, n:], 0.0).astype(BF16)
        wk["a_r"] = jnp.concatenate([jnp.where(incl, big[n:, :n], 0.0), jnp.where(incl, big[n:, n:], 0.0)],
                                    axis=1).astype(BF16)
        wk["x"] = a_ab.astype(BF16)
        wk["tinv"] = eye + a_ab
    span = 2
    while span < cm:
        for wk in work:
            wk["x"] = _dot(wk["x"], wk["x"]).astype(BF16)
        for wk in work:
            wk["tinv"] = wk["tinv"] + _dot(wk["tinv"].astype(BF16), wk["x"])
        span *= 2

    c = ts
    dh = wc // RET_HEADS
    half = dh // 2
    rowi = lax.broadcasted_iota(jnp.int32, (c, c), 0)
    coli = lax.broadcasted_iota(jnp.int32, (c, c), 1)
    rel = (rowi - coli).astype(f32)
    rowd = lax.broadcasted_iota(jnp.int32, (c, dh), 0).astype(f32)
    ret = []
    for b in range(nb):
        ang = pos_ref[b] * freq_ref[...]
        cos, sin = jnp.cos(ang), jnp.sin(ang)
        cos2 = jnp.concatenate([cos, cos], axis=-1)
        sin2 = jnp.concatenate([-sin, sin], axis=-1)
        for hd in range(RET_HEADS):
            log_gamma = math.log1p(-2.0 ** (-5.0 - hd))
            q = proj_ref[b, :, hd * dh:(hd + 1) * dh]
            k = proj_ref[b, :, wc + hd * dh:wc + (hd + 1) * dh]
            v = proj_ref[b, :, 2 * wc + hd * dh:2 * wc + (hd + 1) * dh].astype(BF16)
            q = (q * cos2 + pltpu.roll(q, half, 1) * sin2) * (dh ** -0.5)
            k = k * cos2 + pltpu.roll(k, half, 1) * sin2
            dmask = jnp.where(rel >= 0.0, jnp.exp(jnp.maximum(rel, 0.0) * log_gamma), 0.0)
            ret.append(dict(b=b, hd=hd, v=v, q=q.astype(BF16), k=k.astype(BF16), dmask=dmask,
                            q_in=(q * jnp.exp((rowd + 1.0) * log_gamma)).astype(BF16),
                            k_out=(k * jnp.exp((c - 1.0 - rowd) * log_gamma)).astype(BF16),
                            decay=math.exp(c * log_gamma)))
    for rt in ret:
        rt["scores"] = (_dot_nt(rt["q"], rt["k"]) * rt["dmask"]).astype(BF16)
    for rt in ret:
        st = rstate_ref[rt["b"] * RET_HEADS + rt["hd"]]
        o = _dot(rt["scores"], rt["v"]) + _dot(rt["q_in"], st.astype(BF16))
        rstate_ref[rt["b"] * RET_HEADS + rt["hd"]] = st * rt["decay"] + _dot_tn(rt["k_out"], rt["v"])
        o = o - jnp.mean(o, axis=-1, keepdims=True)
        rt["o"] = o * lax.rsqrt(jnp.mean(o * o, axis=-1, keepdims=True) + HEAD_NORM_EPS)
    for b in range(nb):
        g = proj_ref[b, :, 3 * wc:4 * wc]
        o_all = jnp.concatenate([rt["o"] for rt in ret if rt["b"] == b], axis=-1)
        y_ref[b, :, 0:wc] = (o_all * rnorm_ref[...] * _silu(g)).astype(y_ref.dtype)

    for ch in range(ts // cm):
        rs = slice(ch * cm, (ch + 1) * cm)
        wks = [wk for wk in work if wk["ch"] == ch]
        sidx = [wk["b"] * groups + wk["pr"] for wk in wks]
        sts = [wstate_ref[i] for i in sidx]
        fss = [_dot_nt(wk["l4"], st.astype(BF16)) for wk, st in zip(wks, sts)]
        rhs = [(fs[:n] + _dot(wk["a_ak"], wk["vbd"])).astype(BF16) for wk, fs in zip(wks, fss)]
        us = [_dot(wk["tinv"].astype(BF16), rh) for wk, rh in zip(wks, rhs)]
        uvs = [jnp.concatenate([u.astype(BF16), wk["vbd"]], axis=0) for wk, u in zip(wks, us)]
        for i, st, wk, uv in zip(sidx, sts, wks, uvs):
            wstate_ref[i] = st * wk["p_last"] + _dot_tn(uv, wk["o4"])
        ybds = [fs[n:] + _dot(wk["a_r"], uv) for wk, fs, uv in zip(wks, fss, uvs)]
        for b in range(nb):
            y = jnp.concatenate([yb[:cm] + yb[cm:] for wk, yb in zip(wks, ybds) if wk["b"] == b], axis=-1)
            inv_h = 1.0 / hdim
            y = y - _dot_split_lhs(y, seg, 2) * inv_h
            y = y * lax.rsqrt(_dot_split_lhs(y * y, seg, 2) * inv_h + RWKV_LN_EPS)
            bonus, g = tails[b]
            y = (y * wnorm_ref[...] + bonus[rs]) * g[rs]
            y_ref[b, ch * cm:(ch + 1) * cm, wc:wc + wd] = y.astype(y_ref.dtype)


def _odd_mixer(h, pos, mnorm, win, freq, rnorm, mu, w0, ww2, a0, aw2, gw2, kk, ka, rk, wnorm, seg, *, nb, ts, wc, wd, rw,
               ra, rg):
    bsz, seq, d = h.shape
    cols = win.shape[1]
    dh = wc // RET_HEADS
    consts = (mnorm, win, freq, rnorm, mu, w0, ww2, a0, aw2, gw2, kk, ka, rk, wnorm, seg)
    kern = functools.partial(_odd_mixer_kernel, nb=nb, ts=ts, wc=wc, wd=wd, rw=rw, ra=ra, rg=rg)
    dcols = 3 * wd + rw + ra + rg
    return pl.pallas_call(
        kern,
        grid=(bsz // nb, seq // ts),
        in_specs=[pl.BlockSpec((nb, ts, d), lambda b, s: (b, s, 0)),
                  pl.BlockSpec((nb, ts, 1), lambda b, s: (b, s, 0))] + [_resident(t) for t in consts],
        out_specs=pl.BlockSpec((nb, ts, wc + wd), lambda b, s: (b, s, 0)),
        out_shape=jax.ShapeDtypeStruct((bsz, seq, wc + wd), BF16),
        scratch_shapes=[pltpu.VMEM((nb, ts, cols), F32), pltpu.VMEM((nb * RET_HEADS, dh, dh), F32),
                        pltpu.VMEM((nb, 8, dcols), F32), pltpu.VMEM((nb * (wd // LANES), LANES, LANES), F32)],
        compiler_params=pltpu.CompilerParams(dimension_semantics=("arbitrary", "arbitrary"),
                                             vmem_limit_bytes=VMEM_LIMIT),
        name="odd_mixer",
    )(h, pos, *consts)


def _post_kernel(h_ref, y_ref, p_ref, wout_ref, fnorm_ref, wg_ref, wu_ref, wd_ref, pnorm_ref, pwg_ref, pbg_ref,
                 pwp_ref, onorm_ref, o_ref, *, final):
    h = h_ref[...] + _dot(y_ref[...], wout_ref[...])
    xn = _rms(h, fnorm_ref[...]).astype(BF16)
    act = (_silu(_dot(xn, wg_ref[...])) * _dot(xn, wu_ref[...])).astype(BF16)
    h = h + _dot(act, wd_ref[...])
    xg = _rms(h, pnorm_ref[...]).astype(BF16)
    gate = jax.nn.sigmoid(_dot(xg, pwg_ref[...]) + pbg_ref[...])
    h = h + gate * _dot(p_ref[...].astype(BF16), pwp_ref[...])
    if final:
        h = _rms(h, onorm_ref[...])
    o_ref[...] = h


def _post(h, y, p, wout, fnorm, wg, wu, wd, pnorm, pwg, pbg, pwp, onorm, *, tm, final):
    m, d = h.shape
    rows = lambda width: pl.BlockSpec((tm, width), lambda i: (i, 0))
    resident = lambda t: pl.BlockSpec(t.shape, lambda i: (0, 0), pipeline_mode=pl.Buffered(1))
    weights = (wout, fnorm, wg, wu, wd, pnorm, pwg, pbg, pwp, onorm)
    return pl.pallas_call(
        functools.partial(_post_kernel, final=final),
        grid=(m // tm,),
        in_specs=[rows(d), rows(y.shape[1]), rows(p.shape[1])] + [resident(t) for t in weights],
        out_specs=rows(d),
        out_shape=jax.ShapeDtypeStruct((m, d), F32),
        compiler_params=pltpu.CompilerParams(dimension_semantics=("arbitrary",), vmem_limit_bytes=VMEM_LIMIT),
        name="post_mixer",
    )(h, y, p, *weights)


def _block_diag(blocks):
    g, bi, bj = blocks.shape
    eye = jnp.eye(g, dtype=blocks.dtype)
    return (eye[:, None, :, None] * blocks[:, :, None, :]).reshape(g * bi, g * bj)


def kernel(x, p, positions, ev_w_in, ev_conv_w, ev_conv_b, ev_lru_wr, ev_lru_br, ev_lru_wi, ev_lru_bi, ev_lru_a, ev_gla_wgk, ev_gla_bgk, ev_gla_norm, ev_w_out, od_w_in, od_ret_norm, od_rwkv_mu, od_rwkv_w0, od_rwkv_ww2, od_rwkv_a0, od_rwkv_aw2, od_rwkv_gw2, od_rwkv_kk, od_rwkv_ka, od_rwkv_rk, od_rwkv_norm, od_w_out, mix_norm, ffn_norm, ffn_w_gate, ffn_w_up, ffn_w_down, ple_norm, ple_w_gate, ple_b_gate, ple_w_proj, final_norm):
    bsz, seq, d = x.shape
    depth = p.shape[0]
    m = bsz * seq
    row = lambda t: t.reshape(1, -1).astype(F32)

    wa = ev_conv_w.shape[-1]
    rank = ev_gla_wgk.shape[1]
    wq = ev_gla_wgk.shape[2]
    wb = ev_gla_norm.shape[-1]
    dk, dv = wq // GLA_HEADS, wb // GLA_HEADS
    wc = od_ret_norm.shape[-1]
    wd = od_rwkv_w0.shape[-1]
    rw, ra, rg = od_rwkv_ww2.shape[1], od_rwkv_aw2.shape[1], od_rwkv_gw2.shape[1]
    nb = MIX_BATCH if bsz % MIX_BATCH == 0 else 1

    h = x.reshape(m, d)
    freq = (ROPE_BASE ** (-jnp.arange(wc // RET_HEADS // 2, dtype=F32) / (wc // RET_HEADS // 2))).reshape(1, -1)
    pos = positions.astype(F32).reshape(bsz, seq, 1)
    seg = _block_diag(jnp.ones((wd // RWKV_HEAD_DIM, RWKV_HEAD_DIM, RWKV_HEAD_DIM), BF16))
    assert 2 * RWKV_HEAD_DIM == LANES and wd % LANES == 0

    for i in range(depth):
        j = i // 2
        if i % 2 == 0:
            w_in = ev_w_in[j]
            cols = w_in.shape[1]
            cols_pad = -(-cols // LANES) * LANES
            w_in = jnp.pad(w_in, ((0, 0), (0, cols_pad - cols))).astype(BF16)
            wgate = jnp.concatenate([_block_diag(ev_lru_wr[j]), _block_diag(ev_lru_wi[j])], axis=1).astype(BF16)
            bgate = jnp.concatenate([ev_lru_br[j], ev_lru_bi[j]]).reshape(1, -1)
            y = _even_mixer(h.reshape(bsz, seq, d), row(mix_norm[i]), w_in, ev_conv_w[j], row(ev_conv_b[j]), wgate, bgate,
                            row(ev_lru_a[j]), ev_gla_wgk[j], row(ev_gla_bgk[j]), row(ev_gla_norm[j]),
                            nb=nb, ts=2 * MIX_CHUNK, wa=wa, dk=dk, dv=dv, rank=rank)
            w_out = ev_w_out[j]
        else:
            y = _odd_mixer(h.reshape(bsz, seq, d), pos, row(mix_norm[i]), od_w_in[j].astype(BF16), freq,
                           row(od_ret_norm[j]), row(od_rwkv_mu[j]),
                           row(od_rwkv_w0[j]), od_rwkv_ww2[j].astype(BF16), row(od_rwkv_a0[j]),
                           od_rwkv_aw2[j].astype(BF16), od_rwkv_gw2[j].astype(BF16), row(od_rwkv_kk[j]),
                           row(od_rwkv_ka[j]), row(od_rwkv_rk[j]), row(od_rwkv_norm[j]), seg,
                           nb=nb, ts=RET_CHUNK, wc=wc, wd=wd, rw=rw, ra=ra, rg=rg)
            w_out = od_w_out[j]
        h = _post(h, y.reshape(m, -1), p[i].reshape(m, -1), w_out.astype(BF16), row(ffn_norm[i]),
                  ffn_w_gate[i].astype(BF16), ffn_w_up[i].astype(BF16), ffn_w_down[i].astype(BF16),
                  row(ple_norm[i]), ple_w_gate[i].astype(BF16), row(ple_b_gate[i]), ple_w_proj[i].astype(BF16),
                  row(final_norm), tm=512, final=(i == depth - 1))
    return h.reshape(bsz, seq, d)
```

```python
import functools
import math

import jax
import jax.numpy as jnp
from jax import lax
from jax.experimental import pallas as pl
from jax.experimental.pallas import tpu as pltpu

F32 = jnp.float32
BF16 = jnp.bfloat16
HIGHEST = lax.Precision.HIGHEST

NORM_EPS = 1e-6
LANES = 128
SUBLANES = 8
LRU_BLOCKS = 8
CONV_W = 4
LRU_C = 8.0
GLA_HEADS = 4
GLA_GATE_NORM = 16.0
GLA_SUB = 16
RET_HEADS = 4
RET_CHUNK = 128
ROPE_BASE = 10000.0
RWKV_HEAD_DIM = 64
RWKV_LN_EPS = 64e-5
HEAD_NORM_EPS = 1e-5

MIX_CHUNK = 64
MIX_BATCH = 2
VMEM_LIMIT = 56 * 1024 * 1024


def _log_sigmoid(x):
    return jnp.minimum(x, 0.0) - jnp.log1p(jnp.exp(-jnp.abs(x)))


def _softplus(x):
    return jnp.maximum(x, 0.0) + jnp.log1p(jnp.exp(-jnp.abs(x)))


def _silu(x):
    return x * jax.nn.sigmoid(x)


def _gelu_tanh(x):
    return 0.5 * x * (1.0 + jnp.tanh(math.sqrt(2.0 / math.pi) * (x + 0.044715 * (x * x * x))))


def _rms(x, g):
    return x * lax.rsqrt(jnp.mean(x * x, axis=-1, keepdims=True) + NORM_EPS) * g


def _dot(a, b, precision=None):
    return jnp.dot(a, b, preferred_element_type=F32, precision=precision)


def _dot_nt(a, b, precision=None):
    return lax.dot_general(a, b, (((1,), (1,)), ((), ())), preferred_element_type=F32, precision=precision)


def _dot_tn(a, b, precision=None):
    return lax.dot_general(a, b, (((0,), (0,)), ((), ())), preferred_element_type=F32, precision=precision)


def _dot_split_lhs(x, m, passes):
    acc = None
    for _ in range(passes):
        piece = x.astype(BF16)
        acc = _dot(piece, m) if acc is None else acc + _dot(piece, m)
        x = x - piece.astype(F32)
    return acc


def _dot_split_rhs(m, x, passes):
    acc = None
    for _ in range(passes):
        piece = x.astype(BF16)
        acc = _dot(m, piece) if acc is None else acc + _dot(m, piece)
        x = x - piece.astype(F32)
    return acc


def _tri(n, strict=False):
    row = lax.broadcasted_iota(jnp.int32, (n, n), 0)
    col = lax.broadcasted_iota(jnp.int32, (n, n), 1)
    return (row > col) if strict else (row >= col)


PROJ_GROUP = 512


def _project_steps(h_ref, mnorm_ref, win_ref, proj_ref):
    xn = _rms(h_ref[...], mnorm_ref[...]).astype(BF16)
    cols = proj_ref.shape[1]
    for c0 in range(0, cols, PROJ_GROUP):
        c1 = min(c0 + PROJ_GROUP, cols)
        proj_ref[:, c0:c1] = _dot(xn, win_ref[:, c0:c1])
        yield


def _interleave(*gens):
    gens = list(gens)
    while gens:
        for g in list(gens):
            try:
                next(g)
            except StopIteration:
                gens.remove(g)


def _resident(t):
    return pl.BlockSpec(t.shape, lambda b, s: (0,) * t.ndim, pipeline_mode=pl.Buffered(1))


def _rg_lru_steps(proj_ref, y_ref, xtail_ref, hcar_ref, convw_ref, convb_ref, wgate_ref, bgate_ref, lrua_ref, *, ts, wa):
    ax = proj_ref[:, 0:wa]
    xcat = jnp.concatenate([xtail_ref[...], ax], axis=0)
    xtail_ref[...] = ax[ts - 8:ts]
    cw = convw_ref[...]
    xc = convb_ref[...] + cw[CONV_W - 1:CONV_W] * ax
    for s in range(1, CONV_W):
        xc = xc + cw[CONV_W - 1 - s:CONV_W - s] * xcat[8 - s:8 - s + ts]
    yield
    gates = _dot(xc.astype(BF16), wgate_ref[...]) + bgate_ref[...]
    r = jax.nn.sigmoid(gates[:, :wa])
    i = jax.nn.sigmoid(gates[:, wa:])
    yield
    log_a = LRU_C * r * _log_sigmoid(lrua_ref[...])
    a = jnp.exp(log_a)
    t = jnp.tanh(log_a)
    u = jnp.sqrt(-2.0 * t / (1.0 - t)) * (i * xc)
    yield
    groups = ts // SUBLANES
    u = u.reshape(groups, SUBLANES, wa)
    a = a.reshape(groups, SUBLANES, wa)
    row = lax.broadcasted_iota(jnp.int32, (groups, SUBLANES, wa), 1)
    d = 1
    while d < SUBLANES:
        keep = row >= d
        u = jnp.where(keep, a * pltpu.roll(u, d, 1) + u, u)
        a = jnp.where(keep, a * pltpu.roll(a, d, 1), a)
        d *= 2
        yield
    carry = hcar_ref[...]
    pieces = []
    for j in range(groups):
        hj = u[j] + a[j] * carry
        carry = hj[SUBLANES - 1:SUBLANES]
        pieces.append(hj)
    hcar_ref[...] = carry
    h = jnp.concatenate(pieces, axis=0)
    yield
    y_ref[:, 0:wa] = (h * _gelu_tanh(proj_ref[:, wa:2 * wa])).astype(y_ref.dtype)
    yield


def _even_mixer_kernel(h_ref, mnorm_ref, win_ref, convw_ref, convb_ref, wgate_ref, bgate_ref, lrua_ref,
                       wgk_ref, bgk_ref, gnorm_ref, y_ref, proj_ref, xtail_ref, hcar_ref, st_ref,
                       *, nb, ts, wa, dk, dv, rank):
    heads = GLA_HEADS
    wq = heads * dk
    wb = heads * dv
    o_q = 2 * wa
    o_k, o_v = o_q + wq, o_q + 2 * wq
    o_g, o_gk = o_v + wb, o_v + 2 * wb
    step = pl.program_id(1)

    @pl.when(step == 0)
    def _():
        xtail_ref[...] = jnp.zeros_like(xtail_ref)
        hcar_ref[...] = jnp.zeros_like(hcar_ref)
        st_ref[...] = jnp.zeros_like(st_ref)

    project = [_project_steps(h_ref.at[b], mnorm_ref, win_ref, proj_ref.at[b]) for b in range(nb)]
    rg_lru = [_rg_lru_steps(proj_ref.at[b], y_ref.at[b], xtail_ref.at[b], hcar_ref.at[b], convw_ref, convb_ref,
                            wgate_ref, bgate_ref, lrua_ref, ts=ts, wa=wa) for b in range(nb)]
    _interleave(project[0])
    for b in range(nb):
        _interleave(rg_lru[b], *project[b + 1:b + 2])

    cg = MIX_CHUNK
    tri_incl = jnp.where(_tri(cg), 1.0, 0.0).astype(BF16)
    causal = _tri(cg)
    rowc = lax.broadcasted_iota(jnp.int32, (cg, wq), 0)
    items = []
    for b in range(nb):
        z = _dot(proj_ref[b, :, o_gk:o_gk + rank], wgk_ref[...], precision=HIGHEST) + bgk_ref[...]
        log_f = _log_sigmoid(z) * (1.0 / GLA_GATE_NORM)
        for c in range(ts // cg):
            r0 = c * cg
            cum = _dot_split_rhs(tri_incl, log_f[r0:r0 + cg], 3)
            q = proj_ref[b, r0:r0 + cg, o_q:o_q + wq] * (dk ** -0.5)
            k = proj_ref[b, r0:r0 + cg, o_k:o_k + wq]
            v = proj_ref[b, r0:r0 + cg, o_v:o_v + wb].astype(BF16)
            cum_last = cum[cg - 1:cg]
            it = dict(b=b, c=c, v=v, q_in=(q * jnp.exp(cum)).astype(BF16),
                      k_out=(k * jnp.exp(cum_last - cum)).astype(BF16), chunk_decay=jnp.exp(cum_last), qb=[], kb=[])
            for ib in range(cg // GLA_SUB):
                b0 = ib * GLA_SUB
                base = cum[b0 - 1:b0] if ib > 0 else jnp.zeros((1, wq), F32)
                it["qb"].append((q[b0:b0 + GLA_SUB] * jnp.exp(cum[b0:b0 + GLA_SUB] - base)).astype(BF16))
                it["kb"].append(jnp.where(rowc < b0 + GLA_SUB, k * jnp.exp(base - cum), 0.0).astype(BF16))
            items.append(it)
    for it in items:
        it["scores"] = [
            jnp.where(causal, jnp.concatenate(
                [_dot_nt(qb[:, hd * dk:(hd + 1) * dk], kb[:, hd * dk:(hd + 1) * dk])
                 for qb, kb in zip(it["qb"], it["kb"])], axis=0), 0.0).astype(BF16)
            for hd in range(heads)]
    for it in items:
        it["intra"] = [_dot(it["scores"][hd], it["v"][:, hd * dv:(hd + 1) * dv]) for hd in range(heads)]
    for c in range(ts // cg):
        r0 = c * cg
        for it in [t for t in items if t["c"] == c]:
            b = it["b"]
            outs = []
            for hd in range(heads):
                ks = slice(hd * dk, (hd + 1) * dk)
                vs = slice(hd * dv, (hd + 1) * dv)
                st = st_ref[b * heads + hd]
                o = it["intra"][hd] + _dot_nt(it["q_in"][:, ks], st.astype(BF16))
                st_ref[b * heads + hd] = st * it["chunk_decay"][:, ks] + _dot_tn(it["v"][:, vs], it["k_out"][:, ks])
                outs.append(o * lax.rsqrt(jnp.mean(o * o, axis=-1, keepdims=True) + HEAD_NORM_EPS))
            o_all = jnp.concatenate(outs, axis=-1) * gnorm_ref[...]
            g = proj_ref[b, r0:r0 + cg, o_g:o_g + wb]
            y_ref[b, r0:r0 + cg, wa:wa + wb] = (o_all * _silu(g)).astype(y_ref.dtype)


def _even_mixer(h, mnorm, win, convw, convb, wgate, bgate, lrua, wgk, bgk, gnorm, *, nb, ts, wa, dk, dv, rank):
    bsz, seq, d = h.shape
    cols = win.shape[1]
    heads = GLA_HEADS
    wb = heads * dv
    consts = (mnorm, win, convw, convb, wgate, bgate, lrua, wgk, bgk, gnorm)
    kern = functools.partial(_even_mixer_kernel, nb=nb, ts=ts, wa=wa, dk=dk, dv=dv, rank=rank)
    return pl.pallas_call(
        kern,
        grid=(bsz // nb, seq // ts),
        in_specs=[pl.BlockSpec((nb, ts, d), lambda b, s: (b, s, 0))] + [_resident(t) for t in consts],
        out_specs=pl.BlockSpec((nb, ts, wa + wb), lambda b, s: (b, s, 0)),
        out_shape=jax.ShapeDtypeStruct((bsz, seq, wa + wb), BF16),
        scratch_shapes=[pltpu.VMEM((nb, ts, cols), F32), pltpu.VMEM((nb, 8, wa), F32),
                        pltpu.VMEM((nb, 1, wa), F32), pltpu.VMEM((nb * heads, dv, dk), F32)],
        compiler_params=pltpu.CompilerParams(dimension_semantics=("arbitrary", "arbitrary"),
                                             vmem_limit_bytes=VMEM_LIMIT),
        name="even_mixer",
    )(h, *consts)


def _odd_mixer_kernel(h_ref, pos_ref, mnorm_ref, win_ref, freq_ref, rnorm_ref, mu_ref, w0_ref, ww2_ref, a0_ref,
                      aw2_ref, gw2_ref, kk_ref, ka_ref, rk_ref, wnorm_ref, seg_ref,
                      y_ref, proj_ref, rstate_ref, dprev_ref, wstate_ref, *, nb, ts, wc, wd, rw, ra, rg):
    f32 = F32

    @pl.when(pl.program_id(1) == 0)
    def _():
        rstate_ref[...] = jnp.zeros_like(rstate_ref)
        dprev_ref[...] = jnp.zeros_like(dprev_ref)
        wstate_ref[...] = jnp.zeros_like(wstate_ref)

    for b in range(nb):
        _interleave(_project_steps(h_ref.at[b], mnorm_ref, win_ref, proj_ref.at[b]))

    o_d = 4 * wc
    dcols = 3 * wd + rw + ra + rg
    hdim = RWKV_HEAD_DIM
    cm = MIX_CHUNK
    n = 2 * cm
    groups = wd // LANES
    seg = seg_ref[...]
    lane = lax.broadcasted_iota(jnp.int32, (cm, LANES), 1)
    first_head = lane < hdim

    def stack_heads(t):
        return jnp.concatenate([jnp.where(first_head, t, 0.0), jnp.where(first_head, 0.0, t)], axis=0)

    rin = lax.broadcasted_iota(jnp.int32, (n, n), 0)
    cin = lax.broadcasted_iota(jnp.int32, (n, n), 1)
    strict = (rin & (cm - 1)) > (cin & (cm - 1))
    incl = (rin & (cm - 1)) >= (cin & (cm - 1))
    eye = jnp.where(rin == cin, 1.0, 0.0).astype(f32)
    tri_incl = jnp.where(_tri(cm), 1.0, 0.0).astype(BF16)

    work = []
    tails = {}

    def prepare_steps(b):
        dpart = proj_ref[b, :, o_d:o_d + dcols]
        dshift = jnp.concatenate([dprev_ref[b], dpart], axis=0)[7:7 + ts]
        dprev_ref[b] = dpart[ts - 8:ts]
        dpart = dpart + mu_ref[...] * (dshift - dpart)
        yield
        r = dpart[:, 0:wd]
        k = dpart[:, wd:2 * wd]
        v = dpart[:, 2 * wd:3 * wd]
        w_lr = dpart[:, 3 * wd:3 * wd + rw]
        a_lr = dpart[:, 3 * wd + rw:3 * wd + rw + ra]
        g_lr = dpart[:, 3 * wd + rw + ra:dcols]
        w = -_softplus(-(w0_ref[...] + _dot(jnp.tanh(w_lr).astype(BF16), ww2_ref[...]))) - 0.5
        log_decay = -jnp.exp(w)
        a = jax.nn.sigmoid(a0_ref[...] + _dot(a_lr.astype(BF16), aw2_ref[...]))
        g = _dot(jax.nn.sigmoid(g_lr).astype(BF16), gw2_ref[...])
        yield
        kk = k * kk_ref[...]
        kk = kk * lax.rsqrt(_dot_split_lhs(kk * kk, seg, 2) + 1e-12)
        k = k * (1.0 + (a - 1.0) * ka_ref[...])
        tails[b] = (_dot_split_lhs(r * k * rk_ref[...], seg, 2) * v, g)
        kb = kk * a
        yield
        for ch in range(ts // cm):
            rs = slice(ch * cm, (ch + 1) * cm)
            ld = log_decay[rs]
            cum = _dot_split_rhs(tri_incl, ld, 3)
            cum_last = cum[cm - 1:cm]
            p_incl = jnp.exp(cum)
            p_inv = jnp.exp(-cum)
            a_t = -kk[rs] * jnp.exp(cum - ld)
            r_t = r[rs] * p_incl
            b_t = kb[rs] * p_inv
            k_t = k[rs] * p_inv
            p_out = jnp.exp(cum_last - cum)
            b_o = kb[rs] * p_out
            k_o = k[rs] * p_out
            p_last = jnp.exp(cum_last)
            yield
            for pr in range(groups):
                ls = slice(pr * LANES, (pr + 1) * LANES)
                l4 = jnp.concatenate([stack_heads(a_t[:, ls]), stack_heads(r_t[:, ls])], axis=0).astype(BF16)
                r4 = jnp.concatenate([stack_heads(b_t[:, ls]), stack_heads(k_t[:, ls])], axis=0).astype(BF16)
                o4 = jnp.concatenate([stack_heads(b_o[:, ls]), stack_heads(k_o[:, ls])], axis=0).astype(BF16)
                vbd = stack_heads(v[rs, ls]).astype(BF16)
                work.append(dict(b=b, ch=ch, pr=pr, l4=l4, r4=r4, o4=o4, vbd=vbd, p_last=p_last[:, ls]))
                yield

    for b in range(nb):
        _interleave(prepare_steps(b))
    for wk in work:
        big = _dot_nt(wk["l4"], wk["r4"])
        a_ab = jnp.where(strict, big[:n, :n], 0.0)
        wk["a_ak"] = jnp.where(strict, big[:n, n:], 0.0).astype(BF16)
        wk["a_r"] = jnp.concatenate([jnp.where(incl, big[n:, :n], 0.0), jnp.where(incl, big[n:, n:], 0.0)],
                                    axis=1).astype(BF16)
        wk["x"] = a_ab.astype(BF16)
        wk["tinv"] = eye + a_ab
    span = 2
    while span < cm:
        for wk in work:
            wk["x"] = _dot(wk["x"], wk["x"]).astype(BF16)
        for wk in work:
            wk["tinv"] = wk["tinv"] + _dot(wk["tinv"].astype(BF16), wk["x"])
        span *= 2

    c = ts
    dh = wc // RET_HEADS
    half = dh // 2
    rowi = lax.broadcasted_iota(jnp.int32, (c, c), 0)
    coli = lax.broadcasted_iota(jnp.int32, (c, c), 1)
    rel = (rowi - coli).astype(f32)
    rowd = lax.broadcasted_iota(jnp.int32, (c, dh), 0).astype(f32)
    ret = []
    for b in range(nb):
        ang = freq_ref[...] * pos_ref[b]
        cos, sin = jnp.cos(ang).T, jnp.sin(ang).T
        cos2 = jnp.concatenate([cos, cos], axis=-1)
        sin2 = jnp.concatenate([-sin, sin], axis=-1)
        for hd in range(RET_HEADS):
            log_gamma = math.log1p(-2.0 ** (-5.0 - hd))
            q = proj_ref[b, :, hd * dh:(hd + 1) * dh]
            k = proj_ref[b, :, wc + hd * dh:wc + (hd + 1) * dh]
            v = proj_ref[b, :, 2 * wc + hd * dh:2 * wc + (hd + 1) * dh].astype(BF16)
            q = (q * cos2 + pltpu.roll(q, half, 1) * sin2) * (dh ** -0.5)
            k = k * cos2 + pltpu.roll(k, half, 1) * sin2
            dmask = jnp.where(rel >= 0.0, jnp.exp(jnp.maximum(rel, 0.0) * log_gamma), 0.0)
            ret.append(dict(b=b, hd=hd, v=v, q=q.astype(BF16), k=k.astype(BF16), dmask=dmask,
                            q_in=(q * jnp.exp((rowd + 1.0) * log_gamma)).astype(BF16),
                            k_out=(k * jnp.exp((c - 1.0 - rowd) * log_gamma)).astype(BF16),
                            decay=math.exp(c * log_gamma)))
    for rt in ret:
        rt["scores"] = (_dot_nt(rt["q"], rt["k"]) * rt["dmask"]).astype(BF16)
    for rt in ret:
        st = rstate_ref[rt["b"] * RET_HEADS + rt["hd"]]
        o = _dot(rt["scores"], rt["v"]) + _dot(rt["q_in"], st.astype(BF16))
        rstate_ref[rt["b"] * RET_HEADS + rt["hd"]] = st * rt["decay"] + _dot_tn(rt["k_out"], rt["v"])
        o = o - jnp.mean(o, axis=-1, keepdims=True)
        rt["o"] = o * lax.rsqrt(jnp.mean(o * o, axis=-1, keepdims=True) + HEAD_NORM_EPS)
    for b in range(nb):
        g = proj_ref[b, :, 3 * wc:4 * wc]
        o_all = jnp.concatenate([rt["o"] for rt in ret if rt["b"] == b], axis=-1)
        y_ref[b, :, 0:wc] = (o_all * rnorm_ref[...] * _silu(g)).astype(y_ref.dtype)

    for ch in range(ts // cm):
        rs = slice(ch * cm, (ch + 1) * cm)
        wks = [wk for wk in work if wk["ch"] == ch]
        sidx = [wk["b"] * groups + wk["pr"] for wk in wks]
        sts = [wstate_ref[i] for i in sidx]
        fss = [_dot_nt(wk["l4"], st.astype(BF16)) for wk, st in zip(wks, sts)]
        rhs = [(fs[:n] + _dot(wk["a_ak"], wk["vbd"])).astype(BF16) for wk, fs in zip(wks, fss)]
        us = [_dot(wk["tinv"].astype(BF16), rh) for wk, rh in zip(wks, rhs)]
        uvs = [jnp.concatenate([u.astype(BF16), wk["vbd"]], axis=0) for wk, u in zip(wks, us)]
        for i, st, wk, uv in zip(sidx, sts, wks, uvs):
            wstate_ref[i] = st * wk["p_last"] + _dot_tn(uv, wk["o4"])
        ybds = [fs[n:] + _dot(wk["a_r"], uv) for wk, fs, uv in zip(wks, fss, uvs)]
        for b in range(nb):
            y = jnp.concatenate([yb[:cm] + yb[cm:] for wk, yb in zip(wks, ybds) if wk["b"] == b], axis=-1)
            inv_h = 1.0 / hdim
            y = y - _dot_split_lhs(y, seg, 2) * inv_h
            y = y * lax.rsqrt(_dot_split_lhs(y * y, seg, 2) * inv_h + RWKV_LN_EPS)
            bonus, g = tails[b]
            y = (y * wnorm_ref[...] + bonus[rs]) * g[rs]
            y_ref[b, ch * cm:(ch + 1) * cm, wc:wc + wd] = y.astype(y_ref.dtype)


def _odd_mixer(h, pos, mnorm, win, freq, rnorm, mu, w0, ww2, a0, aw2, gw2, kk, ka, rk, wnorm, seg, *, nb, ts, wc, wd, rw,
               ra, rg):
    bsz, seq, d = h.shape
    cols = win.shape[1]
    dh = wc // RET_HEADS
    consts = (mnorm, win, freq, rnorm, mu, w0, ww2, a0, aw2, gw2, kk, ka, rk, wnorm, seg)
    kern = functools.partial(_odd_mixer_kernel, nb=nb, ts=ts, wc=wc, wd=wd, rw=rw, ra=ra, rg=rg)
    dcols = 3 * wd + rw + ra + rg
    return pl.pallas_call(
        kern,
        grid=(bsz // nb, seq // ts),
        in_specs=[pl.BlockSpec((nb, ts, d), lambda b, s: (b, s, 0)),
                  pl.BlockSpec((nb, 1, ts), lambda b, s: (b, 0, s))] + [_resident(t) for t in consts],
        out_specs=pl.BlockSpec((nb, ts, wc + wd), lambda b, s: (b, s, 0)),
        out_shape=jax.ShapeDtypeStruct((bsz, seq, wc + wd), BF16),
        scratch_shapes=[pltpu.VMEM((nb, ts, cols), F32), pltpu.VMEM((nb * RET_HEADS, dh, dh), F32),
                        pltpu.VMEM((nb, 8, dcols), F32), pltpu.VMEM((nb * (wd // LANES), LANES, LANES), F32)],
        compiler_params=pltpu.CompilerParams(dimension_semantics=("arbitrary", "arbitrary"),
                                             vmem_limit_bytes=VMEM_LIMIT),
        name="odd_mixer",
    )(h, pos, *consts)


def _post_kernel(h_ref, y_ref, p_ref, wout_ref, fnorm_ref, wg_ref, wu_ref, wd_ref, pnorm_ref, pwg_ref, pbg_ref,
                 pwp_ref, onorm_ref, o_ref, *, final):
    h = h_ref[...] + _dot(y_ref[...], wout_ref[...])
    xn = _rms(h, fnorm_ref[...]).astype(BF16)
    act = (_silu(_dot(xn, wg_ref[...])) * _dot(xn, wu_ref[...])).astype(BF16)
    h = h + _dot(act, wd_ref[...])
    xg = _rms(h, pnorm_ref[...]).astype(BF16)
    gate = jax.nn.sigmoid(_dot(xg, pwg_ref[...]) + pbg_ref[...])
    h = h + gate * _dot(p_ref[...].astype(BF16), pwp_ref[...])
    if final:
        h = _rms(h, onorm_ref[...])
    o_ref[...] = h


def _post(h, y, p, wout, fnorm, wg, wu, wd, pnorm, pwg, pbg, pwp, onorm, *, layer, tm, final):
    m, d = h.shape
    rows = lambda width: pl.BlockSpec((tm, width), lambda i: (i, 0))
    resident = lambda t: pl.BlockSpec(t.shape, lambda i: (0, 0), pipeline_mode=pl.Buffered(1))
    of_layer = lambda t: pl.BlockSpec((None,) + t.shape[1:], lambda i: (layer, 0, 0), pipeline_mode=pl.Buffered(1))
    stacked = (fnorm, wg, wu, wd, pnorm, pwg, pbg, pwp)
    return pl.pallas_call(
        functools.partial(_post_kernel, final=final),
        grid=(m // tm,),
        in_specs=[rows(d), rows(y.shape[1]), pl.BlockSpec((None, tm, p.shape[2]), lambda i: (layer, i, 0)),
                  resident(wout)] + [of_layer(t) for t in stacked] + [resident(onorm)],
        out_specs=rows(d),
        out_shape=jax.ShapeDtypeStruct((m, d), F32),
        compiler_params=pltpu.CompilerParams(dimension_semantics=("arbitrary",), vmem_limit_bytes=VMEM_LIMIT),
        name="post_mixer",
    )(h, y, p, wout, *stacked, onorm)


def _block_diag(blocks):
    g, bi, bj = blocks.shape
    eye = jnp.eye(g, dtype=blocks.dtype)
    return (eye[:, None, :, None] * blocks[:, :, None, :]).reshape(g * bi, g * bj)


def kernel(x, p, positions, ev_w_in, ev_conv_w, ev_conv_b, ev_lru_wr, ev_lru_br, ev_lru_wi, ev_lru_bi, ev_lru_a, ev_gla_wgk, ev_gla_bgk, ev_gla_norm, ev_w_out, od_w_in, od_ret_norm, od_rwkv_mu, od_rwkv_w0, od_rwkv_ww2, od_rwkv_a0, od_rwkv_aw2, od_rwkv_gw2, od_rwkv_kk, od_rwkv_ka, od_rwkv_rk, od_rwkv_norm, od_w_out, mix_norm, ffn_norm, ffn_w_gate, ffn_w_up, ffn_w_down, ple_norm, ple_w_gate, ple_b_gate, ple_w_proj, final_norm):
    bsz, seq, d = x.shape
    depth = p.shape[0]
    m = bsz * seq
    row = lambda t: t.reshape(1, -1).astype(F32)
    rows3 = lambda t: t.reshape(t.shape[0], 1, -1).astype(F32)

    wa = ev_conv_w.shape[-1]
    rank = ev_gla_wgk.shape[1]
    wq = ev_gla_wgk.shape[2]
    wb = ev_gla_norm.shape[-1]
    dk, dv = wq // GLA_HEADS, wb // GLA_HEADS
    wc = od_ret_norm.shape[-1]
    wd = od_rwkv_w0.shape[-1]
    rw, ra, rg = od_rwkv_ww2.shape[1], od_rwkv_aw2.shape[1], od_rwkv_gw2.shape[1]
    nb = MIX_BATCH if bsz % MIX_BATCH == 0 else 1

    h = x.reshape(m, d)
    freq = (ROPE_BASE ** (-jnp.arange(wc // RET_HEADS // 2, dtype=F32) / (wc // RET_HEADS // 2))).reshape(-1, 1)
    pos = positions.astype(F32).reshape(bsz, 1, seq)
    seg = _block_diag(jnp.ones((wd // RWKV_HEAD_DIM, RWKV_HEAD_DIM, RWKV_HEAD_DIM), BF16))
    assert 2 * RWKV_HEAD_DIM == LANES and wd % LANES == 0

    for i in range(depth):
        j = i // 2
        if i % 2 == 0:
            w_in = ev_w_in[j]
            cols = w_in.shape[1]
            cols_pad = -(-cols // LANES) * LANES
            w_in = jnp.pad(w_in, ((0, 0), (0, cols_pad - cols))).astype(BF16)
            wgate = jnp.concatenate([_block_diag(ev_lru_wr[j]), _block_diag(ev_lru_wi[j])], axis=1).astype(BF16)
            bgate = jnp.concatenate([ev_lru_br[j], ev_lru_bi[j]]).reshape(1, -1)
            y = _even_mixer(h.reshape(bsz, seq, d), row(mix_norm[i]), w_in, ev_conv_w[j], row(ev_conv_b[j]), wgate, bgate,
                            row(ev_lru_a[j]), ev_gla_wgk[j], row(ev_gla_bgk[j]), row(ev_gla_norm[j]),
                            nb=nb, ts=4 * MIX_CHUNK, wa=wa, dk=dk, dv=dv, rank=rank)
            w_out = ev_w_out[j]
        else:
            y = _odd_mixer(h.reshape(bsz, seq, d), pos, row(mix_norm[i]), od_w_in[j].astype(BF16), freq,
                           row(od_ret_norm[j]), row(od_rwkv_mu[j]),
                           row(od_rwkv_w0[j]), od_rwkv_ww2[j].astype(BF16), row(od_rwkv_a0[j]),
                           od_rwkv_aw2[j].astype(BF16), od_rwkv_gw2[j].astype(BF16), row(od_rwkv_kk[j]),
                           row(od_rwkv_ka[j]), row(od_rwkv_rk[j]), row(od_rwkv_norm[j]), seg,
                           nb=nb, ts=RET_CHUNK, wc=wc, wd=wd, rw=rw, ra=ra, rg=rg)
            w_out = od_w_out[j]
        h = _post(h, y.reshape(m, -1), p.reshape(depth, m, -1), w_out.astype(BF16), rows3(ffn_norm),
                  ffn_w_gate.astype(BF16), ffn_w_up.astype(BF16), ffn_w_down.astype(BF16),
                  rows3(ple_norm), ple_w_gate.astype(BF16), rows3(ple_b_gate), ple_w_proj.astype(BF16),
                  row(final_norm), layer=i, tm=512, final=(i == depth - 1))
    return h.reshape(bsz, seq, d)
```

```python
import functools
import math

import jax
import jax.numpy as jnp
from jax import lax
from jax.experimental import pallas as pl
from jax.experimental.pallas import tpu as pltpu

F32 = jnp.float32
BF16 = jnp.bfloat16
HIGHEST = lax.Precision.HIGHEST

NORM_EPS = 1e-6
LANES = 128
SUBLANES = 8
LRU_BLOCKS = 8
CONV_W = 4
LRU_C = 8.0
GLA_HEADS = 4
GLA_GATE_NORM = 16.0
GLA_SUB = 16
RET_HEADS = 4
RET_CHUNK = 128
ROPE_BASE = 10000.0
RWKV_HEAD_DIM = 64
RWKV_LN_EPS = 64e-5
HEAD_NORM_EPS = 1e-5

MIX_CHUNK = 64
MIX_BATCH = 2
VMEM_LIMIT = 56 * 1024 * 1024


def _log_sigmoid(x):
    return jnp.minimum(x, 0.0) - jnp.log1p(jnp.exp(-jnp.abs(x)))


def _softplus(x):
    return jnp.maximum(x, 0.0) + jnp.log1p(jnp.exp(-jnp.abs(x)))


def _silu(x):
    return x * jax.nn.sigmoid(x)


def _gelu_tanh(x):
    return 0.5 * x * (1.0 + jnp.tanh(math.sqrt(2.0 / math.pi) * (x + 0.044715 * (x * x * x))))


def _rms(x, g):
    return x * lax.rsqrt(jnp.mean(x * x, axis=-1, keepdims=True) + NORM_EPS) * g


def _dot(a, b, precision=None):
    return jnp.dot(a, b, preferred_element_type=F32, precision=precision)


def _dot_nt(a, b, precision=None):
    return lax.dot_general(a, b, (((1,), (1,)), ((), ())), preferred_element_type=F32, precision=precision)


def _dot_tn(a, b, precision=None):
    return lax.dot_general(a, b, (((0,), (0,)), ((), ())), preferred_element_type=F32, precision=precision)


def _head_sums(x, seg):
    return jnp.concatenate([_dot(x[:, g:g + LANES].astype(BF16), seg) for g in range(0, x.shape[1], LANES)], axis=-1)


def _dot_split_rhs(m, x, passes):
    acc = None
    for _ in range(passes):
        piece = x.astype(BF16)
        acc = _dot(m, piece) if acc is None else acc + _dot(m, piece)
        x = x - piece.astype(F32)
    return acc


def _tri(n, strict=False):
    row = lax.broadcasted_iota(jnp.int32, (n, n), 0)
    col = lax.broadcasted_iota(jnp.int32, (n, n), 1)
    return (row > col) if strict else (row >= col)


PROJ_GROUP = 512


def _project_steps(h_ref, mnorm_ref, win_ref, proj_ref):
    xn = _rms(h_ref[...], mnorm_ref[...]).astype(BF16)
    cols = proj_ref.shape[1]
    for c0 in range(0, cols, PROJ_GROUP):
        c1 = min(c0 + PROJ_GROUP, cols)
        proj_ref[:, c0:c1] = _dot(xn, win_ref[:, c0:c1])
        yield


def _interleave(*gens):
    gens = list(gens)
    while gens:
        for g in list(gens):
            try:
                next(g)
            except StopIteration:
                gens.remove(g)


def _resident(t):
    return pl.BlockSpec(t.shape, lambda b, s: (0,) * t.ndim, pipeline_mode=pl.Buffered(1))


def _rg_lru_steps(proj_ref, y_ref, xtail_ref, hcar_ref, convw_ref, convb_ref, wgate_ref, bgate_ref, lrua_ref, *, ts, wa):
    ax = proj_ref[:, 0:wa]
    xcat = jnp.concatenate([xtail_ref[...], ax], axis=0)
    xtail_ref[...] = ax[ts - 8:ts]
    cw = convw_ref[...]
    xc = convb_ref[...] + cw[CONV_W - 1:CONV_W] * ax
    for s in range(1, CONV_W):
        xc = xc + cw[CONV_W - 1 - s:CONV_W - s] * xcat[8 - s:8 - s + ts]
    yield
    gates = _dot(xc.astype(BF16), wgate_ref[...]) + bgate_ref[...]
    r = jax.nn.sigmoid(gates[:, :wa])
    i = jax.nn.sigmoid(gates[:, wa:])
    yield
    log_a = LRU_C * r * _log_sigmoid(lrua_ref[...])
    a = jnp.exp(log_a)
    t = jnp.tanh(log_a)
    u = jnp.sqrt(-2.0 * t / (1.0 - t)) * (i * xc)
    yield
    groups = ts // SUBLANES
    u = u.reshape(groups, SUBLANES, wa)
    a = a.reshape(groups, SUBLANES, wa)
    row = lax.broadcasted_iota(jnp.int32, (groups, SUBLANES, wa), 1)
    d = 1
    while d < SUBLANES:
        keep = row >= d
        u = jnp.where(keep, a * pltpu.roll(u, d, 1) + u, u)
        a = jnp.where(keep, a * pltpu.roll(a, d, 1), a)
        d *= 2
        yield
    carry = hcar_ref[...]
    pieces = []
    for j in range(groups):
        hj = u[j] + a[j] * carry
        carry = hj[SUBLANES - 1:SUBLANES]
        pieces.append(hj)
    hcar_ref[...] = carry
    h = jnp.concatenate(pieces, axis=0)
    yield
    y_ref[:, 0:wa] = (h * _gelu_tanh(proj_ref[:, wa:2 * wa])).astype(y_ref.dtype)
    yield


def _even_mixer_kernel(h_ref, mnorm_ref, win_ref, convw_ref, convb_ref, wgate_ref, bgate_ref, lrua_ref,
                       wgk_ref, bgk_ref, gnorm_ref, y_ref, proj_ref, xtail_ref, hcar_ref, st_ref,
                       *, nb, ts, wa, dk, dv, rank):
    heads = GLA_HEADS
    wq = heads * dk
    wb = heads * dv
    o_q = 2 * wa
    o_k, o_v = o_q + wq, o_q + 2 * wq
    o_g, o_gk = o_v + wb, o_v + 2 * wb
    step = pl.program_id(1)

    @pl.when(step == 0)
    def _():
        xtail_ref[...] = jnp.zeros_like(xtail_ref)
        hcar_ref[...] = jnp.zeros_like(hcar_ref)
        st_ref[...] = jnp.zeros_like(st_ref)

    project = [_project_steps(h_ref.at[b], mnorm_ref, win_ref, proj_ref.at[b]) for b in range(nb)]
    rg_lru = [_rg_lru_steps(proj_ref.at[b], y_ref.at[b], xtail_ref.at[b], hcar_ref.at[b], convw_ref, convb_ref,
                            wgate_ref, bgate_ref, lrua_ref, ts=ts, wa=wa) for b in range(nb)]
    _interleave(project[0])
    for b in range(nb):
        _interleave(rg_lru[b], *project[b + 1:b + 2])

    cg = MIX_CHUNK
    tri_incl = jnp.where(_tri(cg), 1.0, 0.0).astype(BF16)
    causal = _tri(cg)
    rowc = lax.broadcasted_iota(jnp.int32, (cg, wq), 0)
    items = []
    for b in range(nb):
        z = _dot(proj_ref[b, :, o_gk:o_gk + rank], wgk_ref[...], precision=HIGHEST) + bgk_ref[...]
        log_f = _log_sigmoid(z) * (1.0 / GLA_GATE_NORM)
        for c in range(ts // cg):
            r0 = c * cg
            cum = _dot_split_rhs(tri_incl, log_f[r0:r0 + cg], 3)
            q = proj_ref[b, r0:r0 + cg, o_q:o_q + wq] * (dk ** -0.5)
            k = proj_ref[b, r0:r0 + cg, o_k:o_k + wq]
            v = proj_ref[b, r0:r0 + cg, o_v:o_v + wb].astype(BF16)
            cum_last = cum[cg - 1:cg]
            it = dict(b=b, c=c, v=v, q_in=(q * jnp.exp(cum)).astype(BF16),
                      k_out=(k * jnp.exp(cum_last - cum)).astype(BF16), chunk_decay=jnp.exp(cum_last), qb=[], kb=[])
            for ib in range(cg // GLA_SUB):
                b0 = ib * GLA_SUB
                base = cum[b0 - 1:b0] if ib > 0 else jnp.zeros((1, wq), F32)
                it["qb"].append((q[b0:b0 + GLA_SUB] * jnp.exp(cum[b0:b0 + GLA_SUB] - base)).astype(BF16))
                it["kb"].append(jnp.where(rowc < b0 + GLA_SUB, k * jnp.exp(base - cum), 0.0).astype(BF16))
            items.append(it)
    for it in items:
        it["scores"] = [
            jnp.where(causal, jnp.concatenate(
                [_dot_nt(qb[:, hd * dk:(hd + 1) * dk], kb[:, hd * dk:(hd + 1) * dk])
                 for qb, kb in zip(it["qb"], it["kb"])], axis=0), 0.0).astype(BF16)
            for hd in range(heads)]
    for it in items:
        it["intra"] = [_dot(it["scores"][hd], it["v"][:, hd * dv:(hd + 1) * dv]) for hd in range(heads)]
    for c in range(ts // cg):
        r0 = c * cg
        for it in [t for t in items if t["c"] == c]:
            b = it["b"]
            outs = []
            for hd in range(heads):
                ks = slice(hd * dk, (hd + 1) * dk)
                vs = slice(hd * dv, (hd + 1) * dv)
                st = st_ref[b * heads + hd]
                o = it["intra"][hd] + _dot_nt(it["q_in"][:, ks], st.astype(BF16))
                st_ref[b * heads + hd] = st * it["chunk_decay"][:, ks] + _dot_tn(it["v"][:, vs], it["k_out"][:, ks])
                outs.append(o * lax.rsqrt(jnp.mean(o * o, axis=-1, keepdims=True) + HEAD_NORM_EPS))
            o_all = jnp.concatenate(outs, axis=-1) * gnorm_ref[...]
            g = proj_ref[b, r0:r0 + cg, o_g:o_g + wb]
            y_ref[b, r0:r0 + cg, wa:wa + wb] = (o_all * _silu(g)).astype(y_ref.dtype)


def _even_mixer(h, mnorm, win, convw, convb, wgate, bgate, lrua, wgk, bgk, gnorm, *, nb, ts, wa, dk, dv, rank):
    bsz, seq, d = h.shape
    cols = win.shape[1]
    heads = GLA_HEADS
    wb = heads * dv
    consts = (mnorm, win, convw, convb, wgate, bgate, lrua, wgk, bgk, gnorm)
    kern = functools.partial(_even_mixer_kernel, nb=nb, ts=ts, wa=wa, dk=dk, dv=dv, rank=rank)
    return pl.pallas_call(
        kern,
        grid=(bsz // nb, seq // ts),
        in_specs=[pl.BlockSpec((nb, ts, d), lambda b, s: (b, s, 0))] + [_resident(t) for t in consts],
        out_specs=pl.BlockSpec((nb, ts, wa + wb), lambda b, s: (b, s, 0)),
        out_shape=jax.ShapeDtypeStruct((bsz, seq, wa + wb), BF16),
        scratch_shapes=[pltpu.VMEM((nb, ts, cols), F32), pltpu.VMEM((nb, 8, wa), F32),
                        pltpu.VMEM((nb, 1, wa), F32), pltpu.VMEM((nb * heads, dv, dk), F32)],
        compiler_params=pltpu.CompilerParams(dimension_semantics=("arbitrary", "arbitrary"),
                                             vmem_limit_bytes=VMEM_LIMIT),
        name="even_mixer",
    )(h, *consts)


def _odd_mixer_kernel(h_ref, pos_ref, mnorm_ref, win_ref, freq_ref, rnorm_ref, mu_ref, w0_ref, ww2_ref, a0_ref,
                      aw2_ref, gw2_ref, kk_ref, ka_ref, rk_ref, wnorm_ref, seg_ref,
                      y_ref, proj_ref, rstate_ref, dprev_ref, wstate_ref, *, nb, ts, wc, wd, rw, ra, rg):
    f32 = F32

    @pl.when(pl.program_id(1) == 0)
    def _():
        rstate_ref[...] = jnp.zeros_like(rstate_ref)
        dprev_ref[...] = jnp.zeros_like(dprev_ref)
        wstate_ref[...] = jnp.zeros_like(wstate_ref)

    for b in range(nb):
        _interleave(_project_steps(h_ref.at[b], mnorm_ref, win_ref, proj_ref.at[b]))

    o_d = 4 * wc
    dcols = 3 * wd + rw + ra + rg
    hdim = RWKV_HEAD_DIM
    cm = MIX_CHUNK
    n = 2 * cm
    groups = wd // LANES
    seg = seg_ref[...]
    lane = lax.broadcasted_iota(jnp.int32, (cm, LANES), 1)
    first_head = lane < hdim

    def stack_heads(t):
        return jnp.concatenate([jnp.where(first_head, t, 0.0), jnp.where(first_head, 0.0, t)], axis=0)

    rin = lax.broadcasted_iota(jnp.int32, (n, n), 0)
    cin = lax.broadcasted_iota(jnp.int32, (n, n), 1)
    strict = (rin & (cm - 1)) > (cin & (cm - 1))
    incl = (rin & (cm - 1)) >= (cin & (cm - 1))
    eye = jnp.where(rin == cin, 1.0, 0.0).astype(f32)
    tri_incl = jnp.where(_tri(cm), 1.0, 0.0).astype(BF16)

    work = []
    tails = {}

    def prepare_steps(b):
        dpart = proj_ref[b, :, o_d:o_d + dcols]
        dshift = jnp.concatenate([dprev_ref[b], dpart], axis=0)[7:7 + ts]
        dprev_ref[b] = dpart[ts - 8:ts]
        dpart = dpart + mu_ref[...] * (dshift - dpart)
        yield
        r = dpart[:, 0:wd]
        k = dpart[:, wd:2 * wd]
        v = dpart[:, 2 * wd:3 * wd]
        w_lr = dpart[:, 3 * wd:3 * wd + rw]
        a_lr = dpart[:, 3 * wd + rw:3 * wd + rw + ra]
        g_lr = dpart[:, 3 * wd + rw + ra:dcols]
        w = -_softplus(-(w0_ref[...] + _dot(jnp.tanh(w_lr).astype(BF16), ww2_ref[...]))) - 0.5
        log_decay = -jnp.exp(w)
        a = jax.nn.sigmoid(a0_ref[...] + _dot(a_lr.astype(BF16), aw2_ref[...]))
        g = _dot(jax.nn.sigmoid(g_lr).astype(BF16), gw2_ref[...])
        yield
        kk = k * kk_ref[...]
        kk = kk * lax.rsqrt(_head_sums(kk * kk, seg) + 1e-12)
        k = k * (1.0 + (a - 1.0) * ka_ref[...])
        tails[b] = (_head_sums(r * k * rk_ref[...], seg) * v, g)
        kb = kk * a
        yield
        for ch in range(ts // cm):
            rs = slice(ch * cm, (ch + 1) * cm)
            ld = log_decay[rs]
            cum = _dot_split_rhs(tri_incl, ld, 3)
            cum_last = cum[cm - 1:cm]
            p_incl = jnp.exp(cum)
            p_inv = jnp.exp(-cum)
            a_t = -kk[rs] * jnp.exp(cum - ld)
            r_t = r[rs] * p_incl
            b_t = kb[rs] * p_inv
            k_t = k[rs] * p_inv
            p_out = jnp.exp(cum_last - cum)
            b_o = kb[rs] * p_out
            k_o = k[rs] * p_out
            p_last = jnp.exp(cum_last)
            yield
            for pr in range(groups):
                ls = slice(pr * LANES, (pr + 1) * LANES)
                l4 = jnp.concatenate([stack_heads(a_t[:, ls]), stack_heads(r_t[:, ls])], axis=0).astype(BF16)
                r4 = jnp.concatenate([stack_heads(b_t[:, ls]), stack_heads(k_t[:, ls])], axis=0).astype(BF16)
                o4 = jnp.concatenate([stack_heads(b_o[:, ls]), stack_heads(k_o[:, ls])], axis=0).astype(BF16)
                vbd = stack_heads(v[rs, ls]).astype(BF16)
                work.append(dict(b=b, ch=ch, pr=pr, l4=l4, r4=r4, o4=o4, vbd=vbd, p_last=p_last[:, ls]))
                yield

    for b in range(nb):
        _interleave(prepare_steps(b))
    for wk in work:
        big = _dot_nt(wk["l4"], wk["r4"])
        a_ab = jnp.where(strict, big[:n, :n], 0.0)
        wk["a_ak"] = jnp.where(strict, big[:n, n:], 0.0).astype(BF16)
        wk["a_r"] = jnp.concatenate([jnp.where(incl, big[n:, :n], 0.0), jnp.where(incl, big[n:, n:], 0.0)],
                                    axis=1).astype(BF16)
        wk["x"] = a_ab.astype(BF16)
        wk["tinv"] = eye + a_ab
    span = 2
    while span < cm:
        for wk in work:
            wk["x"] = _dot(wk["x"], wk["x"]).astype(BF16)
        for wk in work:
            wk["tinv"] = wk["tinv"] + _dot(wk["tinv"].astype(BF16), wk["x"])
        span *= 2

    c = ts
    dh = wc // RET_HEADS
    half = dh // 2
    rowi = lax.broadcasted_iota(jnp.int32, (c, c), 0)
    coli = lax.broadcasted_iota(jnp.int32, (c, c), 1)
    rel = (rowi - coli).astype(f32)
    rowd = lax.broadcasted_iota(jnp.int32, (c, dh), 0).astype(f32)
    ret = []
    for b in range(nb):
        ang = freq_ref[...] * pos_ref[b]
        cos, sin = jnp.cos(ang).T, jnp.sin(ang).T
        cos2 = jnp.concatenate([cos, cos], axis=-1)
        sin2 = jnp.concatenate([-sin, sin], axis=-1)
        for hd in range(RET_HEADS):
            log_gamma = math.log1p(-2.0 ** (-5.0 - hd))
            q = proj_ref[b, :, hd * dh:(hd + 1) * dh]
            k = proj_ref[b, :, wc + hd * dh:wc + (hd + 1) * dh]
            v = proj_ref[b, :, 2 * wc + hd * dh:2 * wc + (hd + 1) * dh].astype(BF16)
            q = (q * cos2 + pltpu.roll(q, half, 1) * sin2) * (dh ** -0.5)
            k = k * cos2 + pltpu.roll(k, half, 1) * sin2
            dmask = jnp.where(rel >= 0.0, jnp.exp(jnp.maximum(rel, 0.0) * log_gamma), 0.0)
            ret.append(dict(b=b, hd=hd, v=v, q=q.astype(BF16), k=k.astype(BF16), dmask=dmask,
                            q_in=(q * jnp.exp((rowd + 1.0) * log_gamma)).astype(BF16),
                            k_out=(k * jnp.exp((c - 1.0 - rowd) * log_gamma)).astype(BF16),
                            decay=math.exp(c * log_gamma)))
    for rt in ret:
        rt["scores"] = (_dot_nt(rt["q"], rt["k"]) * rt["dmask"]).astype(BF16)
    for rt in ret:
        st = rstate_ref[rt["b"] * RET_HEADS + rt["hd"]]
        o = _dot(rt["scores"], rt["v"]) + _dot(rt["q_in"], st.astype(BF16))
        rstate_ref[rt["b"] * RET_HEADS + rt["hd"]] = st * rt["decay"] + _dot_tn(rt["k_out"], rt["v"])
        o = o - jnp.mean(o, axis=-1, keepdims=True)
        rt["o"] = o * lax.rsqrt(jnp.mean(o * o, axis=-1, keepdims=True) + HEAD_NORM_EPS)
    for b in range(nb):
        g = proj_ref[b, :, 3 * wc:4 * wc]
        o_all = jnp.concatenate([rt["o"] for rt in ret if rt["b"] == b], axis=-1)
        y_ref[b, :, 0:wc] = (o_all * rnorm_ref[...] * _silu(g)).astype(y_ref.dtype)

    for ch in range(ts // cm):
        rs = slice(ch * cm, (ch + 1) * cm)
        wks = [wk for wk in work if wk["ch"] == ch]
        sidx = [wk["b"] * groups + wk["pr"] for wk in wks]
        sts = [wstate_ref[i] for i in sidx]
        fss = [_dot_nt(wk["l4"], st.astype(BF16)) for wk, st in zip(wks, sts)]
        rhs = [(fs[:n] + _dot(wk["a_ak"], wk["vbd"])).astype(BF16) for wk, fs in zip(wks, fss)]
        us = [_dot(wk["tinv"].astype(BF16), rh) for wk, rh in zip(wks, rhs)]
        uvs = [jnp.concatenate([u.astype(BF16), wk["vbd"]], axis=0) for wk, u in zip(wks, us)]
        for i, st, wk, uv in zip(sidx, sts, wks, uvs):
            wstate_ref[i] = st * wk["p_last"] + _dot_tn(uv, wk["o4"])
        ybds = [fs[n:] + _dot(wk["a_r"], uv) for wk, fs, uv in zip(wks, fss, uvs)]
        for b in range(nb):
            y = jnp.concatenate([yb[:cm] + yb[cm:] for wk, yb in zip(wks, ybds) if wk["b"] == b], axis=-1)
            inv_h = 1.0 / hdim
            y = y - _head_sums(y, seg) * inv_h
            y = y * lax.rsqrt(_head_sums(y * y, seg) * inv_h + RWKV_LN_EPS)
            bonus, g = tails[b]
            y = (y * wnorm_ref[...] + bonus[rs]) * g[rs]
            y_ref[b, ch * cm:(ch + 1) * cm, wc:wc + wd] = y.astype(y_ref.dtype)


def _odd_mixer(h, pos, mnorm, win, freq, rnorm, mu, w0, ww2, a0, aw2, gw2, kk, ka, rk, wnorm, seg, *, nb, ts, wc, wd, rw,
               ra, rg):
    bsz, seq, d = h.shape
    cols = win.shape[1]
    dh = wc // RET_HEADS
    consts = (mnorm, win, freq, rnorm, mu, w0, ww2, a0, aw2, gw2, kk, ka, rk, wnorm, seg)
    kern = functools.partial(_odd_mixer_kernel, nb=nb, ts=ts, wc=wc, wd=wd, rw=rw, ra=ra, rg=rg)
    dcols = 3 * wd + rw + ra + rg
    return pl.pallas_call(
        kern,
        grid=(bsz // nb, seq // ts),
        in_specs=[pl.BlockSpec((nb, ts, d), lambda b, s: (b, s, 0)),
                  pl.BlockSpec((nb, 1, ts), lambda b, s: (b, 0, s))] + [_resident(t) for t in consts],
        out_specs=pl.BlockSpec((nb, ts, wc + wd), lambda b, s: (b, s, 0)),
        out_shape=jax.ShapeDtypeStruct((bsz, seq, wc + wd), BF16),
        scratch_shapes=[pltpu.VMEM((nb, ts, cols), F32), pltpu.VMEM((nb * RET_HEADS, dh, dh), F32),
                        pltpu.VMEM((nb, 8, dcols), F32), pltpu.VMEM((nb * (wd // LANES), LANES, LANES), F32)],
        compiler_params=pltpu.CompilerParams(dimension_semantics=("arbitrary", "arbitrary"),
                                             vmem_limit_bytes=VMEM_LIMIT),
        name="odd_mixer",
    )(h, pos, *consts)


def _post_kernel(h_ref, y_ref, p_ref, wout_ref, fnorm_ref, wg_ref, wu_ref, wd_ref, pnorm_ref, pwg_ref, pbg_ref,
                 pwp_ref, onorm_ref, o_ref, *, final):
    h = h_ref[...] + _dot(y_ref[...], wout_ref[...])
    xn = _rms(h, fnorm_ref[...]).astype(BF16)
    act = (_silu(_dot(xn, wg_ref[...])) * _dot(xn, wu_ref[...])).astype(BF16)
    h = h + _dot(act, wd_ref[...])
    xg = _rms(h, pnorm_ref[...]).astype(BF16)
    gate = jax.nn.sigmoid(_dot(xg, pwg_ref[...]) + pbg_ref[...])
    h = h + gate * _dot(p_ref[...].astype(BF16), pwp_ref[...])
    if final:
        h = _rms(h, onorm_ref[...])
    o_ref[...] = h


def _post(h, y, p, wout, fnorm, wg, wu, wd, pnorm, pwg, pbg, pwp, onorm, *, layer, tm, final):
    m, d = h.shape
    rows = lambda width: pl.BlockSpec((tm, width), lambda i: (i, 0))
    resident = lambda t: pl.BlockSpec(t.shape, lambda i: (0, 0), pipeline_mode=pl.Buffered(1))
    of_layer = lambda t: pl.BlockSpec((None,) + t.shape[1:], lambda i: (layer, 0, 0), pipeline_mode=pl.Buffered(1))
    stacked = (fnorm, wg, wu, wd, pnorm, pwg, pbg, pwp)
    return pl.pallas_call(
        functools.partial(_post_kernel, final=final),
        grid=(m // tm,),
        in_specs=[rows(d), rows(y.shape[1]), pl.BlockSpec((None, tm, p.shape[2]), lambda i: (layer, i, 0)),
                  resident(wout)] + [of_layer(t) for t in stacked] + [resident(onorm)],
        out_specs=rows(d),
        out_shape=jax.ShapeDtypeStruct((m, d), F32),
        compiler_params=pltpu.CompilerParams(dimension_semantics=("arbitrary",), vmem_limit_bytes=VMEM_LIMIT),
        name="post_mixer",
    )(h, y, p, wout, *stacked, onorm)


def _block_diag(blocks):
    g, bi, bj = blocks.shape
    eye = jnp.eye(g, dtype=blocks.dtype)
    return (eye[:, None, :, None] * blocks[:, :, None, :]).reshape(g * bi, g * bj)


def kernel(x, p, positions, ev_w_in, ev_conv_w, ev_conv_b, ev_lru_wr, ev_lru_br, ev_lru_wi, ev_lru_bi, ev_lru_a, ev_gla_wgk, ev_gla_bgk, ev_gla_norm, ev_w_out, od_w_in, od_ret_norm, od_rwkv_mu, od_rwkv_w0, od_rwkv_ww2, od_rwkv_a0, od_rwkv_aw2, od_rwkv_gw2, od_rwkv_kk, od_rwkv_ka, od_rwkv_rk, od_rwkv_norm, od_w_out, mix_norm, ffn_norm, ffn_w_gate, ffn_w_up, ffn_w_down, ple_norm, ple_w_gate, ple_b_gate, ple_w_proj, final_norm):
    bsz, seq, d = x.shape
    depth = p.shape[0]
    m = bsz * seq
    row = lambda t: t.reshape(1, -1).astype(F32)
    rows3 = lambda t: t.reshape(t.shape[0], 1, -1).astype(F32)

    wa = ev_conv_w.shape[-1]
    rank = ev_gla_wgk.shape[1]
    wq = ev_gla_wgk.shape[2]
    wb = ev_gla_norm.shape[-1]
    dk, dv = wq // GLA_HEADS, wb // GLA_HEADS
    wc = od_ret_norm.shape[-1]
    wd = od_rwkv_w0.shape[-1]
    rw, ra, rg = od_rwkv_ww2.shape[1], od_rwkv_aw2.shape[1], od_rwkv_gw2.shape[1]
    nb = MIX_BATCH if bsz % MIX_BATCH == 0 else 1

    h = x.reshape(m, d)
    freq = (ROPE_BASE ** (-jnp.arange(wc // RET_HEADS // 2, dtype=F32) / (wc // RET_HEADS // 2))).reshape(-1, 1)
    pos = positions.astype(F32).reshape(bsz, 1, seq)
    seg = _block_diag(jnp.ones((LANES // RWKV_HEAD_DIM, RWKV_HEAD_DIM, RWKV_HEAD_DIM), BF16))
    assert 2 * RWKV_HEAD_DIM == LANES and wd % LANES == 0

    for i in range(depth):
        j = i // 2
        if i % 2 == 0:
            w_in = ev_w_in[j]
            cols = w_in.shape[1]
            cols_pad = -(-cols // LANES) * LANES
            w_in = jnp.pad(w_in, ((0, 0), (0, cols_pad - cols))).astype(BF16)
            wgate = jnp.concatenate([_block_diag(ev_lru_wr[j]), _block_diag(ev_lru_wi[j])], axis=1).astype(BF16)
            bgate = jnp.concatenate([ev_lru_br[j], ev_lru_bi[j]]).reshape(1, -1)
            y = _even_mixer(h.reshape(bsz, seq, d), row(mix_norm[i]), w_in, ev_conv_w[j], row(ev_conv_b[j]), wgate, bgate,
                            row(ev_lru_a[j]), ev_gla_wgk[j], row(ev_gla_bgk[j]), row(ev_gla_norm[j]),
                            nb=nb, ts=4 * MIX_CHUNK, wa=wa, dk=dk, dv=dv, rank=rank)
            w_out = ev_w_out[j]
        else:
            y = _odd_mixer(h.reshape(bsz, seq, d), pos, row(mix_norm[i]), od_w_in[j].astype(BF16), freq,
                           row(od_ret_norm[j]), row(od_rwkv_mu[j]),
                           row(od_rwkv_w0[j]), od_rwkv_ww2[j].astype(BF16), row(od_rwkv_a0[j]),
                           od_rwkv_aw2[j].astype(BF16), od_rwkv_gw2[j].astype(BF16), row(od_rwkv_kk[j]),
                           row(od_rwkv_ka[j]), row(od_rwkv_rk[j]), row(od_rwkv_norm[j]), seg,
                           nb=nb, ts=RET_CHUNK, wc=wc, wd=wd, rw=rw, ra=ra, rg=rg)
            w_out = od_w_out[j]
        h = _post(h, y.reshape(m, -1), p.reshape(depth, m, -1), w_out.astype(BF16), rows3(ffn_norm),
                  ffn_w_gate.astype(BF16), ffn_w_up.astype(BF16), ffn_w_down.astype(BF16),
                  rows3(ple_norm), ple_w_gate.astype(BF16), rows3(ple_b_gate), ple_w_proj.astype(BF16),
                  row(final_norm), layer=i, tm=512, final=(i == depth - 1))
    return h.reshape(bsz, seq, d)
```

```python
import functools
import math

import jax
import jax.numpy as jnp
from jax import lax
from jax.experimental import pallas as pl
from jax.experimental.pallas import tpu as pltpu

F32 = jnp.float32
BF16 = jnp.bfloat16
HIGHEST = lax.Precision.HIGHEST

NORM_EPS = 1e-6
LANES = 128
SUBLANES = 8
LRU_BLOCKS = 8
CONV_W = 4
LRU_C = 8.0
GLA_HEADS = 4
GLA_GATE_NORM = 16.0
GLA_SUB = 16
RET_HEADS = 4
RET_CHUNK = 128
ROPE_BASE = 10000.0
RWKV_HEAD_DIM = 64
RWKV_LN_EPS = 64e-5
HEAD_NORM_EPS = 1e-5

MIX_CHUNK = 64
MIX_BATCH = 4
VMEM_LIMIT = 56 * 1024 * 1024


def _log_sigmoid(x):
    return jnp.minimum(x, 0.0) - jnp.log1p(jnp.exp(-jnp.abs(x)))


def _softplus(x):
    return jnp.maximum(x, 0.0) + jnp.log1p(jnp.exp(-jnp.abs(x)))


def _silu(x):
    return x * jax.nn.sigmoid(x)


def _gelu_tanh(x):
    return 0.5 * x * (1.0 + jnp.tanh(math.sqrt(2.0 / math.pi) * (x + 0.044715 * (x * x * x))))


def _rms(x, g):
    return x * lax.rsqrt(jnp.mean(x * x, axis=-1, keepdims=True) + NORM_EPS) * g


def _dot(a, b, precision=None):
    return jnp.dot(a, b, preferred_element_type=F32, precision=precision)


def _dot_nt(a, b, precision=None):
    return lax.dot_general(a, b, (((1,), (1,)), ((), ())), preferred_element_type=F32, precision=precision)


def _dot_tn(a, b, precision=None):
    return lax.dot_general(a, b, (((0,), (0,)), ((), ())), preferred_element_type=F32, precision=precision)


def _head_sums(x, seg):
    return jnp.concatenate([_dot(x[:, g:g + LANES].astype(BF16), seg) for g in range(0, x.shape[1], LANES)], axis=-1)


def _dot_split_rhs(m, x, passes):
    acc = None
    for _ in range(passes):
        piece = x.astype(BF16)
        acc = _dot(m, piece) if acc is None else acc + _dot(m, piece)
        x = x - piece.astype(F32)
    return acc


def _tri(n, strict=False):
    row = lax.broadcasted_iota(jnp.int32, (n, n), 0)
    col = lax.broadcasted_iota(jnp.int32, (n, n), 1)
    return (row > col) if strict else (row >= col)


PROJ_GROUP = 512


def _project_steps(h_ref, mnorm_ref, win_ref, proj_ref):
    xn = _rms(h_ref[...], mnorm_ref[...]).astype(BF16)
    cols = proj_ref.shape[1]
    for c0 in range(0, cols, PROJ_GROUP):
        c1 = min(c0 + PROJ_GROUP, cols)
        proj_ref[:, c0:c1] = _dot(xn, win_ref[:, c0:c1])
        yield


def _interleave(*gens):
    gens = list(gens)
    while gens:
        for g in list(gens):
            try:
                next(g)
            except StopIteration:
                gens.remove(g)


def _resident(t):
    return pl.BlockSpec(t.shape, lambda b, s: (0,) * t.ndim, pipeline_mode=pl.Buffered(1))


def _rg_lru_steps(proj_ref, y_ref, xtail_ref, hcar_ref, convw_ref, convb_ref, wgate_ref, bgate_ref, lrua_ref, *, ts, wa):
    ax = proj_ref[:, 0:wa]
    xcat = jnp.concatenate([xtail_ref[...], ax], axis=0)
    xtail_ref[...] = ax[ts - 8:ts]
    cw = convw_ref[...]
    xc = convb_ref[...] + cw[CONV_W - 1:CONV_W] * ax
    for s in range(1, CONV_W):
        xc = xc + cw[CONV_W - 1 - s:CONV_W - s] * xcat[8 - s:8 - s + ts]
    yield
    gates = _dot(xc.astype(BF16), wgate_ref[...]) + bgate_ref[...]
    r = jax.nn.sigmoid(gates[:, :wa])
    i = jax.nn.sigmoid(gates[:, wa:])
    yield
    log_a = LRU_C * r * _log_sigmoid(lrua_ref[...])
    a = jnp.exp(log_a)
    t = jnp.tanh(log_a)
    u = jnp.sqrt(-2.0 * t / (1.0 - t)) * (i * xc)
    yield
    groups = ts // SUBLANES
    u = u.reshape(groups, SUBLANES, wa)
    a = a.reshape(groups, SUBLANES, wa)
    row = lax.broadcasted_iota(jnp.int32, (groups, SUBLANES, wa), 1)
    d = 1
    while d < SUBLANES:
        keep = row >= d
        u = jnp.where(keep, a * pltpu.roll(u, d, 1) + u, u)
        a = jnp.where(keep, a * pltpu.roll(a, d, 1), a)
        d *= 2
        yield
    carry = hcar_ref[...]
    pieces = []
    for j in range(groups):
        hj = u[j] + a[j] * carry
        carry = hj[SUBLANES - 1:SUBLANES]
        pieces.append(hj)
    hcar_ref[...] = carry
    h = jnp.concatenate(pieces, axis=0)
    yield
    y_ref[:, 0:wa] = (h * _gelu_tanh(proj_ref[:, wa:2 * wa])).astype(y_ref.dtype)
    yield


def _even_mixer_kernel(h_ref, mnorm_ref, win_ref, convw_ref, convb_ref, wgate_ref, bgate_ref, lrua_ref,
                       wgk_ref, bgk_ref, gnorm_ref, y_ref, proj_ref, xtail_ref, hcar_ref, st_ref,
                       *, nb, ts, wa, dk, dv, rank):
    heads = GLA_HEADS
    wq = heads * dk
    wb = heads * dv
    o_q = 2 * wa
    o_k, o_v = o_q + wq, o_q + 2 * wq
    o_g, o_gk = o_v + wb, o_v + 2 * wb
    step = pl.program_id(1)

    @pl.when(step == 0)
    def _():
        xtail_ref[...] = jnp.zeros_like(xtail_ref)
        hcar_ref[...] = jnp.zeros_like(hcar_ref)
        st_ref[...] = jnp.zeros_like(st_ref)

    project = [_project_steps(h_ref.at[b], mnorm_ref, win_ref, proj_ref.at[b]) for b in range(nb)]
    rg_lru = [_rg_lru_steps(proj_ref.at[b], y_ref.at[b], xtail_ref.at[b], hcar_ref.at[b], convw_ref, convb_ref,
                            wgate_ref, bgate_ref, lrua_ref, ts=ts, wa=wa) for b in range(nb)]
    _interleave(project[0])
    for b in range(nb):
        _interleave(rg_lru[b], *project[b + 1:b + 2])

    cg = MIX_CHUNK
    tri_incl = jnp.where(_tri(cg), 1.0, 0.0).astype(BF16)
    causal = _tri(cg)
    rowc = lax.broadcasted_iota(jnp.int32, (cg, wq), 0)
    items = []
    for b in range(nb):
        z = _dot(proj_ref[b, :, o_gk:o_gk + rank], wgk_ref[...], precision=HIGHEST) + bgk_ref[...]
        log_f = _log_sigmoid(z) * (1.0 / GLA_GATE_NORM)
        for c in range(ts // cg):
            r0 = c * cg
            cum = _dot_split_rhs(tri_incl, log_f[r0:r0 + cg], 3)
            q = proj_ref[b, r0:r0 + cg, o_q:o_q + wq] * (dk ** -0.5)
            k = proj_ref[b, r0:r0 + cg, o_k:o_k + wq]
            v = proj_ref[b, r0:r0 + cg, o_v:o_v + wb].astype(BF16)
            cum_last = cum[cg - 1:cg]
            it = dict(b=b, c=c, v=v, q_in=(q * jnp.exp(cum)).astype(BF16),
                      k_out=(k * jnp.exp(cum_last - cum)).astype(BF16), chunk_decay=jnp.exp(cum_last), qb=[], kb=[])
            for ib in range(cg // GLA_SUB):
                b0 = ib * GLA_SUB
                base = cum[b0 - 1:b0] if ib > 0 else jnp.zeros((1, wq), F32)
                it["qb"].append((q[b0:b0 + GLA_SUB] * jnp.exp(cum[b0:b0 + GLA_SUB] - base)).astype(BF16))
                it["kb"].append(jnp.where(rowc < b0 + GLA_SUB, k * jnp.exp(base - cum), 0.0).astype(BF16))
            items.append(it)
    for it in items:
        it["scores"] = [
            jnp.where(causal, jnp.concatenate(
                [_dot_nt(qb[:, hd * dk:(hd + 1) * dk], kb[:, hd * dk:(hd + 1) * dk])
                 for qb, kb in zip(it["qb"], it["kb"])], axis=0), 0.0).astype(BF16)
            for hd in range(heads)]
    for it in items:
        it["intra"] = [_dot(it["scores"][hd], it["v"][:, hd * dv:(hd + 1) * dv]) for hd in range(heads)]
    for c in range(ts // cg):
        r0 = c * cg
        for it in [t for t in items if t["c"] == c]:
            b = it["b"]
            outs = []
            for hd in range(heads):
                ks = slice(hd * dk, (hd + 1) * dk)
                vs = slice(hd * dv, (hd + 1) * dv)
                st = st_ref[b * heads + hd]
                o = it["intra"][hd] + _dot_nt(it["q_in"][:, ks], st.astype(BF16))
                st_ref[b * heads + hd] = st * it["chunk_decay"][:, ks] + _dot_tn(it["v"][:, vs], it["k_out"][:, ks])
                outs.append(o * lax.rsqrt(jnp.mean(o * o, axis=-1, keepdims=True) + HEAD_NORM_EPS))
            o_all = jnp.concatenate(outs, axis=-1) * gnorm_ref[...]
            g = proj_ref[b, r0:r0 + cg, o_g:o_g + wb]
            y_ref[b, r0:r0 + cg, wa:wa + wb] = (o_all * _silu(g)).astype(y_ref.dtype)


def _even_mixer(h, mnorm, win, convw, convb, wgate, bgate, lrua, wgk, bgk, gnorm, *, nb, ts, wa, dk, dv, rank):
    bsz, seq, d = h.shape
    cols = win.shape[1]
    heads = GLA_HEADS
    wb = heads * dv
    consts = (mnorm, win, convw, convb, wgate, bgate, lrua, wgk, bgk, gnorm)
    kern = functools.partial(_even_mixer_kernel, nb=nb, ts=ts, wa=wa, dk=dk, dv=dv, rank=rank)
    return pl.pallas_call(
        kern,
        grid=(bsz // nb, seq // ts),
        in_specs=[pl.BlockSpec((nb, ts, d), lambda b, s: (b, s, 0))] + [_resident(t) for t in consts],
        out_specs=pl.BlockSpec((nb, ts, wa + wb), lambda b, s: (b, s, 0)),
        out_shape=jax.ShapeDtypeStruct((bsz, seq, wa + wb), BF16),
        scratch_shapes=[pltpu.VMEM((nb, ts, cols), F32), pltpu.VMEM((nb, 8, wa), F32),
                        pltpu.VMEM((nb, 1, wa), F32), pltpu.VMEM((nb * heads, dv, dk), F32)],
        compiler_params=pltpu.CompilerParams(dimension_semantics=("arbitrary", "arbitrary"),
                                             vmem_limit_bytes=VMEM_LIMIT),
        name="even_mixer",
    )(h, *consts)


def _odd_mixer_kernel(h_ref, pos_ref, mnorm_ref, win_ref, freq_ref, rnorm_ref, mu_ref, w0_ref, ww2_ref, a0_ref,
                      aw2_ref, gw2_ref, kk_ref, ka_ref, rk_ref, wnorm_ref, seg_ref,
                      y_ref, proj_ref, rstate_ref, dprev_ref, wstate_ref, *, nb, ts, wc, wd, rw, ra, rg):
    f32 = F32

    @pl.when(pl.program_id(1) == 0)
    def _():
        rstate_ref[...] = jnp.zeros_like(rstate_ref)
        dprev_ref[...] = jnp.zeros_like(dprev_ref)
        wstate_ref[...] = jnp.zeros_like(wstate_ref)

    for b in range(nb):
        _interleave(_project_steps(h_ref.at[b], mnorm_ref, win_ref, proj_ref.at[b]))

    o_d = 4 * wc
    dcols = 3 * wd + rw + ra + rg
    hdim = RWKV_HEAD_DIM
    cm = MIX_CHUNK
    n = 2 * cm
    groups = wd // LANES
    seg = seg_ref[...]
    lane = lax.broadcasted_iota(jnp.int32, (cm, LANES), 1)
    first_head = lane < hdim

    def stack_heads(t):
        return jnp.concatenate([jnp.where(first_head, t, 0.0), jnp.where(first_head, 0.0, t)], axis=0)

    rin = lax.broadcasted_iota(jnp.int32, (n, n), 0)
    cin = lax.broadcasted_iota(jnp.int32, (n, n), 1)
    strict = (rin & (cm - 1)) > (cin & (cm - 1))
    incl = (rin & (cm - 1)) >= (cin & (cm - 1))
    eye = jnp.where(rin == cin, 1.0, 0.0).astype(f32)
    tri_incl = jnp.where(_tri(cm), 1.0, 0.0).astype(BF16)

    work = []
    tails = {}

    def prepare_steps(b):
        dpart = proj_ref[b, :, o_d:o_d + dcols]
        dshift = jnp.concatenate([dprev_ref[b], dpart], axis=0)[7:7 + ts]
        dprev_ref[b] = dpart[ts - 8:ts]
        dpart = dpart + mu_ref[...] * (dshift - dpart)
        yield
        r = dpart[:, 0:wd]
        k = dpart[:, wd:2 * wd]
        v = dpart[:, 2 * wd:3 * wd]
        w_lr = dpart[:, 3 * wd:3 * wd + rw]
        a_lr = dpart[:, 3 * wd + rw:3 * wd + rw + ra]
        g_lr = dpart[:, 3 * wd + rw + ra:dcols]
        w = -_softplus(-(w0_ref[...] + _dot(jnp.tanh(w_lr).astype(BF16), ww2_ref[...]))) - 0.5
        log_decay = -jnp.exp(w)
        a = jax.nn.sigmoid(a0_ref[...] + _dot(a_lr.astype(BF16), aw2_ref[...]))
        g = _dot(jax.nn.sigmoid(g_lr).astype(BF16), gw2_ref[...])
        yield
        kk = k * kk_ref[...]
        kk = kk * lax.rsqrt(_head_sums(kk * kk, seg) + 1e-12)
        k = k * (1.0 + (a - 1.0) * ka_ref[...])
        tails[b] = (_head_sums(r * k * rk_ref[...], seg) * v, g)
        kb = kk * a
        yield
        for ch in range(ts // cm):
            rs = slice(ch * cm, (ch + 1) * cm)
            ld = log_decay[rs]
            cum = _dot_split_rhs(tri_incl, ld, 3)
            cum_last = cum[cm - 1:cm]
            p_incl = jnp.exp(cum)
            p_inv = jnp.exp(-cum)
            a_t = -kk[rs] * jnp.exp(cum - ld)
            r_t = r[rs] * p_incl
            b_t = kb[rs] * p_inv
            k_t = k[rs] * p_inv
            p_out = jnp.exp(cum_last - cum)
            b_o = kb[rs] * p_out
            k_o = k[rs] * p_out
            p_last = jnp.exp(cum_last)
            yield
            for pr in range(groups):
                ls = slice(pr * LANES, (pr + 1) * LANES)
                l4 = jnp.concatenate([stack_heads(a_t[:, ls]), stack_heads(r_t[:, ls])], axis=0).astype(BF16)
                r4 = jnp.concatenate([stack_heads(b_t[:, ls]), stack_heads(k_t[:, ls])], axis=0).astype(BF16)
                o4 = jnp.concatenate([stack_heads(b_o[:, ls]), stack_heads(k_o[:, ls])], axis=0).astype(BF16)
                vbd = stack_heads(v[rs, ls]).astype(BF16)
                work.append(dict(b=b, ch=ch, pr=pr, l4=l4, r4=r4, o4=o4, vbd=vbd, p_last=p_last[:, ls]))
                yield

    for b in range(nb):
        _interleave(prepare_steps(b))
    for wk in work:
        big = _dot_nt(wk["l4"], wk["r4"])
        a_ab = jnp.where(strict, big[:n, :n], 0.0)
        wk["a_ak"] = jnp.where(strict, big[:n, n:], 0.0).astype(BF16)
        wk["a_r"] = jnp.concatenate([jnp.where(incl, big[n:, :n], 0.0), jnp.where(incl, big[n:, n:], 0.0)],
                                    axis=1).astype(BF16)
        wk["x"] = a_ab.astype(BF16)
        wk["tinv"] = eye + a_ab
    span = 2
    while span < cm:
        for wk in work:
            wk["x"] = _dot(wk["x"], wk["x"]).astype(BF16)
        for wk in work:
            wk["tinv"] = wk["tinv"] + _dot(wk["tinv"].astype(BF16), wk["x"])
        span *= 2

    c = ts
    dh = wc // RET_HEADS
    half = dh // 2
    rowi = lax.broadcasted_iota(jnp.int32, (c, c), 0)
    coli = lax.broadcasted_iota(jnp.int32, (c, c), 1)
    rel = (rowi - coli).astype(f32)
    rowd = lax.broadcasted_iota(jnp.int32, (c, dh), 0).astype(f32)
    ret = []
    for b in range(nb):
        ang = freq_ref[...] * pos_ref[b]
        cos, sin = jnp.cos(ang).T, jnp.sin(ang).T
        cos2 = jnp.concatenate([cos, cos], axis=-1)
        sin2 = jnp.concatenate([-sin, sin], axis=-1)
        for hd in range(RET_HEADS):
            log_gamma = math.log1p(-2.0 ** (-5.0 - hd))
            q = proj_ref[b, :, hd * dh:(hd + 1) * dh]
            k = proj_ref[b, :, wc + hd * dh:wc + (hd + 1) * dh]
            v = proj_ref[b, :, 2 * wc + hd * dh:2 * wc + (hd + 1) * dh].astype(BF16)
            q = (q * cos2 + pltpu.roll(q, half, 1) * sin2) * (dh ** -0.5)
            k = k * cos2 + pltpu.roll(k, half, 1) * sin2
            dmask = jnp.where(rel >= 0.0, jnp.exp(jnp.maximum(rel, 0.0) * log_gamma), 0.0)
            ret.append(dict(b=b, hd=hd, v=v, q=q.astype(BF16), k=k.astype(BF16), dmask=dmask,
                            q_in=(q * jnp.exp((rowd + 1.0) * log_gamma)).astype(BF16),
                            k_out=(k * jnp.exp((c - 1.0 - rowd) * log_gamma)).astype(BF16),
                            decay=math.exp(c * log_gamma)))
    for rt in ret:
        rt["scores"] = (_dot_nt(rt["q"], rt["k"]) * rt["dmask"]).astype(BF16)
    for rt in ret:
        st = rstate_ref[rt["b"] * RET_HEADS + rt["hd"]]
        o = _dot(rt["scores"], rt["v"]) + _dot(rt["q_in"], st.astype(BF16))
        rstate_ref[rt["b"] * RET_HEADS + rt["hd"]] = st * rt["decay"] + _dot_tn(rt["k_out"], rt["v"])
        o = o - jnp.mean(o, axis=-1, keepdims=True)
        rt["o"] = o * lax.rsqrt(jnp.mean(o * o, axis=-1, keepdims=True) + HEAD_NORM_EPS)
    for b in range(nb):
        g = proj_ref[b, :, 3 * wc:4 * wc]
        o_all = jnp.concatenate([rt["o"] for rt in ret if rt["b"] == b], axis=-1)
        y_ref[b, :, 0:wc] = (o_all * rnorm_ref[...] * _silu(g)).astype(y_ref.dtype)

    for ch in range(ts // cm):
        rs = slice(ch * cm, (ch + 1) * cm)
        wks = [wk for wk in work if wk["ch"] == ch]
        sidx = [wk["b"] * groups + wk["pr"] for wk in wks]
        sts = [wstate_ref[i] for i in sidx]
        fss = [_dot_nt(wk["l4"], st.astype(BF16)) for wk, st in zip(wks, sts)]
        rhs = [(fs[:n] + _dot(wk["a_ak"], wk["vbd"])).astype(BF16) for wk, fs in zip(wks, fss)]
        us = [_dot(wk["tinv"].astype(BF16), rh) for wk, rh in zip(wks, rhs)]
        uvs = [jnp.concatenate([u.astype(BF16), wk["vbd"]], axis=0) for wk, u in zip(wks, us)]
        for i, st, wk, uv in zip(sidx, sts, wks, uvs):
            wstate_ref[i] = st * wk["p_last"] + _dot_tn(uv, wk["o4"])
        ybds = [fs[n:] + _dot(wk["a_r"], uv) for wk, fs, uv in zip(wks, fss, uvs)]
        for b in range(nb):
            y = jnp.concatenate([yb[:cm] + yb[cm:] for wk, yb in zip(wks, ybds) if wk["b"] == b], axis=-1)
            inv_h = 1.0 / hdim
            y = y - _head_sums(y, seg) * inv_h
            y = y * lax.rsqrt(_head_sums(y * y, seg) * inv_h + RWKV_LN_EPS)
            bonus, g = tails[b]
            y = (y * wnorm_ref[...] + bonus[rs]) * g[rs]
            y_ref[b, ch * cm:(ch + 1) * cm, wc:wc + wd] = y.astype(y_ref.dtype)


def _odd_mixer(h, pos, mnorm, win, freq, rnorm, mu, w0, ww2, a0, aw2, gw2, kk, ka, rk, wnorm, seg, *, nb, ts, wc, wd, rw,
               ra, rg):
    bsz, seq, d = h.shape
    cols = win.shape[1]
    dh = wc // RET_HEADS
    consts = (mnorm, win, freq, rnorm, mu, w0, ww2, a0, aw2, gw2, kk, ka, rk, wnorm, seg)
    kern = functools.partial(_odd_mixer_kernel, nb=nb, ts=ts, wc=wc, wd=wd, rw=rw, ra=ra, rg=rg)
    dcols = 3 * wd + rw + ra + rg
    return pl.pallas_call(
        kern,
        grid=(bsz // nb, seq // ts),
        in_specs=[pl.BlockSpec((nb, ts, d), lambda b, s: (b, s, 0)),
                  pl.BlockSpec((nb, 1, ts), lambda b, s: (b, 0, s))] + [_resident(t) for t in consts],
        out_specs=pl.BlockSpec((nb, ts, wc + wd), lambda b, s: (b, s, 0)),
        out_shape=jax.ShapeDtypeStruct((bsz, seq, wc + wd), BF16),
        scratch_shapes=[pltpu.VMEM((nb, ts, cols), F32), pltpu.VMEM((nb * RET_HEADS, dh, dh), F32),
                        pltpu.VMEM((nb, 8, dcols), F32), pltpu.VMEM((nb * (wd // LANES), LANES, LANES), F32)],
        compiler_params=pltpu.CompilerParams(dimension_semantics=("arbitrary", "arbitrary"),
                                             vmem_limit_bytes=VMEM_LIMIT),
        name="odd_mixer",
    )(h, pos, *consts)


def _post_kernel(h_ref, y_ref, p_ref, wout_ref, fnorm_ref, wg_ref, wu_ref, wd_ref, pnorm_ref, pwg_ref, pbg_ref,
                 pwp_ref, onorm_ref, o_ref, *, final):
    h = h_ref[...] + _dot(y_ref[...], wout_ref[...])
    xn = _rms(h, fnorm_ref[...]).astype(BF16)
    act = (_silu(_dot(xn, wg_ref[...])) * _dot(xn, wu_ref[...])).astype(BF16)
    h = h + _dot(act, wd_ref[...])
    xg = _rms(h, pnorm_ref[...]).astype(BF16)
    gate = jax.nn.sigmoid(_dot(xg, pwg_ref[...]) + pbg_ref[...])
    h = h + gate * _dot(p_ref[...].astype(BF16), pwp_ref[...])
    if final:
        h = _rms(h, onorm_ref[...])
    o_ref[...] = h


def _post(h, y, p, wout, fnorm, wg, wu, wd, pnorm, pwg, pbg, pwp, onorm, *, layer, tm, final):
    m, d = h.shape
    rows = lambda width: pl.BlockSpec((tm, width), lambda i: (i, 0))
    resident = lambda t: pl.BlockSpec(t.shape, lambda i: (0, 0), pipeline_mode=pl.Buffered(1))
    of_layer = lambda t: pl.BlockSpec((None,) + t.shape[1:], lambda i: (layer, 0, 0), pipeline_mode=pl.Buffered(1))
    stacked = (fnorm, wg, wu, wd, pnorm, pwg, pbg, pwp)
    return pl.pallas_call(
        functools.partial(_post_kernel, final=final),
        grid=(m // tm,),
        in_specs=[rows(d), rows(y.shape[1]), pl.BlockSpec((None, tm, p.shape[2]), lambda i: (layer, i, 0)),
                  resident(wout)] + [of_layer(t) for t in stacked] + [resident(onorm)],
        out_specs=rows(d),
        out_shape=jax.ShapeDtypeStruct((m, d), F32),
        compiler_params=pltpu.CompilerParams(dimension_semantics=("arbitrary",), vmem_limit_bytes=VMEM_LIMIT),
        name="post_mixer",
    )(h, y, p, wout, *stacked, onorm)


def _block_diag(blocks):
    g, bi, bj = blocks.shape
    eye = jnp.eye(g, dtype=blocks.dtype)
    return (eye[:, None, :, None] * blocks[:, :, None, :]).reshape(g * bi, g * bj)


def kernel(x, p, positions, ev_w_in, ev_conv_w, ev_conv_b, ev_lru_wr, ev_lru_br, ev_lru_wi, ev_lru_bi, ev_lru_a, ev_gla_wgk, ev_gla_bgk, ev_gla_norm, ev_w_out, od_w_in, od_ret_norm, od_rwkv_mu, od_rwkv_w0, od_rwkv_ww2, od_rwkv_a0, od_rwkv_aw2, od_rwkv_gw2, od_rwkv_kk, od_rwkv_ka, od_rwkv_rk, od_rwkv_norm, od_w_out, mix_norm, ffn_norm, ffn_w_gate, ffn_w_up, ffn_w_down, ple_norm, ple_w_gate, ple_b_gate, ple_w_proj, final_norm):
    bsz, seq, d = x.shape
    depth = p.shape[0]
    m = bsz * seq
    row = lambda t: t.reshape(1, -1).astype(F32)
    rows3 = lambda t: t.reshape(t.shape[0], 1, -1).astype(F32)

    wa = ev_conv_w.shape[-1]
    rank = ev_gla_wgk.shape[1]
    wq = ev_gla_wgk.shape[2]
    wb = ev_gla_norm.shape[-1]
    dk, dv = wq // GLA_HEADS, wb // GLA_HEADS
    wc = od_ret_norm.shape[-1]
    wd = od_rwkv_w0.shape[-1]
    rw, ra, rg = od_rwkv_ww2.shape[1], od_rwkv_aw2.shape[1], od_rwkv_gw2.shape[1]
    nb = MIX_BATCH if bsz % MIX_BATCH == 0 else 1

    h = x.reshape(m, d)
    freq = (ROPE_BASE ** (-jnp.arange(wc // RET_HEADS // 2, dtype=F32) / (wc // RET_HEADS // 2))).reshape(-1, 1)
    pos = positions.astype(F32).reshape(bsz, 1, seq)
    seg = _block_diag(jnp.ones((LANES // RWKV_HEAD_DIM, RWKV_HEAD_DIM, RWKV_HEAD_DIM), BF16))
    assert 2 * RWKV_HEAD_DIM == LANES and wd % LANES == 0

    for i in range(depth):
        j = i // 2
        if i % 2 == 0:
            w_in = ev_w_in[j]
            cols = w_in.shape[1]
            cols_pad = -(-cols // LANES) * LANES
            w_in = jnp.pad(w_in, ((0, 0), (0, cols_pad - cols))).astype(BF16)
            wgate = jnp.concatenate([_block_diag(ev_lru_wr[j]), _block_diag(ev_lru_wi[j])], axis=1).astype(BF16)
            bgate = jnp.concatenate([ev_lru_br[j], ev_lru_bi[j]]).reshape(1, -1)
            y = _even_mixer(h.reshape(bsz, seq, d), row(mix_norm[i]), w_in, ev_conv_w[j], row(ev_conv_b[j]), wgate, bgate,
                            row(ev_lru_a[j]), ev_gla_wgk[j], row(ev_gla_bgk[j]), row(ev_gla_norm[j]),
                            nb=nb, ts=4 * MIX_CHUNK, wa=wa, dk=dk, dv=dv, rank=rank)
            w_out = ev_w_out[j]
        else:
            y = _odd_mixer(h.reshape(bsz, seq, d), pos, row(mix_norm[i]), od_w_in[j].astype(BF16), freq,
                           row(od_ret_norm[j]), row(od_rwkv_mu[j]),
                           row(od_rwkv_w0[j]), od_rwkv_ww2[j].astype(BF16), row(od_rwkv_a0[j]),
                           od_rwkv_aw2[j].astype(BF16), od_rwkv_gw2[j].astype(BF16), row(od_rwkv_kk[j]),
                           row(od_rwkv_ka[j]), row(od_rwkv_rk[j]), row(od_rwkv_norm[j]), seg,
                           nb=nb, ts=RET_CHUNK, wc=wc, wd=wd, rw=rw, ra=ra, rg=rg)
            w_out = od_w_out[j]
        h = _post(h, y.reshape(m, -1), p.reshape(depth, m, -1), w_out.astype(BF16), rows3(ffn_norm),
                  ffn_w_gate.astype(BF16), ffn_w_up.astype(BF16), ffn_w_down.astype(BF16),
                  rows3(ple_norm), ple_w_gate.astype(BF16), rows3(ple_b_gate), ple_w_proj.astype(BF16),
                  row(final_norm), layer=i, tm=512, final=(i == depth - 1))
    return h.reshape(bsz, seq, d)
```

```python
import functools
import math

import jax
import jax.numpy as jnp
from jax import lax
from jax.experimental import pallas as pl
from jax.experimental.pallas import tpu as pltpu

F32 = jnp.float32
BF16 = jnp.bfloat16
HIGHEST = lax.Precision.HIGHEST

NORM_EPS = 1e-6
LANES = 128
SUBLANES = 8
LRU_BLOCKS = 8
CONV_W = 4
LRU_C = 8.0
GLA_HEADS = 4
GLA_GATE_NORM = 16.0
GLA_SUB = 16
RET_HEADS = 4
RET_CHUNK = 128
ROPE_BASE = 10000.0
RWKV_HEAD_DIM = 64
RWKV_LN_EPS = 64e-5
HEAD_NORM_EPS = 1e-5

MIX_CHUNK = 64
MIX_BATCH = 4
VMEM_LIMIT = 56 * 1024 * 1024


def _log_sigmoid(x):
    return jnp.minimum(x, 0.0) - jnp.log1p(jnp.exp(-jnp.abs(x)))


def _softplus(x):
    return jnp.maximum(x, 0.0) + jnp.log1p(jnp.exp(-jnp.abs(x)))


def _silu(x):
    return x * jax.nn.sigmoid(x)


def _gelu_tanh(x):
    return 0.5 * x * (1.0 + jnp.tanh(math.sqrt(2.0 / math.pi) * (x + 0.044715 * (x * x * x))))


def _rms(x, g):
    return x * lax.rsqrt(jnp.mean(x * x, axis=-1, keepdims=True) + NORM_EPS) * g


def _dot(a, b, precision=None):
    return jnp.dot(a, b, preferred_element_type=F32, precision=precision)


def _dot_nt(a, b, precision=None):
    return lax.dot_general(a, b, (((1,), (1,)), ((), ())), preferred_element_type=F32, precision=precision)


def _dot_tn(a, b, precision=None):
    return lax.dot_general(a, b, (((0,), (0,)), ((), ())), preferred_element_type=F32, precision=precision)


def _head_sums(x, seg):
    return jnp.concatenate([_dot(x[:, g:g + LANES].astype(BF16), seg) for g in range(0, x.shape[1], LANES)], axis=-1)


def _dot_split_rhs(m, x, passes):
    acc = None
    for _ in range(passes):
        piece = x.astype(BF16)
        acc = _dot(m, piece) if acc is None else acc + _dot(m, piece)
        x = x - piece.astype(F32)
    return acc


def _tri(n, strict=False):
    row = lax.broadcasted_iota(jnp.int32, (n, n), 0)
    col = lax.broadcasted_iota(jnp.int32, (n, n), 1)
    return (row > col) if strict else (row >= col)


PROJ_GROUP = 512


def _project_steps(h_ref, mnorm_ref, win_ref, proj_ref):
    nb, ts, d = h_ref.shape
    xn = _rms(h_ref[...].reshape(nb * ts, d), mnorm_ref[...]).astype(BF16)
    cols = proj_ref.shape[2]
    for c0 in range(0, cols, PROJ_GROUP):
        c1 = min(c0 + PROJ_GROUP, cols)
        proj_ref[:, :, c0:c1] = _dot(xn, win_ref[:, c0:c1]).reshape(nb, ts, c1 - c0)
        yield


def _interleave(*gens):
    gens = list(gens)
    while gens:
        for g in list(gens):
            try:
                next(g)
            except StopIteration:
                gens.remove(g)


def _resident(t):
    return pl.BlockSpec(t.shape, lambda b, s: (0,) * t.ndim, pipeline_mode=pl.Buffered(1))


def _rg_lru_steps(proj_ref, y_ref, xtail_ref, hcar_ref, convw_ref, convb_ref, wgate_ref, bgate_ref, lrua_ref, *, ts, wa):
    ax = proj_ref[:, 0:wa]
    xcat = jnp.concatenate([xtail_ref[...], ax], axis=0)
    xtail_ref[...] = ax[ts - 8:ts]
    cw = convw_ref[...]
    xc = convb_ref[...] + cw[CONV_W - 1:CONV_W] * ax
    for s in range(1, CONV_W):
        xc = xc + cw[CONV_W - 1 - s:CONV_W - s] * xcat[8 - s:8 - s + ts]
    yield
    gates = _dot(xc.astype(BF16), wgate_ref[...]) + bgate_ref[...]
    r = jax.nn.sigmoid(gates[:, :wa])
    i = jax.nn.sigmoid(gates[:, wa:])
    yield
    log_a = LRU_C * r * _log_sigmoid(lrua_ref[...])
    a = jnp.exp(log_a)
    t = jnp.tanh(log_a)
    u = jnp.sqrt(-2.0 * t / (1.0 - t)) * (i * xc)
    yield
    groups = ts // SUBLANES
    u = u.reshape(groups, SUBLANES, wa)
    a = a.reshape(groups, SUBLANES, wa)
    row = lax.broadcasted_iota(jnp.int32, (groups, SUBLANES, wa), 1)
    d = 1
    while d < SUBLANES:
        keep = row >= d
        u = jnp.where(keep, a * pltpu.roll(u, d, 1) + u, u)
        a = jnp.where(keep, a * pltpu.roll(a, d, 1), a)
        d *= 2
        yield
    carry = hcar_ref[...]
    pieces = []
    for j in range(groups):
        hj = u[j] + a[j] * carry
        carry = hj[SUBLANES - 1:SUBLANES]
        pieces.append(hj)
    hcar_ref[...] = carry
    h = jnp.concatenate(pieces, axis=0)
    yield
    y_ref[:, 0:wa] = (h * _gelu_tanh(proj_ref[:, wa:2 * wa])).astype(y_ref.dtype)
    yield


def _even_mixer_kernel(h_ref, mnorm_ref, win_ref, convw_ref, convb_ref, wgate_ref, bgate_ref, lrua_ref,
                       wgk_ref, bgk_ref, gnorm_ref, y_ref, proj_ref, xtail_ref, hcar_ref, st_ref,
                       *, nb, ts, wa, dk, dv, rank):
    heads = GLA_HEADS
    wq = heads * dk
    wb = heads * dv
    o_q = 2 * wa
    o_k, o_v = o_q + wq, o_q + 2 * wq
    o_g, o_gk = o_v + wb, o_v + 2 * wb
    step = pl.program_id(1)

    @pl.when(step == 0)
    def _():
        xtail_ref[...] = jnp.zeros_like(xtail_ref)
        hcar_ref[...] = jnp.zeros_like(hcar_ref)
        st_ref[...] = jnp.zeros_like(st_ref)

    _interleave(_project_steps(h_ref, mnorm_ref, win_ref, proj_ref))
    for b in range(nb):
        _interleave(_rg_lru_steps(proj_ref.at[b], y_ref.at[b], xtail_ref.at[b], hcar_ref.at[b], convw_ref, convb_ref,
                                  wgate_ref, bgate_ref, lrua_ref, ts=ts, wa=wa))

    cg = MIX_CHUNK
    tri_incl = jnp.where(_tri(cg), 1.0, 0.0).astype(BF16)
    causal = _tri(cg)
    rowc = lax.broadcasted_iota(jnp.int32, (cg, wq), 0)
    items = []
    for b in range(nb):
        z = _dot(proj_ref[b, :, o_gk:o_gk + rank], wgk_ref[...], precision=HIGHEST) + bgk_ref[...]
        log_f = _log_sigmoid(z) * (1.0 / GLA_GATE_NORM)
        for c in range(ts // cg):
            r0 = c * cg
            cum = _dot_split_rhs(tri_incl, log_f[r0:r0 + cg], 3)
            q = proj_ref[b, r0:r0 + cg, o_q:o_q + wq] * (dk ** -0.5)
            k = proj_ref[b, r0:r0 + cg, o_k:o_k + wq]
            v = proj_ref[b, r0:r0 + cg, o_v:o_v + wb].astype(BF16)
            cum_last = cum[cg - 1:cg]
            it = dict(b=b, c=c, v=v, q_in=(q * jnp.exp(cum)).astype(BF16),
                      k_out=(k * jnp.exp(cum_last - cum)).astype(BF16), chunk_decay=jnp.exp(cum_last), qb=[], kb=[])
            for ib in range(cg // GLA_SUB):
                b0 = ib * GLA_SUB
                base = cum[b0 - 1:b0] if ib > 0 else jnp.zeros((1, wq), F32)
                it["qb"].append((q[b0:b0 + GLA_SUB] * jnp.exp(cum[b0:b0 + GLA_SUB] - base)).astype(BF16))
                it["kb"].append(jnp.where(rowc < b0 + GLA_SUB, k * jnp.exp(base - cum), 0.0).astype(BF16))
            items.append(it)
    for it in items:
        it["scores"] = [
            jnp.where(causal, jnp.concatenate(
                [_dot_nt(qb[:, hd * dk:(hd + 1) * dk], kb[:, hd * dk:(hd + 1) * dk])
                 for qb, kb in zip(it["qb"], it["kb"])], axis=0), 0.0).astype(BF16)
            for hd in range(heads)]
    for it in items:
        it["intra"] = [_dot(it["scores"][hd], it["v"][:, hd * dv:(hd + 1) * dv]) for hd in range(heads)]
    for c in range(ts // cg):
        r0 = c * cg
        for it in [t for t in items if t["c"] == c]:
            b = it["b"]
            outs = []
            for hd in range(heads):
                ks = slice(hd * dk, (hd + 1) * dk)
                vs = slice(hd * dv, (hd + 1) * dv)
                st = st_ref[b * heads + hd]
                o = it["intra"][hd] + _dot_nt(it["q_in"][:, ks], st.astype(BF16))
                st_ref[b * heads + hd] = st * it["chunk_decay"][:, ks] + _dot_tn(it["v"][:, vs], it["k_out"][:, ks])
                outs.append(o * lax.rsqrt(jnp.mean(o * o, axis=-1, keepdims=True) + HEAD_NORM_EPS))
            o_all = jnp.concatenate(outs, axis=-1) * gnorm_ref[...]
            g = proj_ref[b, r0:r0 + cg, o_g:o_g + wb]
            y_ref[b, r0:r0 + cg, wa:wa + wb] = (o_all * _silu(g)).astype(y_ref.dtype)


def _even_mixer(h, mnorm, win, convw, convb, wgate, bgate, lrua, wgk, bgk, gnorm, *, nb, ts, wa, dk, dv, rank):
    bsz, seq, d = h.shape
    cols = win.shape[1]
    heads = GLA_HEADS
    wb = heads * dv
    consts = (mnorm, win, convw, convb, wgate, bgate, lrua, wgk, bgk, gnorm)
    kern = functools.partial(_even_mixer_kernel, nb=nb, ts=ts, wa=wa, dk=dk, dv=dv, rank=rank)
    return pl.pallas_call(
        kern,
        grid=(bsz // nb, seq // ts),
        in_specs=[pl.BlockSpec((nb, ts, d), lambda b, s: (b, s, 0))] + [_resident(t) for t in consts],
        out_specs=pl.BlockSpec((nb, ts, wa + wb), lambda b, s: (b, s, 0)),
        out_shape=jax.ShapeDtypeStruct((bsz, seq, wa + wb), BF16),
        scratch_shapes=[pltpu.VMEM((nb, ts, cols), F32), pltpu.VMEM((nb, 8, wa), F32),
                        pltpu.VMEM((nb, 1, wa), F32), pltpu.VMEM((nb * heads, dv, dk), F32)],
        compiler_params=pltpu.CompilerParams(dimension_semantics=("arbitrary", "arbitrary"),
                                             vmem_limit_bytes=VMEM_LIMIT),
        name="even_mixer",
    )(h, *consts)


def _odd_mixer_kernel(h_ref, pos_ref, mnorm_ref, win_ref, freq_ref, rnorm_ref, mu_ref, w0_ref, ww2_ref, a0_ref,
                      aw2_ref, gw2_ref, kk_ref, ka_ref, rk_ref, wnorm_ref, seg_ref,
                      y_ref, proj_ref, rstate_ref, dprev_ref, wstate_ref, *, nb, ts, wc, wd, rw, ra, rg):
    f32 = F32

    @pl.when(pl.program_id(1) == 0)
    def _():
        rstate_ref[...] = jnp.zeros_like(rstate_ref)
        dprev_ref[...] = jnp.zeros_like(dprev_ref)
        wstate_ref[...] = jnp.zeros_like(wstate_ref)

    _interleave(_project_steps(h_ref, mnorm_ref, win_ref, proj_ref))

    o_d = 4 * wc
    dcols = 3 * wd + rw + ra + rg
    hdim = RWKV_HEAD_DIM
    cm = MIX_CHUNK
    n = 2 * cm
    groups = wd // LANES
    seg = seg_ref[...]
    lane = lax.broadcasted_iota(jnp.int32, (cm, LANES), 1)
    first_head = lane < hdim

    def stack_heads(t):
        return jnp.concatenate([jnp.where(first_head, t, 0.0), jnp.where(first_head, 0.0, t)], axis=0)

    rin = lax.broadcasted_iota(jnp.int32, (n, n), 0)
    cin = lax.broadcasted_iota(jnp.int32, (n, n), 1)
    strict = (rin & (cm - 1)) > (cin & (cm - 1))
    incl = (rin & (cm - 1)) >= (cin & (cm - 1))
    eye = jnp.where(rin == cin, 1.0, 0.0).astype(f32)
    tri_incl = jnp.where(_tri(cm), 1.0, 0.0).astype(BF16)

    work = []
    tails = {}

    def prepare_steps(b):
        dpart = proj_ref[b, :, o_d:o_d + dcols]
        dshift = jnp.concatenate([dprev_ref[b], dpart], axis=0)[7:7 + ts]
        dprev_ref[b] = dpart[ts - 8:ts]
        dpart = dpart + mu_ref[...] * (dshift - dpart)
        yield
        r = dpart[:, 0:wd]
        k = dpart[:, wd:2 * wd]
        v = dpart[:, 2 * wd:3 * wd]
        w_lr = dpart[:, 3 * wd:3 * wd + rw]
        a_lr = dpart[:, 3 * wd + rw:3 * wd + rw + ra]
        g_lr = dpart[:, 3 * wd + rw + ra:dcols]
        w = -_softplus(-(w0_ref[...] + _dot(jnp.tanh(w_lr).astype(BF16), ww2_ref[...]))) - 0.5
        log_decay = -jnp.exp(w)
        a = jax.nn.sigmoid(a0_ref[...] + _dot(a_lr.astype(BF16), aw2_ref[...]))
        g = _dot(jax.nn.sigmoid(g_lr).astype(BF16), gw2_ref[...])
        yield
        kk = k * kk_ref[...]
        kk = kk * lax.rsqrt(_head_sums(kk * kk, seg) + 1e-12)
        k = k * (1.0 + (a - 1.0) * ka_ref[...])
        tails[b] = (_head_sums(r * k * rk_ref[...], seg) * v, g)
        kb = kk * a
        yield
        for ch in range(ts // cm):
            rs = slice(ch * cm, (ch + 1) * cm)
            ld = log_decay[rs]
            cum = _dot_split_rhs(tri_incl, ld, 3)
            cum_last = cum[cm - 1:cm]
            p_incl = jnp.exp(cum)
            p_inv = jnp.exp(-cum)
            a_t = -kk[rs] * jnp.exp(cum - ld)
            r_t = r[rs] * p_incl
            b_t = kb[rs] * p_inv
            k_t = k[rs] * p_inv
            p_out = jnp.exp(cum_last - cum)
            b_o = kb[rs] * p_out
            k_o = k[rs] * p_out
            p_last = jnp.exp(cum_last)
            yield
            for pr in range(groups):
                ls = slice(pr * LANES, (pr + 1) * LANES)
                l4 = jnp.concatenate([stack_heads(a_t[:, ls]), stack_heads(r_t[:, ls])], axis=0).astype(BF16)
                r4 = jnp.concatenate([stack_heads(b_t[:, ls]), stack_heads(k_t[:, ls])], axis=0).astype(BF16)
                o4 = jnp.concatenate([stack_heads(b_o[:, ls]), stack_heads(k_o[:, ls])], axis=0).astype(BF16)
                vbd = stack_heads(v[rs, ls]).astype(BF16)
                work.append(dict(b=b, ch=ch, pr=pr, l4=l4, r4=r4, o4=o4, vbd=vbd, p_last=p_last[:, ls]))
                yield

    for b in range(nb):
        _interleave(prepare_steps(b))
    for wk in work:
        big = _dot_nt(wk["l4"], wk["r4"])
        a_ab = jnp.where(strict, big[:n, :n], 0.0)
        wk["a_ak"] = jnp.where(strict, big[:n, n:], 0.0).astype(BF16)
        wk["a_r"] = jnp.concatenate([jnp.where(incl, big[n:, :n], 0.0), jnp.where(incl, big[n:, n:], 0.0)],
                                    axis=1).astype(BF16)
        wk["x"] = a_ab.astype(BF16)
        wk["tinv"] = eye + a_ab
    span = 2
    while span < cm:
        for wk in work:
            wk["x"] = _dot(wk["x"], wk["x"]).astype(BF16)
        for wk in work:
            wk["tinv"] = wk["tinv"] + _dot(wk["tinv"].astype(BF16), wk["x"])
        span *= 2

    c = ts
    dh = wc // RET_HEADS
    half = dh // 2
    rowi = lax.broadcasted_iota(jnp.int32, (c, c), 0)
    coli = lax.broadcasted_iota(jnp.int32, (c, c), 1)
    rel = (rowi - coli).astype(f32)
    rowd = lax.broadcasted_iota(jnp.int32, (c, dh), 0).astype(f32)
    ret = []
    for b in range(nb):
        ang = freq_ref[...] * pos_ref[b]
        cos, sin = jnp.cos(ang).T, jnp.sin(ang).T
        cos2 = jnp.concatenate([cos, cos], axis=-1)
        sin2 = jnp.concatenate([-sin, sin], axis=-1)
        for hd in range(RET_HEADS):
            log_gamma = math.log1p(-2.0 ** (-5.0 - hd))
            q = proj_ref[b, :, hd * dh:(hd + 1) * dh]
            k = proj_ref[b, :, wc + hd * dh:wc + (hd + 1) * dh]
            v = proj_ref[b, :, 2 * wc + hd * dh:2 * wc + (hd + 1) * dh].astype(BF16)
            q = (q * cos2 + pltpu.roll(q, half, 1) * sin2) * (dh ** -0.5)
            k = k * cos2 + pltpu.roll(k, half, 1) * sin2
            dmask = jnp.where(rel >= 0.0, jnp.exp(jnp.maximum(rel, 0.0) * log_gamma), 0.0)
            ret.append(dict(b=b, hd=hd, v=v, q=q.astype(BF16), k=k.astype(BF16), dmask=dmask,
                            q_in=(q * jnp.exp((rowd + 1.0) * log_gamma)).astype(BF16),
                            k_out=(k * jnp.exp((c - 1.0 - rowd) * log_gamma)).astype(BF16),
                            decay=math.exp(c * log_gamma)))
    for rt in ret:
        rt["scores"] = (_dot_nt(rt["q"], rt["k"]) * rt["dmask"]).astype(BF16)
    for rt in ret:
        st = rstate_ref[rt["b"] * RET_HEADS + rt["hd"]]
        o = _dot(rt["scores"], rt["v"]) + _dot(rt["q_in"], st.astype(BF16))
        rstate_ref[rt["b"] * RET_HEADS + rt["hd"]] = st * rt["decay"] + _dot_tn(rt["k_out"], rt["v"])
        o = o - jnp.mean(o, axis=-1, keepdims=True)
        rt["o"] = o * lax.rsqrt(jnp.mean(o * o, axis=-1, keepdims=True) + HEAD_NORM_EPS)
    for b in range(nb):
        g = proj_ref[b, :, 3 * wc:4 * wc]
        o_all = jnp.concatenate([rt["o"] for rt in ret if rt["b"] == b], axis=-1)
        y_ref[b, :, 0:wc] = (o_all * rnorm_ref[...] * _silu(g)).astype(y_ref.dtype)

    for ch in range(ts // cm):
        rs = slice(ch * cm, (ch + 1) * cm)
        wks = [wk for wk in work if wk["ch"] == ch]
        sidx = [wk["b"] * groups + wk["pr"] for wk in wks]
        sts = [wstate_ref[i] for i in sidx]
        fss = [_dot_nt(wk["l4"], st.astype(BF16)) for wk, st in zip(wks, sts)]
        rhs = [(fs[:n] + _dot(wk["a_ak"], wk["vbd"])).astype(BF16) for wk, fs in zip(wks, fss)]
        us = [_dot(wk["tinv"].astype(BF16), rh) for wk, rh in zip(wks, rhs)]
        uvs = [jnp.concatenate([u.astype(BF16), wk["vbd"]], axis=0) for wk, u in zip(wks, us)]
        for i, st, wk, uv in zip(sidx, sts, wks, uvs):
            wstate_ref[i] = st * wk["p_last"] + _dot_tn(uv, wk["o4"])
        ybds = [fs[n:] + _dot(wk["a_r"], uv) for wk, fs, uv in zip(wks, fss, uvs)]
        for b in range(nb):
            y = jnp.concatenate([yb[:cm] + yb[cm:] for wk, yb in zip(wks, ybds) if wk["b"] == b], axis=-1)
            inv_h = 1.0 / hdim
            y = y - _head_sums(y, seg) * inv_h
            y = y * lax.rsqrt(_head_sums(y * y, seg) * inv_h + RWKV_LN_EPS)
            bonus, g = tails[b]
            y = (y * wnorm_ref[...] + bonus[rs]) * g[rs]
            y_ref[b, ch * cm:(ch + 1) * cm, wc:wc + wd] = y.astype(y_ref.dtype)


def _odd_mixer(h, pos, mnorm, win, freq, rnorm, mu, w0, ww2, a0, aw2, gw2, kk, ka, rk, wnorm, seg, *, nb, ts, wc, wd, rw,
               ra, rg):
    bsz, seq, d = h.shape
    cols = win.shape[1]
    dh = wc // RET_HEADS
    consts = (mnorm, win, freq, rnorm, mu, w0, ww2, a0, aw2, gw2, kk, ka, rk, wnorm, seg)
    kern = functools.partial(_odd_mixer_kernel, nb=nb, ts=ts, wc=wc, wd=wd, rw=rw, ra=ra, rg=rg)
    dcols = 3 * wd + rw + ra + rg
    return pl.pallas_call(
        kern,
        grid=(bsz // nb, seq // ts),
        in_specs=[pl.BlockSpec((nb, ts, d), lambda b, s: (b, s, 0)),
                  pl.BlockSpec((nb, 1, ts), lambda b, s: (b, 0, s))] + [_resident(t) for t in consts],
        out_specs=pl.BlockSpec((nb, ts, wc + wd), lambda b, s: (b, s, 0)),
        out_shape=jax.ShapeDtypeStruct((bsz, seq, wc + wd), BF16),
        scratch_shapes=[pltpu.VMEM((nb, ts, cols), F32), pltpu.VMEM((nb * RET_HEADS, dh, dh), F32),
                        pltpu.VMEM((nb, 8, dcols), F32), pltpu.VMEM((nb * (wd // LANES), LANES, LANES), F32)],
        compiler_params=pltpu.CompilerParams(dimension_semantics=("arbitrary", "arbitrary"),
                                             vmem_limit_bytes=VMEM_LIMIT),
        name="odd_mixer",
    )(h, pos, *consts)


def _post_kernel(h_ref, y_ref, p_ref, wout_ref, fnorm_ref, wg_ref, wu_ref, wd_ref, pnorm_ref, pwg_ref, pbg_ref,
                 pwp_ref, onorm_ref, o_ref, *, final):
    h = h_ref[...] + _dot(y_ref[...], wout_ref[...])
    xn = _rms(h, fnorm_ref[...]).astype(BF16)
    act = (_silu(_dot(xn, wg_ref[...])) * _dot(xn, wu_ref[...])).astype(BF16)
    h = h + _dot(act, wd_ref[...])
    xg = _rms(h, pnorm_ref[...]).astype(BF16)
    gate = jax.nn.sigmoid(_dot(xg, pwg_ref[...]) + pbg_ref[...])
    h = h + gate * _dot(p_ref[...].astype(BF16), pwp_ref[...])
    if final:
        h = _rms(h, onorm_ref[...])
    o_ref[...] = h


def _post(h, y, p, wout, fnorm, wg, wu, wd, pnorm, pwg, pbg, pwp, onorm, *, layer, tm, final):
    m, d = h.shape
    rows = lambda width: pl.BlockSpec((tm, width), lambda i: (i, 0))
    resident = lambda t: pl.BlockSpec(t.shape, lambda i: (0, 0), pipeline_mode=pl.Buffered(1))
    of_layer = lambda t: pl.BlockSpec((None,) + t.shape[1:], lambda i: (layer, 0, 0), pipeline_mode=pl.Buffered(1))
    stacked = (fnorm, wg, wu, wd, pnorm, pwg, pbg, pwp)
    return pl.pallas_call(
        functools.partial(_post_kernel, final=final),
        grid=(m // tm,),
        in_specs=[rows(d), rows(y.shape[1]), pl.BlockSpec((None, tm, p.shape[2]), lambda i: (layer, i, 0)),
                  resident(wout)] + [of_layer(t) for t in stacked] + [resident(onorm)],
        out_specs=rows(d),
        out_shape=jax.ShapeDtypeStruct((m, d), F32),
        compiler_params=pltpu.CompilerParams(dimension_semantics=("arbitrary",), vmem_limit_bytes=VMEM_LIMIT),
        name="post_mixer",
    )(h, y, p, wout, *stacked, onorm)


def _block_diag(blocks):
    g, bi, bj = blocks.shape
    eye = jnp.eye(g, dtype=blocks.dtype)
    return (eye[:, None, :, None] * blocks[:, :, None, :]).reshape(g * bi, g * bj)


def kernel(x, p, positions, ev_w_in, ev_conv_w, ev_conv_b, ev_lru_wr, ev_lru_br, ev_lru_wi, ev_lru_bi, ev_lru_a, ev_gla_wgk, ev_gla_bgk, ev_gla_norm, ev_w_out, od_w_in, od_ret_norm, od_rwkv_mu, od_rwkv_w0, od_rwkv_ww2, od_rwkv_a0, od_rwkv_aw2, od_rwkv_gw2, od_rwkv_kk, od_rwkv_ka, od_rwkv_rk, od_rwkv_norm, od_w_out, mix_norm, ffn_norm, ffn_w_gate, ffn_w_up, ffn_w_down, ple_norm, ple_w_gate, ple_b_gate, ple_w_proj, final_norm):
    bsz, seq, d = x.shape
    depth = p.shape[0]
    m = bsz * seq
    row = lambda t: t.reshape(1, -1).astype(F32)
    rows3 = lambda t: t.reshape(t.shape[0], 1, -1).astype(F32)

    wa = ev_conv_w.shape[-1]
    rank = ev_gla_wgk.shape[1]
    wq = ev_gla_wgk.shape[2]
    wb = ev_gla_norm.shape[-1]
    dk, dv = wq // GLA_HEADS, wb // GLA_HEADS
    wc = od_ret_norm.shape[-1]
    wd = od_rwkv_w0.shape[-1]
    rw, ra, rg = od_rwkv_ww2.shape[1], od_rwkv_aw2.shape[1], od_rwkv_gw2.shape[1]
    nb = MIX_BATCH if bsz % MIX_BATCH == 0 else 1

    h = x.reshape(m, d)
    freq = (ROPE_BASE ** (-jnp.arange(wc // RET_HEADS // 2, dtype=F32) / (wc // RET_HEADS // 2))).reshape(-1, 1)
    pos = positions.astype(F32).reshape(bsz, 1, seq)
    seg = _block_diag(jnp.ones((LANES // RWKV_HEAD_DIM, RWKV_HEAD_DIM, RWKV_HEAD_DIM), BF16))
    assert 2 * RWKV_HEAD_DIM == LANES and wd % LANES == 0

    for i in range(depth):
        j = i // 2
        if i % 2 == 0:
            w_in = ev_w_in[j]
            cols = w_in.shape[1]
            cols_pad = -(-cols // LANES) * LANES
            w_in = jnp.pad(w_in, ((0, 0), (0, cols_pad - cols))).astype(BF16)
            wgate = jnp.concatenate([_block_diag(ev_lru_wr[j]), _block_diag(ev_lru_wi[j])], axis=1).astype(BF16)
            bgate = jnp.concatenate([ev_lru_br[j], ev_lru_bi[j]]).reshape(1, -1)
            y = _even_mixer(h.reshape(bsz, seq, d), row(mix_norm[i]), w_in, ev_conv_w[j], row(ev_conv_b[j]), wgate, bgate,
                            row(ev_lru_a[j]), ev_gla_wgk[j], row(ev_gla_bgk[j]), row(ev_gla_norm[j]),
                            nb=nb, ts=4 * MIX_CHUNK, wa=wa, dk=dk, dv=dv, rank=rank)
            w_out = ev_w_out[j]
        else:
            y = _odd_mixer(h.reshape(bsz, seq, d), pos, row(mix_norm[i]), od_w_in[j].astype(BF16), freq,
                           row(od_ret_norm[j]), row(od_rwkv_mu[j]),
                           row(od_rwkv_w0[j]), od_rwkv_ww2[j].astype(BF16), row(od_rwkv_a0[j]),
                           od_rwkv_aw2[j].astype(BF16), od_rwkv_gw2[j].astype(BF16), row(od_rwkv_kk[j]),
                           row(od_rwkv_ka[j]), row(od_rwkv_rk[j]), row(od_rwkv_norm[j]), seg,
                           nb=nb, ts=RET_CHUNK, wc=wc, wd=wd, rw=rw, ra=ra, rg=rg)
            w_out = od_w_out[j]
        h = _post(h, y.reshape(m, -1), p.reshape(depth, m, -1), w_out.astype(BF16), rows3(ffn_norm),
                  ffn_w_gate.astype(BF16), ffn_w_up.astype(BF16), ffn_w_down.astype(BF16),
                  rows3(ple_norm), ple_w_gate.astype(BF16), rows3(ple_b_gate), ple_w_proj.astype(BF16),
                  row(final_norm), layer=i, tm=512, final=(i == depth - 1))
    return h.reshape(bsz, seq, d)
```

```python
import functools
import math

import jax
import jax.numpy as jnp
from jax import lax
from jax.experimental import pallas as pl
from jax.experimental.pallas import tpu as pltpu

F32 = jnp.float32
BF16 = jnp.bfloat16
HIGHEST = lax.Precision.HIGHEST

NORM_EPS = 1e-6
LANES = 128
SUBLANES = 8
CONV_W = 4
LRU_C = 8.0
GLA_HEADS = 4
GLA_GATE_NORM = 16.0
GLA_SUB = 16
RET_HEADS = 4
RET_CHUNK = 128
ROPE_BASE = 10000.0
RWKV_HEAD_DIM = 64
RWKV_LN_EPS = 64e-5
HEAD_NORM_EPS = 1e-5

MIX_CHUNK = 64
MIX_BATCH = 4
EVEN_TILE = 4 * MIX_CHUNK
ODD_TILE = RET_CHUNK
POST_ROWS = 512
VMEM_LIMIT = 56 * 1024 * 1024


def _log_sigmoid(x):
    return jnp.minimum(x, 0.0) - jnp.log1p(jnp.exp(-jnp.abs(x)))


def _softplus(x):
    return jnp.maximum(x, 0.0) + jnp.log1p(jnp.exp(-jnp.abs(x)))


def _silu(x):
    return x * jax.nn.sigmoid(x)


def _gelu_tanh(x):
    return 0.5 * x * (1.0 + jnp.tanh(math.sqrt(2.0 / math.pi) * (x + 0.044715 * (x * x * x))))


def _rms(x, g):
    return x * lax.rsqrt(jnp.mean(x * x, axis=-1, keepdims=True) + NORM_EPS) * g


def _dot(a, b, precision=None):
    return jnp.dot(a, b, preferred_element_type=F32, precision=precision)


def _dot_nt(a, b, precision=None):
    return lax.dot_general(a, b, (((1,), (1,)), ((), ())), preferred_element_type=F32, precision=precision)


def _dot_tn(a, b, precision=None):
    return lax.dot_general(a, b, (((0,), (0,)), ((), ())), preferred_element_type=F32, precision=precision)


def _head_sums(x, seg):
    return jnp.concatenate([_dot(x[:, g:g + LANES].astype(BF16), seg) for g in range(0, x.shape[1], LANES)], axis=-1)


def _dot_split_rhs(m, x, passes):
    acc = None
    for _ in range(passes):
        piece = x.astype(BF16)
        acc = _dot(m, piece) if acc is None else acc + _dot(m, piece)
        x = x - piece.astype(F32)
    return acc


def _tri(n, strict=False):
    row = lax.broadcasted_iota(jnp.int32, (n, n), 0)
    col = lax.broadcasted_iota(jnp.int32, (n, n), 1)
    return (row > col) if strict else (row >= col)


PROJ_GROUP = 512


def _project(h_ref, mnorm_ref, win_ref, proj_ref):
    nb, ts, d = h_ref.shape
    xn = _rms(h_ref[...].reshape(nb * ts, d), mnorm_ref[...]).astype(BF16)
    cols = proj_ref.shape[2]
    for c0 in range(0, cols, PROJ_GROUP):
        c1 = min(c0 + PROJ_GROUP, cols)
        proj_ref[:, :, c0:c1] = _dot(xn, win_ref[:, c0:c1]).reshape(nb, ts, c1 - c0)


def _resident(t):
    return pl.BlockSpec(t.shape, lambda b, s: (0,) * t.ndim, pipeline_mode=pl.Buffered(1))


def _rg_lru(proj_ref, y_ref, xtail_ref, hcar_ref, convw_ref, convb_ref, wgate_ref, bgate_ref, lrua_ref, *, nb, ts, wa):
    cw = convw_ref[...]
    xcs = []
    for b in range(nb):
        ax = proj_ref[b, :, 0:wa]
        xcat = jnp.concatenate([xtail_ref[b], ax], axis=0)
        xtail_ref[b] = ax[ts - 8:ts]
        xc = convb_ref[...] + cw[CONV_W - 1:CONV_W] * ax
        for s in range(1, CONV_W):
            xc = xc + cw[CONV_W - 1 - s:CONV_W - s] * xcat[8 - s:8 - s + ts]
        xcs.append(xc)
    gates_all = _dot(jnp.concatenate(xcs, axis=0).astype(BF16), wgate_ref[...]) + bgate_ref[...]
    log_sig_a = _log_sigmoid(lrua_ref[...])
    groups = ts // SUBLANES
    row = lax.broadcasted_iota(jnp.int32, (groups, SUBLANES, wa), 1)
    for b in range(nb):
        xc = xcs[b]
        gates = gates_all[b * ts:(b + 1) * ts]
        r = jax.nn.sigmoid(gates[:, :wa])
        i = jax.nn.sigmoid(gates[:, wa:])
        log_a = LRU_C * r * log_sig_a
        a = jnp.exp(log_a)
        t = jnp.tanh(log_a)
        u = jnp.sqrt(-2.0 * t / (1.0 - t)) * (i * xc)
        u = u.reshape(groups, SUBLANES, wa)
        a = a.reshape(groups, SUBLANES, wa)
        d = 1
        while d < SUBLANES:
            keep = row >= d
            u = jnp.where(keep, a * pltpu.roll(u, d, 1) + u, u)
            a = jnp.where(keep, a * pltpu.roll(a, d, 1), a)
            d *= 2
        carry = hcar_ref[b]
        pieces = []
        for j in range(groups):
            hj = u[j] + a[j] * carry
            carry = hj[SUBLANES - 1:SUBLANES]
            pieces.append(hj)
        hcar_ref[b] = carry
        h = jnp.concatenate(pieces, axis=0)
        y_ref[b, :, 0:wa] = (h * _gelu_tanh(proj_ref[b, :, wa:2 * wa])).astype(y_ref.dtype)


def _even_mixer_kernel(h_ref, mnorm_ref, win_ref, convw_ref, convb_ref, wgate_ref, bgate_ref, lrua_ref,
                       wgk_ref, bgk_ref, gnorm_ref, y_ref, proj_ref, xtail_ref, hcar_ref, st_ref,
                       *, nb, ts, wa, dk, dv, rank):
    heads = GLA_HEADS
    wq = heads * dk
    wb = heads * dv
    o_q = 2 * wa
    o_k, o_v = o_q + wq, o_q + 2 * wq
    o_g, o_gk = o_v + wb, o_v + 2 * wb
    step = pl.program_id(1)

    @pl.when(step == 0)
    def _():
        xtail_ref[...] = jnp.zeros_like(xtail_ref)
        hcar_ref[...] = jnp.zeros_like(hcar_ref)
        st_ref[...] = jnp.zeros_like(st_ref)

    _project(h_ref, mnorm_ref, win_ref, proj_ref)
    _rg_lru(proj_ref, y_ref, xtail_ref, hcar_ref, convw_ref, convb_ref, wgate_ref, bgate_ref, lrua_ref,
            nb=nb, ts=ts, wa=wa)

    cg = MIX_CHUNK
    tri_incl = jnp.where(_tri(cg), 1.0, 0.0).astype(BF16)
    causal = _tri(cg)
    rowc = lax.broadcasted_iota(jnp.int32, (cg, wq), 0)
    items = []
    for b in range(nb):
        z = _dot(proj_ref[b, :, o_gk:o_gk + rank], wgk_ref[...], precision=HIGHEST) + bgk_ref[...]
        log_f = _log_sigmoid(z) * (1.0 / GLA_GATE_NORM)
        for c in range(ts // cg):
            r0 = c * cg
            cum = _dot_split_rhs(tri_incl, log_f[r0:r0 + cg], 3)
            q = proj_ref[b, r0:r0 + cg, o_q:o_q + wq] * (dk ** -0.5)
            k = proj_ref[b, r0:r0 + cg, o_k:o_k + wq]
            v = proj_ref[b, r0:r0 + cg, o_v:o_v + wb].astype(BF16)
            cum_last = cum[cg - 1:cg]
            it = dict(b=b, c=c, v=v, q_in=(q * jnp.exp(cum)).astype(BF16),
                      k_out=(k * jnp.exp(cum_last - cum)).astype(BF16), chunk_decay=jnp.exp(cum_last), qb=[], kb=[])
            for ib in range(cg // GLA_SUB):
                b0 = ib * GLA_SUB
                base = cum[b0 - 1:b0] if ib > 0 else jnp.zeros((1, wq), F32)
                it["qb"].append((q[b0:b0 + GLA_SUB] * jnp.exp(cum[b0:b0 + GLA_SUB] - base)).astype(BF16))
                it["kb"].append(jnp.where(rowc < b0 + GLA_SUB, k * jnp.exp(base - cum), 0.0).astype(BF16))
            items.append(it)
    for it in items:
        it["scores"] = [
            jnp.where(causal, jnp.concatenate(
                [_dot_nt(qb[:, hd * dk:(hd + 1) * dk], kb[:, hd * dk:(hd + 1) * dk])
                 for qb, kb in zip(it["qb"], it["kb"])], axis=0), 0.0).astype(BF16)
            for hd in range(heads)]
    for it in items:
        it["intra"] = [_dot(it["scores"][hd], it["v"][:, hd * dv:(hd + 1) * dv]) for hd in range(heads)]
    for c in range(ts // cg):
        r0 = c * cg
        for it in [t for t in items if t["c"] == c]:
            b = it["b"]
            outs = []
            for hd in range(heads):
                ks = slice(hd * dk, (hd + 1) * dk)
                vs = slice(hd * dv, (hd + 1) * dv)
                st = st_ref[b * heads + hd]
                o = it["intra"][hd] + _dot_nt(it["q_in"][:, ks], st.astype(BF16))
                st_ref[b * heads + hd] = st * it["chunk_decay"][:, ks] + _dot_tn(it["v"][:, vs], it["k_out"][:, ks])
                outs.append(o * lax.rsqrt(jnp.mean(o * o, axis=-1, keepdims=True) + HEAD_NORM_EPS))
            o_all = jnp.concatenate(outs, axis=-1) * gnorm_ref[...]
            g = proj_ref[b, r0:r0 + cg, o_g:o_g + wb]
            y_ref[b, r0:r0 + cg, wa:wa + wb] = (o_all * _silu(g)).astype(y_ref.dtype)


def _even_mixer(h, mnorm, win, convw, convb, wgate, bgate, lrua, wgk, bgk, gnorm, *, nb, ts, wa, dk, dv, rank):
    bsz, seq, d = h.shape
    cols = win.shape[1]
    heads = GLA_HEADS
    wb = heads * dv
    consts = (mnorm, win, convw, convb, wgate, bgate, lrua, wgk, bgk, gnorm)
    kern = functools.partial(_even_mixer_kernel, nb=nb, ts=ts, wa=wa, dk=dk, dv=dv, rank=rank)
    return pl.pallas_call(
        kern,
        grid=(bsz // nb, seq // ts),
        in_specs=[pl.BlockSpec((nb, ts, d), lambda b, s: (b, s, 0))] + [_resident(t) for t in consts],
        out_specs=pl.BlockSpec((nb, ts, wa + wb), lambda b, s: (b, s, 0)),
        out_shape=jax.ShapeDtypeStruct((bsz, seq, wa + wb), BF16),
        scratch_shapes=[pltpu.VMEM((nb, ts, cols), F32), pltpu.VMEM((nb, 8, wa), F32),
                        pltpu.VMEM((nb, 1, wa), F32), pltpu.VMEM((nb * heads, dv, dk), F32)],
        compiler_params=pltpu.CompilerParams(dimension_semantics=("arbitrary", "arbitrary"),
                                             vmem_limit_bytes=VMEM_LIMIT),
        name="even_mixer",
    )(h, *consts)


def _odd_mixer_kernel(h_ref, pos_ref, mnorm_ref, win_ref, freq_ref, rnorm_ref, mu_ref, w0_ref, ww2_ref, a0_ref,
                      aw2_ref, gw2_ref, kk_ref, ka_ref, rk_ref, wnorm_ref, seg_ref,
                      y_ref, proj_ref, rstate_ref, dprev_ref, wstate_ref, *, nb, ts, wc, wd, rw, ra, rg):
    f32 = F32

    @pl.when(pl.program_id(1) == 0)
    def _():
        rstate_ref[...] = jnp.zeros_like(rstate_ref)
        dprev_ref[...] = jnp.zeros_like(dprev_ref)
        wstate_ref[...] = jnp.zeros_like(wstate_ref)

    _project(h_ref, mnorm_ref, win_ref, proj_ref)

    o_d = 4 * wc
    dcols = 3 * wd + rw + ra + rg
    hdim = RWKV_HEAD_DIM
    cm = MIX_CHUNK
    n = 2 * cm
    groups = wd // LANES
    seg = seg_ref[...]
    lane = lax.broadcasted_iota(jnp.int32, (cm, LANES), 1)
    first_head = lane < hdim

    def stack_heads(t):
        return jnp.concatenate([jnp.where(first_head, t, 0.0), jnp.where(first_head, 0.0, t)], axis=0)

    rin = lax.broadcasted_iota(jnp.int32, (n, n), 0)
    cin = lax.broadcasted_iota(jnp.int32, (n, n), 1)
    strict = (rin & (cm - 1)) > (cin & (cm - 1))
    incl = (rin & (cm - 1)) >= (cin & (cm - 1))
    eye = jnp.where(rin == cin, 1.0, 0.0).astype(f32)
    tri_incl = jnp.where(_tri(cm), 1.0, 0.0).astype(BF16)

    work = []
    dparts = []
    for b in range(nb):
        dpart = proj_ref[b, :, o_d:o_d + dcols]
        dshift = jnp.concatenate([dprev_ref[b], dpart], axis=0)[7:7 + ts]
        dprev_ref[b] = dpart[ts - 8:ts]
        dparts.append(dpart + mu_ref[...] * (dshift - dpart))
    dall = jnp.concatenate(dparts, axis=0)
    r_all = dall[:, 0:wd]
    k_all = dall[:, wd:2 * wd]
    v_all = dall[:, 2 * wd:3 * wd]
    w_lr = dall[:, 3 * wd:3 * wd + rw]
    a_lr = dall[:, 3 * wd + rw:3 * wd + rw + ra]
    g_lr = dall[:, 3 * wd + rw + ra:dcols]
    w_all = -_softplus(-(w0_ref[...] + _dot(jnp.tanh(w_lr).astype(BF16), ww2_ref[...]))) - 0.5
    log_decay_all = -jnp.exp(w_all)
    a_all = jax.nn.sigmoid(a0_ref[...] + _dot(a_lr.astype(BF16), aw2_ref[...]))
    g_all = _dot(jax.nn.sigmoid(g_lr).astype(BF16), gw2_ref[...])
    kk_all = k_all * kk_ref[...]
    kk_all = kk_all * lax.rsqrt(_head_sums(kk_all * kk_all, seg) + 1e-12)
    k_all = k_all * (1.0 + (a_all - 1.0) * ka_ref[...])
    bonus_all = _head_sums(r_all * k_all * rk_ref[...], seg) * v_all
    kb_all = kk_all * a_all
    for b in range(nb):
        for ch in range(ts // cm):
            rs = slice(b * ts + ch * cm, b * ts + (ch + 1) * cm)
            ld = log_decay_all[rs]
            cum = _dot_split_rhs(tri_incl, ld, 3)
            cum_last = cum[cm - 1:cm]
            p_incl = jnp.exp(cum)
            p_inv = jnp.exp(-cum)
            a_t = -kk_all[rs] * jnp.exp(cum - ld)
            r_t = r_all[rs] * p_incl
            b_t = kb_all[rs] * p_inv
            k_t = k_all[rs] * p_inv
            p_out = jnp.exp(cum_last - cum)
            b_o = kb_all[rs] * p_out
            k_o = k_all[rs] * p_out
            p_last = jnp.exp(cum_last)
            for pr in range(groups):
                ls = slice(pr * LANES, (pr + 1) * LANES)
                l4 = jnp.concatenate([stack_heads(a_t[:, ls]), stack_heads(r_t[:, ls])], axis=0).astype(BF16)
                r4 = jnp.concatenate([stack_heads(b_t[:, ls]), stack_heads(k_t[:, ls])], axis=0).astype(BF16)
                o4 = jnp.concatenate([stack_heads(b_o[:, ls]), stack_heads(k_o[:, ls])], axis=0).astype(BF16)
                vbd = stack_heads(v_all[rs, ls]).astype(BF16)
                work.append(dict(b=b, ch=ch, pr=pr, l4=l4, r4=r4, o4=o4, vbd=vbd, p_last=p_last[:, ls]))
    for wk in work:
        big = _dot_nt(wk["l4"], wk["r4"])
        a_ab = jnp.where(strict, big[:n, :n], 0.0)
        wk["a_ak"] = jnp.where(strict, big[:n, n:], 0.0).astype(BF16)
        wk["a_r"] = jnp.concatenate([jnp.where(incl, big[n:, :n], 0.0), jnp.where(incl, big[n:, n:], 0.0)],
                                    axis=1).astype(BF16)
        wk["x"] = a_ab.astype(BF16)
        wk["tinv"] = eye + a_ab
    span = 2
    while span < cm:
        for wk in work:
            wk["x"] = _dot(wk["x"], wk["x"]).astype(BF16)
        for wk in work:
            wk["tinv"] = wk["tinv"] + _dot(wk["tinv"].astype(BF16), wk["x"])
        span *= 2

    c = ts
    dh = wc // RET_HEADS
    half = dh // 2
    rowi = lax.broadcasted_iota(jnp.int32, (c, c), 0)
    coli = lax.broadcasted_iota(jnp.int32, (c, c), 1)
    rel = (rowi - coli).astype(f32)
    rowd = lax.broadcasted_iota(jnp.int32, (c, dh), 0).astype(f32)
    ret = []
    for b in range(nb):
        ang = freq_ref[...] * pos_ref[b]
        cos, sin = jnp.cos(ang).T, jnp.sin(ang).T
        cos2 = jnp.concatenate([cos, cos], axis=-1)
        sin2 = jnp.concatenate([-sin, sin], axis=-1)
        for hd in range(RET_HEADS):
            log_gamma = math.log1p(-2.0 ** (-5.0 - hd))
            q = proj_ref[b, :, hd * dh:(hd + 1) * dh]
            k = proj_ref[b, :, wc + hd * dh:wc + (hd + 1) * dh]
            v = proj_ref[b, :, 2 * wc + hd * dh:2 * wc + (hd + 1) * dh].astype(BF16)
            q = (q * cos2 + pltpu.roll(q, half, 1) * sin2) * (dh ** -0.5)
            k = k * cos2 + pltpu.roll(k, half, 1) * sin2
            dmask = jnp.where(rel >= 0.0, jnp.exp(jnp.maximum(rel, 0.0) * log_gamma), 0.0)
            ret.append(dict(b=b, hd=hd, v=v, q=q.astype(BF16), k=k.astype(BF16), dmask=dmask,
                            q_in=(q * jnp.exp((rowd + 1.0) * log_gamma)).astype(BF16),
                            k_out=(k * jnp.exp((c - 1.0 - rowd) * log_gamma)).astype(BF16),
                            decay=math.exp(c * log_gamma)))
    for rt in ret:
        rt["scores"] = (_dot_nt(rt["q"], rt["k"]) * rt["dmask"]).astype(BF16)
    for rt in ret:
        st = rstate_ref[rt["b"] * RET_HEADS + rt["hd"]]
        o = _dot(rt["scores"], rt["v"]) + _dot(rt["q_in"], st.astype(BF16))
        rstate_ref[rt["b"] * RET_HEADS + rt["hd"]] = st * rt["decay"] + _dot_tn(rt["k_out"], rt["v"])
        o = o - jnp.mean(o, axis=-1, keepdims=True)
        rt["o"] = o * lax.rsqrt(jnp.mean(o * o, axis=-1, keepdims=True) + HEAD_NORM_EPS)
    for b in range(nb):
        g = proj_ref[b, :, 3 * wc:4 * wc]
        o_all = jnp.concatenate([rt["o"] for rt in ret if rt["b"] == b], axis=-1)
        y_ref[b, :, 0:wc] = (o_all * rnorm_ref[...] * _silu(g)).astype(y_ref.dtype)

    for ch in range(ts // cm):
        wks = [wk for wk in work if wk["ch"] == ch]
        sidx = [wk["b"] * groups + wk["pr"] for wk in wks]
        sts = [wstate_ref[i] for i in sidx]
        fss = [_dot_nt(wk["l4"], st.astype(BF16)) for wk, st in zip(wks, sts)]
        rhs = [(fs[:n] + _dot(wk["a_ak"], wk["vbd"])).astype(BF16) for wk, fs in zip(wks, fss)]
        us = [_dot(wk["tinv"].astype(BF16), rh) for wk, rh in zip(wks, rhs)]
        uvs = [jnp.concatenate([u.astype(BF16), wk["vbd"]], axis=0) for wk, u in zip(wks, us)]
        for i, st, wk, uv in zip(sidx, sts, wks, uvs):
            wstate_ref[i] = st * wk["p_last"] + _dot_tn(uv, wk["o4"])
        ybds = [fs[n:] + _dot(wk["a_r"], uv) for wk, fs, uv in zip(wks, fss, uvs)]
        y = jnp.concatenate(
            [jnp.concatenate([yb[:cm] + yb[cm:] for wk, yb in zip(wks, ybds) if wk["b"] == b], axis=-1)
             for b in range(nb)], axis=0)
        inv_h = 1.0 / hdim
        y = y - _head_sums(y, seg) * inv_h
        y = y * lax.rsqrt(_head_sums(y * y, seg) * inv_h + RWKV_LN_EPS)
        for b in range(nb):
            tile_rows = slice(b * ts + ch * cm, b * ts + (ch + 1) * cm)
            yb = (y[b * cm:(b + 1) * cm] * wnorm_ref[...] + bonus_all[tile_rows]) * g_all[tile_rows]
            y_ref[b, ch * cm:(ch + 1) * cm, wc:wc + wd] = yb.astype(y_ref.dtype)


def _odd_mixer(h, pos, mnorm, win, freq, rnorm, mu, w0, ww2, a0, aw2, gw2, kk, ka, rk, wnorm, seg, *, nb, ts, wc, wd, rw,
               ra, rg):
    bsz, seq, d = h.shape
    cols = win.shape[1]
    dh = wc // RET_HEADS
    consts = (mnorm, win, freq, rnorm, mu, w0, ww2, a0, aw2, gw2, kk, ka, rk, wnorm, seg)
    kern = functools.partial(_odd_mixer_kernel, nb=nb, ts=ts, wc=wc, wd=wd, rw=rw, ra=ra, rg=rg)
    dcols = 3 * wd + rw + ra + rg
    return pl.pallas_call(
        kern,
        grid=(bsz // nb, seq // ts),
        in_specs=[pl.BlockSpec((nb, ts, d), lambda b, s: (b, s, 0)),
                  pl.BlockSpec((nb, 1, ts), lambda b, s: (b, 0, s))] + [_resident(t) for t in consts],
        out_specs=pl.BlockSpec((nb, ts, wc + wd), lambda b, s: (b, s, 0)),
        out_shape=jax.ShapeDtypeStruct((bsz, seq, wc + wd), BF16),
        scratch_shapes=[pltpu.VMEM((nb, ts, cols), F32), pltpu.VMEM((nb * RET_HEADS, dh, dh), F32),
                        pltpu.VMEM((nb, 8, dcols), F32), pltpu.VMEM((nb * (wd // LANES), LANES, LANES), F32)],
        compiler_params=pltpu.CompilerParams(dimension_semantics=("arbitrary", "arbitrary"),
                                             vmem_limit_bytes=VMEM_LIMIT),
        name="odd_mixer",
    )(h, pos, *consts)


def _post_kernel(h_ref, y_ref, p_ref, wout_ref, fnorm_ref, wg_ref, wu_ref, wd_ref, pnorm_ref, pwg_ref, pbg_ref,
                 pwp_ref, onorm_ref, o_ref, *, final):
    h = h_ref[...] + _dot(y_ref[...], wout_ref[...])
    xn = _rms(h, fnorm_ref[...]).astype(BF16)
    act = (_silu(_dot(xn, wg_ref[...])) * _dot(xn, wu_ref[...])).astype(BF16)
    h = h + _dot(act, wd_ref[...])
    xg = _rms(h, pnorm_ref[...]).astype(BF16)
    gate = jax.nn.sigmoid(_dot(xg, pwg_ref[...]) + pbg_ref[...])
    h = h + gate * _dot(p_ref[...].astype(BF16), pwp_ref[...])
    if final:
        h = _rms(h, onorm_ref[...])
    o_ref[...] = h


def _post(h, y, p, wout, fnorm, wg, wu, wd, pnorm, pwg, pbg, pwp, onorm, *, layer, tm, final):
    m, d = h.shape
    rows = lambda width: pl.BlockSpec((tm, width), lambda i: (i, 0))
    resident = lambda t: pl.BlockSpec(t.shape, lambda i: (0, 0), pipeline_mode=pl.Buffered(1))
    of_layer = lambda t: pl.BlockSpec((None,) + t.shape[1:], lambda i: (layer, 0, 0), pipeline_mode=pl.Buffered(1))
    stacked = (fnorm, wg, wu, wd, pnorm, pwg, pbg, pwp)
    return pl.pallas_call(
        functools.partial(_post_kernel, final=final),
        grid=(m // tm,),
        in_specs=[rows(d), rows(y.shape[1]), pl.BlockSpec((None, tm, p.shape[2]), lambda i: (layer, i, 0)),
                  resident(wout)] + [of_layer(t) for t in stacked] + [resident(onorm)],
        out_specs=rows(d),
        out_shape=jax.ShapeDtypeStruct((m, d), F32),
        compiler_params=pltpu.CompilerParams(dimension_semantics=("arbitrary",), vmem_limit_bytes=VMEM_LIMIT),
        name="post_mixer",
    )(h, y, p, wout, *stacked, onorm)


def _block_diag(blocks):
    g, bi, bj = blocks.shape
    eye = jnp.eye(g, dtype=blocks.dtype)
    return (eye[:, None, :, None] * blocks[:, :, None, :]).reshape(g * bi, g * bj)


def kernel(x, p, positions, ev_w_in, ev_conv_w, ev_conv_b, ev_lru_wr, ev_lru_br, ev_lru_wi, ev_lru_bi, ev_lru_a, ev_gla_wgk, ev_gla_bgk, ev_gla_norm, ev_w_out, od_w_in, od_ret_norm, od_rwkv_mu, od_rwkv_w0, od_rwkv_ww2, od_rwkv_a0, od_rwkv_aw2, od_rwkv_gw2, od_rwkv_kk, od_rwkv_ka, od_rwkv_rk, od_rwkv_norm, od_w_out, mix_norm, ffn_norm, ffn_w_gate, ffn_w_up, ffn_w_down, ple_norm, ple_w_gate, ple_b_gate, ple_w_proj, final_norm):
    bsz, seq, d = x.shape
    depth = p.shape[0]
    m = bsz * seq
    row = lambda t: t.reshape(1, -1).astype(F32)
    rows3 = lambda t: t.reshape(t.shape[0], 1, -1).astype(F32)

    wa = ev_conv_w.shape[-1]
    rank = ev_gla_wgk.shape[1]
    wq = ev_gla_wgk.shape[2]
    wb = ev_gla_norm.shape[-1]
    dk, dv = wq // GLA_HEADS, wb // GLA_HEADS
    wc = od_ret_norm.shape[-1]
    wd = od_rwkv_w0.shape[-1]
    rw, ra, rg = od_rwkv_ww2.shape[1], od_rwkv_aw2.shape[1], od_rwkv_gw2.shape[1]
    nb = MIX_BATCH if bsz % MIX_BATCH == 0 else 1

    h = x.reshape(m, d)
    freq = (ROPE_BASE ** (-jnp.arange(wc // RET_HEADS // 2, dtype=F32) / (wc // RET_HEADS // 2))).reshape(-1, 1)
    pos = positions.astype(F32).reshape(bsz, 1, seq)
    seg = _block_diag(jnp.ones((LANES // RWKV_HEAD_DIM, RWKV_HEAD_DIM, RWKV_HEAD_DIM), BF16))
    assert 2 * RWKV_HEAD_DIM == LANES and wd % LANES == 0
    assert seq % EVEN_TILE == 0 and seq % ODD_TILE == 0 and m % POST_ROWS == 0

    for i in range(depth):
        j = i // 2
        if i % 2 == 0:
            w_in = ev_w_in[j]
            cols = w_in.shape[1]
            cols_pad = -(-cols // LANES) * LANES
            w_in = jnp.pad(w_in, ((0, 0), (0, cols_pad - cols))).astype(BF16)
            wgate = jnp.concatenate([_block_diag(ev_lru_wr[j]), _block_diag(ev_lru_wi[j])], axis=1).astype(BF16)
            bgate = jnp.concatenate([ev_lru_br[j], ev_lru_bi[j]]).reshape(1, -1)
            y = _even_mixer(h.reshape(bsz, seq, d), row(mix_norm[i]), w_in, ev_conv_w[j], row(ev_conv_b[j]), wgate, bgate,
                            row(ev_lru_a[j]), ev_gla_wgk[j], row(ev_gla_bgk[j]), row(ev_gla_norm[j]),
                            nb=nb, ts=EVEN_TILE, wa=wa, dk=dk, dv=dv, rank=rank)
            w_out = ev_w_out[j]
        else:
            y = _odd_mixer(h.reshape(bsz, seq, d), pos, row(mix_norm[i]), od_w_in[j].astype(BF16), freq,
                           row(od_ret_norm[j]), row(od_rwkv_mu[j]),
                           row(od_rwkv_w0[j]), od_rwkv_ww2[j].astype(BF16), row(od_rwkv_a0[j]),
                           od_rwkv_aw2[j].astype(BF16), od_rwkv_gw2[j].astype(BF16), row(od_rwkv_kk[j]),
                           row(od_rwkv_ka[j]), row(od_rwkv_rk[j]), row(od_rwkv_norm[j]), seg,
                           nb=nb, ts=ODD_TILE, wc=wc, wd=wd, rw=rw, ra=ra, rg=rg)
            w_out = od_w_out[j]
        h = _post(h, y.reshape(m, -1), p.reshape(depth, m, -1), w_out.astype(BF16), rows3(ffn_norm),
                  ffn_w_gate.astype(BF16), ffn_w_up.astype(BF16), ffn_w_down.astype(BF16),
                  rows3(ple_norm), ple_w_gate.astype(BF16), rows3(ple_b_gate), ple_w_proj.astype(BF16),
                  row(final_norm), layer=i, tm=POST_ROWS, final=(i == depth - 1))
    return h.reshape(bsz, seq, d)
```

```python
import functools
import math

import jax
import jax.numpy as jnp
from jax import lax
from jax.experimental import pallas as pl
from jax.experimental.pallas import tpu as pltpu

F32 = jnp.float32
BF16 = jnp.bfloat16

NORM_EPS = 1e-6
LANES = 128
SUBLANES = 8
CONV_W = 4
LRU_C = 8.0
GLA_HEADS = 4
GLA_GATE_NORM = 16.0
GLA_SUB = 16
RET_HEADS = 4
RET_CHUNK = 128
ROPE_BASE = 10000.0
RWKV_HEAD_DIM = 64
RWKV_LN_EPS = 64e-5
HEAD_NORM_EPS = 1e-5

MIX_CHUNK = 64
MIX_BATCH = 4
EVEN_TILE = 4 * MIX_CHUNK
ODD_TILE = RET_CHUNK
POST_ROWS = 512
VMEM_LIMIT = 56 * 1024 * 1024


def _log_sigmoid(x):
    return jnp.minimum(x, 0.0) - jnp.log1p(jnp.exp(-jnp.abs(x)))


def _softplus(x):
    return jnp.maximum(x, 0.0) + jnp.log1p(jnp.exp(-jnp.abs(x)))


def _silu(x):
    return x * jax.nn.sigmoid(x)


def _gelu_tanh(x):
    return 0.5 * x * (1.0 + jnp.tanh(math.sqrt(2.0 / math.pi) * (x + 0.044715 * (x * x * x))))


def _rms(x, g):
    return x * lax.rsqrt(jnp.mean(x * x, axis=-1, keepdims=True) + NORM_EPS) * g


def _dot(a, b, precision=None):
    return jnp.dot(a, b, preferred_element_type=F32, precision=precision)


def _dot_nt(a, b, precision=None):
    return lax.dot_general(a, b, (((1,), (1,)), ((), ())), preferred_element_type=F32, precision=precision)


def _dot_tn(a, b, precision=None):
    return lax.dot_general(a, b, (((0,), (0,)), ((), ())), preferred_element_type=F32, precision=precision)


def _head_sums(x, seg):
    return jnp.concatenate([_dot(x[:, g:g + LANES].astype(BF16), seg) for g in range(0, x.shape[1], LANES)], axis=-1)


def _dot_split_rhs(m, x, passes):
    acc = None
    for _ in range(passes):
        piece = x.astype(BF16)
        acc = _dot(m, piece) if acc is None else acc + _dot(m, piece)
        x = x - piece.astype(F32)
    return acc


def _tri(n, strict=False):
    row = lax.broadcasted_iota(jnp.int32, (n, n), 0)
    col = lax.broadcasted_iota(jnp.int32, (n, n), 1)
    return (row > col) if strict else (row >= col)


PROJ_GROUP = 512


def _project(h_ref, mnorm_ref, win_ref, proj_ref):
    nb, ts, d = h_ref.shape
    xn = _rms(h_ref[...].reshape(nb * ts, d), mnorm_ref[...]).astype(BF16)
    cols = proj_ref.shape[2]
    for c0 in range(0, cols, PROJ_GROUP):
        c1 = min(c0 + PROJ_GROUP, cols)
        proj_ref[:, :, c0:c1] = _dot(xn, win_ref[:, c0:c1]).reshape(nb, ts, c1 - c0)


def _resident(t):
    return pl.BlockSpec(t.shape, lambda b, s: (0,) * t.ndim, pipeline_mode=pl.Buffered(1))


def _rg_lru(proj_ref, y_ref, xtail_ref, hcar_ref, convw_ref, convb_ref, wgate_ref, bgate_ref, lrua_ref, *, nb, ts, wa):
    cw = convw_ref[...]
    xcs = []
    for b in range(nb):
        ax = proj_ref[b, :, 0:wa]
        xcat = jnp.concatenate([xtail_ref[b], ax], axis=0)
        xtail_ref[b] = ax[ts - 8:ts]
        xc = convb_ref[...] + cw[CONV_W - 1:CONV_W] * ax
        for s in range(1, CONV_W):
            xc = xc + cw[CONV_W - 1 - s:CONV_W - s] * xcat[8 - s:8 - s + ts]
        xcs.append(xc)
    gates_all = _dot(jnp.concatenate(xcs, axis=0).astype(BF16), wgate_ref[...]) + bgate_ref[...]
    log_sig_a = _log_sigmoid(lrua_ref[...])
    groups = ts // SUBLANES
    row = lax.broadcasted_iota(jnp.int32, (groups, SUBLANES, wa), 1)
    for b in range(nb):
        xc = xcs[b]
        gates = gates_all[b * ts:(b + 1) * ts]
        r = jax.nn.sigmoid(gates[:, :wa])
        i = jax.nn.sigmoid(gates[:, wa:])
        log_a = LRU_C * r * log_sig_a
        a = jnp.exp(log_a)
        t = jnp.tanh(log_a)
        u = jnp.sqrt(-2.0 * t / (1.0 - t)) * (i * xc)
        u = u.reshape(groups, SUBLANES, wa)
        a = a.reshape(groups, SUBLANES, wa)
        d = 1
        while d < SUBLANES:
            keep = row >= d
            u = jnp.where(keep, a * pltpu.roll(u, d, 1) + u, u)
            a = jnp.where(keep, a * pltpu.roll(a, d, 1), a)
            d *= 2
        carry = hcar_ref[b]
        pieces = []
        for j in range(groups):
            hj = u[j] + a[j] * carry
            carry = hj[SUBLANES - 1:SUBLANES]
            pieces.append(hj)
        hcar_ref[b] = carry
        h = jnp.concatenate(pieces, axis=0)
        y_ref[b, :, 0:wa] = (h * _gelu_tanh(proj_ref[b, :, wa:2 * wa])).astype(y_ref.dtype)


def _even_mixer_kernel(h_ref, mnorm_ref, win_ref, convw_ref, convb_ref, wgate_ref, bgate_ref, lrua_ref,
                       wgk_ref, bgk_ref, gnorm_ref, y_ref, proj_ref, xtail_ref, hcar_ref, st_ref,
                       *, nb, ts, wa, dk, dv, rank):
    heads = GLA_HEADS
    wq = heads * dk
    wb = heads * dv
    o_q = 2 * wa
    o_k, o_v = o_q + wq, o_q + 2 * wq
    o_g, o_gk = o_v + wb, o_v + 2 * wb
    step = pl.program_id(1)

    @pl.when(step == 0)
    def _():
        xtail_ref[...] = jnp.zeros_like(xtail_ref)
        hcar_ref[...] = jnp.zeros_like(hcar_ref)
        st_ref[...] = jnp.zeros_like(st_ref)

    _project(h_ref, mnorm_ref, win_ref, proj_ref)
    _rg_lru(proj_ref, y_ref, xtail_ref, hcar_ref, convw_ref, convb_ref, wgate_ref, bgate_ref, lrua_ref,
            nb=nb, ts=ts, wa=wa)

    cg = MIX_CHUNK
    tri_incl = jnp.where(_tri(cg), 1.0, 0.0).astype(BF16)
    causal = _tri(cg)
    rowc = lax.broadcasted_iota(jnp.int32, (cg, wq), 0)
    items = []
    gk = proj_ref[:, :, o_gk:o_gk + rank].reshape(nb * ts, rank)
    gk_hi = gk.astype(BF16)
    gk_lo = (gk - gk_hi.astype(F32)).astype(BF16)
    w_hi = wgk_ref[...].astype(BF16)
    w_lo = (wgk_ref[...] - w_hi.astype(F32)).astype(BF16)
    z_all = _dot(gk_hi, w_hi) + (_dot(gk_lo, w_hi) + _dot(gk_hi, w_lo)) + bgk_ref[...]
    log_f_all = _log_sigmoid(z_all) * (1.0 / GLA_GATE_NORM)
    for b in range(nb):
        for c in range(ts // cg):
            r0 = c * cg
            cum = _dot_split_rhs(tri_incl, log_f_all[b * ts + r0:b * ts + r0 + cg], 2)
            q = proj_ref[b, r0:r0 + cg, o_q:o_q + wq] * (dk ** -0.5)
            k = proj_ref[b, r0:r0 + cg, o_k:o_k + wq]
            v = proj_ref[b, r0:r0 + cg, o_v:o_v + wb].astype(BF16)
            cum_last = cum[cg - 1:cg]
            it = dict(b=b, c=c, v=v, q_in=(q * jnp.exp(cum)).astype(BF16),
                      k_out=(k * jnp.exp(cum_last - cum)).astype(BF16), chunk_decay=jnp.exp(cum_last), qb=[], kb=[])
            for ib in range(cg // GLA_SUB):
                b0 = ib * GLA_SUB
                base = cum[b0 - 1:b0] if ib > 0 else jnp.zeros((1, wq), F32)
                it["qb"].append((q[b0:b0 + GLA_SUB] * jnp.exp(cum[b0:b0 + GLA_SUB] - base)).astype(BF16))
                it["kb"].append(jnp.where(rowc < b0 + GLA_SUB, k * jnp.exp(base - cum), 0.0).astype(BF16))
            items.append(it)
    for it in items:
        it["scores"] = [
            jnp.where(causal, jnp.concatenate(
                [_dot_nt(qb[:, hd * dk:(hd + 1) * dk], kb[:, hd * dk:(hd + 1) * dk])
                 for qb, kb in zip(it["qb"], it["kb"])], axis=0), 0.0).astype(BF16)
            for hd in range(heads)]
    for it in items:
        it["intra"] = [_dot(it["scores"][hd], it["v"][:, hd * dv:(hd + 1) * dv]) for hd in range(heads)]
    for c in range(ts // cg):
        r0 = c * cg
        for it in [t for t in items if t["c"] == c]:
            b = it["b"]
            outs = []
            for hd in range(heads):
                ks = slice(hd * dk, (hd + 1) * dk)
                vs = slice(hd * dv, (hd + 1) * dv)
                st = st_ref[b * heads + hd]
                o = it["intra"][hd] + _dot_nt(it["q_in"][:, ks], st.astype(BF16))
                st_ref[b * heads + hd] = st * it["chunk_decay"][:, ks] + _dot_tn(it["v"][:, vs], it["k_out"][:, ks])
                outs.append(o * lax.rsqrt(jnp.mean(o * o, axis=-1, keepdims=True) + HEAD_NORM_EPS))
            o_all = jnp.concatenate(outs, axis=-1) * gnorm_ref[...]
            g = proj_ref[b, r0:r0 + cg, o_g:o_g + wb]
            y_ref[b, r0:r0 + cg, wa:wa + wb] = (o_all * _silu(g)).astype(y_ref.dtype)


def _even_mixer(h, mnorm, win, convw, convb, wgate, bgate, lrua, wgk, bgk, gnorm, *, nb, ts, wa, dk, dv, rank):
    bsz, seq, d = h.shape
    cols = win.shape[1]
    heads = GLA_HEADS
    wb = heads * dv
    consts = (mnorm, win, convw, convb, wgate, bgate, lrua, wgk, bgk, gnorm)
    kern = functools.partial(_even_mixer_kernel, nb=nb, ts=ts, wa=wa, dk=dk, dv=dv, rank=rank)
    return pl.pallas_call(
        kern,
        grid=(bsz // nb, seq // ts),
        in_specs=[pl.BlockSpec((nb, ts, d), lambda b, s: (b, s, 0))] + [_resident(t) for t in consts],
        out_specs=pl.BlockSpec((nb, ts, wa + wb), lambda b, s: (b, s, 0)),
        out_shape=jax.ShapeDtypeStruct((bsz, seq, wa + wb), BF16),
        scratch_shapes=[pltpu.VMEM((nb, ts, cols), F32), pltpu.VMEM((nb, 8, wa), F32),
                        pltpu.VMEM((nb, 1, wa), F32), pltpu.VMEM((nb * heads, dv, dk), F32)],
        compiler_params=pltpu.CompilerParams(dimension_semantics=("arbitrary", "arbitrary"),
                                             vmem_limit_bytes=VMEM_LIMIT),
        name="even_mixer",
    )(h, *consts)


def _odd_mixer_kernel(h_ref, pos_ref, mnorm_ref, win_ref, freq_ref, rnorm_ref, mu_ref, w0_ref, ww2_ref, a0_ref,
                      aw2_ref, gw2_ref, kk_ref, ka_ref, rk_ref, wnorm_ref, seg_ref,
                      y_ref, proj_ref, rstate_ref, dprev_ref, wstate_ref, *, nb, ts, wc, wd, rw, ra, rg):
    f32 = F32

    @pl.when(pl.program_id(1) == 0)
    def _():
        rstate_ref[...] = jnp.zeros_like(rstate_ref)
        dprev_ref[...] = jnp.zeros_like(dprev_ref)
        wstate_ref[...] = jnp.zeros_like(wstate_ref)

    _project(h_ref, mnorm_ref, win_ref, proj_ref)

    o_d = 4 * wc
    dcols = 3 * wd + rw + ra + rg
    hdim = RWKV_HEAD_DIM
    cm = MIX_CHUNK
    n = 2 * cm
    groups = wd // LANES
    seg = seg_ref[...]
    lane = lax.broadcasted_iota(jnp.int32, (cm, LANES), 1)
    first_head = lane < hdim

    def stack_heads(t):
        return jnp.concatenate([jnp.where(first_head, t, 0.0), jnp.where(first_head, 0.0, t)], axis=0)

    rin = lax.broadcasted_iota(jnp.int32, (n, n), 0)
    cin = lax.broadcasted_iota(jnp.int32, (n, n), 1)
    strict = (rin & (cm - 1)) > (cin & (cm - 1))
    incl = (rin & (cm - 1)) >= (cin & (cm - 1))
    eye = jnp.where(rin == cin, 1.0, 0.0).astype(f32)
    tri_incl = jnp.where(_tri(cm), 1.0, 0.0).astype(BF16)

    work = []
    dparts = []
    for b in range(nb):
        dpart = proj_ref[b, :, o_d:o_d + dcols]
        dshift = jnp.concatenate([dprev_ref[b], dpart], axis=0)[7:7 + ts]
        dprev_ref[b] = dpart[ts - 8:ts]
        dparts.append(dpart + mu_ref[...] * (dshift - dpart))
    dall = jnp.concatenate(dparts, axis=0)
    r_all = dall[:, 0:wd]
    k_all = dall[:, wd:2 * wd]
    v_all = dall[:, 2 * wd:3 * wd]
    w_lr = dall[:, 3 * wd:3 * wd + rw]
    a_lr = dall[:, 3 * wd + rw:3 * wd + rw + ra]
    g_lr = dall[:, 3 * wd + rw + ra:dcols]
    w_all = -_softplus(-(w0_ref[...] + _dot(jnp.tanh(w_lr).astype(BF16), ww2_ref[...]))) - 0.5
    log_decay_all = -jnp.exp(w_all)
    a_all = jax.nn.sigmoid(a0_ref[...] + _dot(a_lr.astype(BF16), aw2_ref[...]))
    g_all = _dot(jax.nn.sigmoid(g_lr).astype(BF16), gw2_ref[...])
    kk_all = k_all * kk_ref[...]
    k_all = k_all * (1.0 + (a_all - 1.0) * ka_ref[...])
    rows_all = nb * ts
    sums = _head_sums(jnp.concatenate([kk_all * kk_all, r_all * k_all * rk_ref[...]], axis=0), seg)
    kk_all = kk_all * lax.rsqrt(sums[:rows_all] + 1e-12)
    bonus_all = sums[rows_all:] * v_all
    kb_all = kk_all * a_all
    for b in range(nb):
        for ch in range(ts // cm):
            rs = slice(b * ts + ch * cm, b * ts + (ch + 1) * cm)
            ld = log_decay_all[rs]
            cum = _dot_split_rhs(tri_incl, ld, 2)
            cum_last = cum[cm - 1:cm]
            p_incl = jnp.exp(cum)
            p_inv = jnp.exp(-cum)
            a_t = -kk_all[rs] * jnp.exp(cum - ld)
            r_t = r_all[rs] * p_incl
            b_t = kb_all[rs] * p_inv
            k_t = k_all[rs] * p_inv
            p_out = jnp.exp(cum_last - cum)
            b_o = kb_all[rs] * p_out
            k_o = k_all[rs] * p_out
            p_last = jnp.exp(cum_last)
            for pr in range(groups):
                ls = slice(pr * LANES, (pr + 1) * LANES)
                l4 = jnp.concatenate([stack_heads(a_t[:, ls]), stack_heads(r_t[:, ls])], axis=0).astype(BF16)
                r4 = jnp.concatenate([stack_heads(b_t[:, ls]), stack_heads(k_t[:, ls])], axis=0).astype(BF16)
                o4 = jnp.concatenate([stack_heads(b_o[:, ls]), stack_heads(k_o[:, ls])], axis=0).astype(BF16)
                vbd = stack_heads(v_all[rs, ls]).astype(BF16)
                work.append(dict(b=b, ch=ch, pr=pr, l4=l4, r4=r4, o4=o4, vbd=vbd, p_last=p_last[:, ls]))
    for wk in work:
        big = _dot_nt(wk["l4"], wk["r4"])
        a_ab = jnp.where(strict, big[:n, :n], 0.0)
        wk["a_ak"] = jnp.where(strict, big[:n, n:], 0.0).astype(BF16)
        wk["a_r"] = jnp.concatenate([jnp.where(incl, big[n:, :n], 0.0), jnp.where(incl, big[n:, n:], 0.0)],
                                    axis=1).astype(BF16)
        wk["x"] = a_ab.astype(BF16)
        wk["tinv"] = eye + a_ab
    span = 2
    while span < cm:
        for wk in work:
            wk["x"] = _dot(wk["x"], wk["x"]).astype(BF16)
        for wk in work:
            wk["tinv"] = wk["tinv"] + _dot(wk["tinv"].astype(BF16), wk["x"])
        span *= 2

    c = ts
    dh = wc // RET_HEADS
    half = dh // 2
    rowi = lax.broadcasted_iota(jnp.int32, (c, c), 0)
    coli = lax.broadcasted_iota(jnp.int32, (c, c), 1)
    rel = (rowi - coli).astype(f32)
    rowd = lax.broadcasted_iota(jnp.int32, (c, dh), 0).astype(f32)
    ret = []
    for b in range(nb):
        ang = freq_ref[...] * pos_ref[b]
        cos, sin = jnp.cos(ang).T, jnp.sin(ang).T
        cos2 = jnp.concatenate([cos, cos], axis=-1)
        sin2 = jnp.concatenate([-sin, sin], axis=-1)
        for hd in range(RET_HEADS):
            log_gamma = math.log1p(-2.0 ** (-5.0 - hd))
            q = proj_ref[b, :, hd * dh:(hd + 1) * dh]
            k = proj_ref[b, :, wc + hd * dh:wc + (hd + 1) * dh]
            v = proj_ref[b, :, 2 * wc + hd * dh:2 * wc + (hd + 1) * dh].astype(BF16)
            q = (q * cos2 + pltpu.roll(q, half, 1) * sin2) * (dh ** -0.5)
            k = k * cos2 + pltpu.roll(k, half, 1) * sin2
            dmask = jnp.where(rel >= 0.0, jnp.exp(jnp.maximum(rel, 0.0) * log_gamma), 0.0)
            ret.append(dict(b=b, hd=hd, v=v, q=q.astype(BF16), k=k.astype(BF16), dmask=dmask,
                            q_in=(q * jnp.exp((rowd + 1.0) * log_gamma)).astype(BF16),
                            k_out=(k * jnp.exp((c - 1.0 - rowd) * log_gamma)).astype(BF16),
                            decay=math.exp(c * log_gamma)))
    for rt in ret:
        rt["scores"] = (_dot_nt(rt["q"], rt["k"]) * rt["dmask"]).astype(BF16)
    for rt in ret:
        st = rstate_ref[rt["b"] * RET_HEADS + rt["hd"]]
        o = _dot(rt["scores"], rt["v"]) + _dot(rt["q_in"], st.astype(BF16))
        rstate_ref[rt["b"] * RET_HEADS + rt["hd"]] = st * rt["decay"] + _dot_tn(rt["k_out"], rt["v"])
        o = o - jnp.mean(o, axis=-1, keepdims=True)
        rt["o"] = o * lax.rsqrt(jnp.mean(o * o, axis=-1, keepdims=True) + HEAD_NORM_EPS)
    for b in range(nb):
        g = proj_ref[b, :, 3 * wc:4 * wc]
        o_all = jnp.concatenate([rt["o"] for rt in ret if rt["b"] == b], axis=-1)
        y_ref[b, :, 0:wc] = (o_all * rnorm_ref[...] * _silu(g)).astype(y_ref.dtype)

    for ch in range(ts // cm):
        wks = [wk for wk in work if wk["ch"] == ch]
        sidx = [wk["b"] * groups + wk["pr"] for wk in wks]
        sts = [wstate_ref[i] for i in sidx]
        fss = [_dot_nt(wk["l4"], st.astype(BF16)) for wk, st in zip(wks, sts)]
        rhs = [(fs[:n] + _dot(wk["a_ak"], wk["vbd"])).astype(BF16) for wk, fs in zip(wks, fss)]
        us = [_dot(wk["tinv"].astype(BF16), rh) for wk, rh in zip(wks, rhs)]
        uvs = [jnp.concatenate([u.astype(BF16), wk["vbd"]], axis=0) for wk, u in zip(wks, us)]
        for i, st, wk, uv in zip(sidx, sts, wks, uvs):
            wstate_ref[i] = st * wk["p_last"] + _dot_tn(uv, wk["o4"])
        ybds = [fs[n:] + _dot(wk["a_r"], uv) for wk, fs, uv in zip(wks, fss, uvs)]
        y = jnp.concatenate(
            [jnp.concatenate([yb[:cm] + yb[cm:] for wk, yb in zip(wks, ybds) if wk["b"] == b], axis=-1)
             for b in range(nb)], axis=0)
        inv_h = 1.0 / hdim
        y = y - _head_sums(y, seg) * inv_h
        y = y * lax.rsqrt(_head_sums(y * y, seg) * inv_h + RWKV_LN_EPS)
        for b in range(nb):
            tile_rows = slice(b * ts + ch * cm, b * ts + (ch + 1) * cm)
            yb = (y[b * cm:(b + 1) * cm] * wnorm_ref[...] + bonus_all[tile_rows]) * g_all[tile_rows]
            y_ref[b, ch * cm:(ch + 1) * cm, wc:wc + wd] = yb.astype(y_ref.dtype)


def _odd_mixer(h, pos, mnorm, win, freq, rnorm, mu, w0, ww2, a0, aw2, gw2, kk, ka, rk, wnorm, seg, *, nb, ts, wc, wd, rw,
               ra, rg):
    bsz, seq, d = h.shape
    cols = win.shape[1]
    dh = wc // RET_HEADS
    consts = (mnorm, win, freq, rnorm, mu, w0, ww2, a0, aw2, gw2, kk, ka, rk, wnorm, seg)
    kern = functools.partial(_odd_mixer_kernel, nb=nb, ts=ts, wc=wc, wd=wd, rw=rw, ra=ra, rg=rg)
    dcols = 3 * wd + rw + ra + rg
    return pl.pallas_call(
        kern,
        grid=(bsz // nb, seq // ts),
        in_specs=[pl.BlockSpec((nb, ts, d), lambda b, s: (b, s, 0)),
                  pl.BlockSpec((nb, 1, ts), lambda b, s: (b, 0, s))] + [_resident(t) for t in consts],
        out_specs=pl.BlockSpec((nb, ts, wc + wd), lambda b, s: (b, s, 0)),
        out_shape=jax.ShapeDtypeStruct((bsz, seq, wc + wd), BF16),
        scratch_shapes=[pltpu.VMEM((nb, ts, cols), F32), pltpu.VMEM((nb * RET_HEADS, dh, dh), F32),
                        pltpu.VMEM((nb, 8, dcols), F32), pltpu.VMEM((nb * (wd // LANES), LANES, LANES), F32)],
        compiler_params=pltpu.CompilerParams(dimension_semantics=("arbitrary", "arbitrary"),
                                             vmem_limit_bytes=VMEM_LIMIT),
        name="odd_mixer",
    )(h, pos, *consts)


def _post_kernel(h_ref, y_ref, p_ref, wout_ref, fnorm_ref, wg_ref, wu_ref, wd_ref, pnorm_ref, pwg_ref, pbg_ref,
                 pwp_ref, onorm_ref, o_ref, *, final):
    h = h_ref[...] + _dot(y_ref[...], wout_ref[...])
    xn = _rms(h, fnorm_ref[...]).astype(BF16)
    act = (_silu(_dot(xn, wg_ref[...])) * _dot(xn, wu_ref[...])).astype(BF16)
    h = h + _dot(act, wd_ref[...])
    xg = _rms(h, pnorm_ref[...]).astype(BF16)
    gate = jax.nn.sigmoid(_dot(xg, pwg_ref[...]) + pbg_ref[...])
    h = h + gate * _dot(p_ref[...].astype(BF16), pwp_ref[...])
    if final:
        h = _rms(h, onorm_ref[...])
    o_ref[...] = h


def _post(h, y, p, wout, fnorm, wg, wu, wd, pnorm, pwg, pbg, pwp, onorm, *, layer, tm, final):
    m, d = h.shape
    rows = lambda width: pl.BlockSpec((tm, width), lambda i: (i, 0))
    resident = lambda t: pl.BlockSpec(t.shape, lambda i: (0, 0), pipeline_mode=pl.Buffered(1))
    of_layer = lambda t: pl.BlockSpec((None,) + t.shape[1:], lambda i: (layer, 0, 0), pipeline_mode=pl.Buffered(1))
    stacked = (fnorm, wg, wu, wd, pnorm, pwg, pbg, pwp)
    return pl.pallas_call(
        functools.partial(_post_kernel, final=final),
        grid=(m // tm,),
        in_specs=[rows(d), rows(y.shape[1]), pl.BlockSpec((None, tm, p.shape[2]), lambda i: (layer, i, 0)),
                  resident(wout)] + [of_layer(t) for t in stacked] + [resident(onorm)],
        out_specs=rows(d),
        out_shape=jax.ShapeDtypeStruct((m, d), F32),
        compiler_params=pltpu.CompilerParams(dimension_semantics=("arbitrary",), vmem_limit_bytes=VMEM_LIMIT),
        name="post_mixer",
    )(h, y, p, wout, *stacked, onorm)


def _block_diag(blocks):
    g, bi, bj = blocks.shape
    eye = jnp.eye(g, dtype=blocks.dtype)
    return (eye[:, None, :, None] * blocks[:, :, None, :]).reshape(g * bi, g * bj)


def kernel(x, p, positions, ev_w_in, ev_conv_w, ev_conv_b, ev_lru_wr, ev_lru_br, ev_lru_wi, ev_lru_bi, ev_lru_a, ev_gla_wgk, ev_gla_bgk, ev_gla_norm, ev_w_out, od_w_in, od_ret_norm, od_rwkv_mu, od_rwkv_w0, od_rwkv_ww2, od_rwkv_a0, od_rwkv_aw2, od_rwkv_gw2, od_rwkv_kk, od_rwkv_ka, od_rwkv_rk, od_rwkv_norm, od_w_out, mix_norm, ffn_norm, ffn_w_gate, ffn_w_up, ffn_w_down, ple_norm, ple_w_gate, ple_b_gate, ple_w_proj, final_norm):
    bsz, seq, d = x.shape
    depth = p.shape[0]
    m = bsz * seq
    row = lambda t: t.reshape(1, -1).astype(F32)
    rows3 = lambda t: t.reshape(t.shape[0], 1, -1).astype(F32)

    wa = ev_conv_w.shape[-1]
    rank = ev_gla_wgk.shape[1]
    wq = ev_gla_wgk.shape[2]
    wb = ev_gla_norm.shape[-1]
    dk, dv = wq // GLA_HEADS, wb // GLA_HEADS
    wc = od_ret_norm.shape[-1]
    wd = od_rwkv_w0.shape[-1]
    rw, ra, rg = od_rwkv_ww2.shape[1], od_rwkv_aw2.shape[1], od_rwkv_gw2.shape[1]
    nb = MIX_BATCH if bsz % MIX_BATCH == 0 else 1

    h = x.reshape(m, d)
    freq = (ROPE_BASE ** (-jnp.arange(wc // RET_HEADS // 2, dtype=F32) / (wc // RET_HEADS // 2))).reshape(-1, 1)
    pos = positions.astype(F32).reshape(bsz, 1, seq)
    seg = _block_diag(jnp.ones((LANES // RWKV_HEAD_DIM, RWKV_HEAD_DIM, RWKV_HEAD_DIM), BF16))
    assert 2 * RWKV_HEAD_DIM == LANES and wd % LANES == 0
    assert seq % EVEN_TILE == 0 and seq % ODD_TILE == 0 and m % POST_ROWS == 0

    for i in range(depth):
        j = i // 2
        if i % 2 == 0:
            w_in = ev_w_in[j]
            cols = w_in.shape[1]
            cols_pad = -(-cols // LANES) * LANES
            w_in = jnp.pad(w_in, ((0, 0), (0, cols_pad - cols))).astype(BF16)
            wgate = jnp.concatenate([_block_diag(ev_lru_wr[j]), _block_diag(ev_lru_wi[j])], axis=1).astype(BF16)
            bgate = jnp.concatenate([ev_lru_br[j], ev_lru_bi[j]]).reshape(1, -1)
            y = _even_mixer(h.reshape(bsz, seq, d), row(mix_norm[i]), w_in, ev_conv_w[j], row(ev_conv_b[j]), wgate, bgate,
                            row(ev_lru_a[j]), ev_gla_wgk[j], row(ev_gla_bgk[j]), row(ev_gla_norm[j]),
                            nb=nb, ts=EVEN_TILE, wa=wa, dk=dk, dv=dv, rank=rank)
            w_out = ev_w_out[j]
        else:
            y = _odd_mixer(h.reshape(bsz, seq, d), pos, row(mix_norm[i]), od_w_in[j].astype(BF16), freq,
                           row(od_ret_norm[j]), row(od_rwkv_mu[j]),
                           row(od_rwkv_w0[j]), od_rwkv_ww2[j].astype(BF16), row(od_rwkv_a0[j]),
                           od_rwkv_aw2[j].astype(BF16), od_rwkv_gw2[j].astype(BF16), row(od_rwkv_kk[j]),
                           row(od_rwkv_ka[j]), row(od_rwkv_rk[j]), row(od_rwkv_norm[j]), seg,
                           nb=nb, ts=ODD_TILE, wc=wc, wd=wd, rw=rw, ra=ra, rg=rg)
            w_out = od_w_out[j]
        h = _post(h, y.reshape(m, -1), p.reshape(depth, m, -1), w_out.astype(BF16), rows3(ffn_norm),
                  ffn_w_gate.astype(BF16), ffn_w_up.astype(BF16), ffn_w_down.astype(BF16),
                  rows3(ple_norm), ple_w_gate.astype(BF16), rows3(ple_b_gate), ple_w_proj.astype(BF16),
                  row(final_norm), layer=i, tm=POST_ROWS, final=(i == depth - 1))
    return h.reshape(bsz, seq, d)
```

```python
import functools
import math

import jax
import jax.numpy as jnp
from jax import lax
from jax.experimental import pallas as pl
from jax.experimental.pallas import tpu as pltpu

F32 = jnp.float32
BF16 = jnp.bfloat16

NORM_EPS = 1e-6
LANES = 128
SUBLANES = 8
CONV_W = 4
LRU_C = 8.0
GLA_HEADS = 4
GLA_GATE_NORM = 16.0
GLA_SUB = 16
RET_HEADS = 4
RET_CHUNK = 128
ROPE_BASE = 10000.0
RWKV_HEAD_DIM = 64
RWKV_LN_EPS = 64e-5
HEAD_NORM_EPS = 1e-5

MIX_CHUNK = 64
MIX_BATCH = 4
EVEN_TILE = 4 * MIX_CHUNK
ODD_TILE = RET_CHUNK
POST_ROWS = 512
VMEM_LIMIT = 56 * 1024 * 1024


def _log_sigmoid(x):
    return jnp.minimum(x, 0.0) - jnp.log1p(jnp.exp(-jnp.abs(x)))


def _softplus(x):
    return jnp.maximum(x, 0.0) + jnp.log1p(jnp.exp(-jnp.abs(x)))


def _silu(x):
    return x * jax.nn.sigmoid(x)


def _gelu_tanh(x):
    return 0.5 * x * (1.0 + jnp.tanh(math.sqrt(2.0 / math.pi) * (x + 0.044715 * (x * x * x))))


def _rms(x, g):
    return x * lax.rsqrt(jnp.mean(x * x, axis=-1, keepdims=True) + NORM_EPS) * g


def _dot(a, b, precision=None):
    return jnp.dot(a, b, preferred_element_type=F32, precision=precision)


def _dot_nt(a, b, precision=None):
    return lax.dot_general(a, b, (((1,), (1,)), ((), ())), preferred_element_type=F32, precision=precision)


def _dot_tn(a, b, precision=None):
    return lax.dot_general(a, b, (((0,), (0,)), ((), ())), preferred_element_type=F32, precision=precision)


def _head_sums(x, seg):
    return jnp.concatenate([_dot(x[:, g:g + LANES].astype(BF16), seg) for g in range(0, x.shape[1], LANES)], axis=-1)


def _dot_split_rhs(m, x, passes):
    acc = None
    for _ in range(passes):
        piece = x.astype(BF16)
        acc = _dot(m, piece) if acc is None else acc + _dot(m, piece)
        x = x - piece.astype(F32)
    return acc


def _tri(n, strict=False):
    row = lax.broadcasted_iota(jnp.int32, (n, n), 0)
    col = lax.broadcasted_iota(jnp.int32, (n, n), 1)
    return (row > col) if strict else (row >= col)


PROJ_GROUP = 512


def _project(h_ref, mnorm_ref, win_ref, proj_ref):
    nb, ts, d = h_ref.shape
    xn = _rms(h_ref[...].reshape(nb * ts, d), mnorm_ref[...]).astype(BF16)
    cols = proj_ref.shape[2]
    for c0 in range(0, cols, PROJ_GROUP):
        c1 = min(c0 + PROJ_GROUP, cols)
        proj_ref[:, :, c0:c1] = _dot(xn, win_ref[:, c0:c1]).reshape(nb, ts, c1 - c0)


def _resident(t):
    return pl.BlockSpec(t.shape, lambda b, s: (0,) * t.ndim, pipeline_mode=pl.Buffered(1))


def _rg_lru(proj_ref, y_ref, xtail_ref, hcar_ref, convw_ref, convb_ref, wgate_ref, bgate_ref, lrua_ref, *, nb, ts, wa):
    cw = convw_ref[...]
    xcs = []
    for b in range(nb):
        ax = proj_ref[b, :, 0:wa]
        xcat = jnp.concatenate([xtail_ref[b], ax], axis=0)
        xtail_ref[b] = ax[ts - 8:ts]
        xc = convb_ref[...] + cw[CONV_W - 1:CONV_W] * ax
        for s in range(1, CONV_W):
            xc = xc + cw[CONV_W - 1 - s:CONV_W - s] * xcat[8 - s:8 - s + ts]
        xcs.append(xc)
    gates_all = _dot(jnp.concatenate(xcs, axis=0).astype(BF16), wgate_ref[...]) + bgate_ref[...]
    log_sig_a = _log_sigmoid(lrua_ref[...])
    groups = ts // SUBLANES
    row = lax.broadcasted_iota(jnp.int32, (groups, SUBLANES, wa), 1)
    for b in range(nb):
        xc = xcs[b]
        gates = gates_all[b * ts:(b + 1) * ts]
        r = jax.nn.sigmoid(gates[:, :wa])
        i = jax.nn.sigmoid(gates[:, wa:])
        log_a = LRU_C * r * log_sig_a
        a = jnp.exp(log_a)
        t = jnp.tanh(log_a)
        u = jnp.sqrt(-2.0 * t / (1.0 - t)) * (i * xc)
        u = u.reshape(groups, SUBLANES, wa)
        a = a.reshape(groups, SUBLANES, wa)
        d = 1
        while d < SUBLANES:
            keep = row >= d
            u = jnp.where(keep, a * pltpu.roll(u, d, 1) + u, u)
            a = jnp.where(keep, a * pltpu.roll(a, d, 1), a)
            d *= 2
        carry = hcar_ref[b]
        pieces = []
        for j in range(groups):
            hj = u[j] + a[j] * carry
            carry = hj[SUBLANES - 1:SUBLANES]
            pieces.append(hj)
        hcar_ref[b] = carry
        h = jnp.concatenate(pieces, axis=0)
        y_ref[b, :, 0:wa] = (h * _gelu_tanh(proj_ref[b, :, wa:2 * wa])).astype(y_ref.dtype)


def _even_mixer_kernel(h_ref, mnorm_ref, win_ref, convw_ref, convb_ref, wgate_ref, bgate_ref, lrua_ref,
                       wgk_ref, bgk_ref, gnorm_ref, y_ref, proj_ref, xtail_ref, hcar_ref, st_ref,
                       *, nb, ts, wa, dk, dv, rank):
    heads = GLA_HEADS
    wq = heads * dk
    wb = heads * dv
    o_q = 2 * wa
    o_k, o_v = o_q + wq, o_q + 2 * wq
    o_g, o_gk = o_v + wb, o_v + 2 * wb
    step = pl.program_id(1)

    @pl.when(step == 0)
    def _():
        xtail_ref[...] = jnp.zeros_like(xtail_ref)
        hcar_ref[...] = jnp.zeros_like(hcar_ref)
        st_ref[...] = jnp.zeros_like(st_ref)

    _project(h_ref, mnorm_ref, win_ref, proj_ref)
    _rg_lru(proj_ref, y_ref, xtail_ref, hcar_ref, convw_ref, convb_ref, wgate_ref, bgate_ref, lrua_ref,
            nb=nb, ts=ts, wa=wa)

    cg = MIX_CHUNK
    tri_incl = jnp.where(_tri(cg), 1.0, 0.0).astype(BF16)
    causal = _tri(cg)
    rowc = lax.broadcasted_iota(jnp.int32, (cg, wq), 0)
    items = []
    gk = proj_ref[:, :, o_gk:o_gk + rank].reshape(nb * ts, rank)
    gk_hi = gk.astype(BF16)
    gk_lo = (gk - gk_hi.astype(F32)).astype(BF16)
    w_hi = wgk_ref[...].astype(BF16)
    w_lo = (wgk_ref[...] - w_hi.astype(F32)).astype(BF16)
    z_all = _dot(gk_hi, w_hi) + (_dot(gk_lo, w_hi) + _dot(gk_hi, w_lo)) + bgk_ref[...]
    log_f_all = _log_sigmoid(z_all) * (1.0 / GLA_GATE_NORM)
    for b in range(nb):
        for c in range(ts // cg):
            r0 = c * cg
            cum = _dot_split_rhs(tri_incl, log_f_all[b * ts + r0:b * ts + r0 + cg], 2)
            q = proj_ref[b, r0:r0 + cg, o_q:o_q + wq] * (dk ** -0.5)
            k = proj_ref[b, r0:r0 + cg, o_k:o_k + wq]
            v = proj_ref[b, r0:r0 + cg, o_v:o_v + wb].astype(BF16)
            cum_last = cum[cg - 1:cg]
            it = dict(b=b, c=c, v=v, q_in=(q * jnp.exp(cum)).astype(BF16),
                      k_out=(k * jnp.exp(cum_last - cum)).astype(BF16), chunk_decay=jnp.exp(cum_last), qb=[], kb=[])
            for ib in range(cg // GLA_SUB):
                b0 = ib * GLA_SUB
                base = cum[b0 - 1:b0] if ib > 0 else jnp.zeros((1, wq), F32)
                it["qb"].append((q[b0:b0 + GLA_SUB] * jnp.exp(cum[b0:b0 + GLA_SUB] - base)).astype(BF16))
                it["kb"].append(jnp.where(rowc < b0 + GLA_SUB, k * jnp.exp(base - cum), 0.0).astype(BF16))
            items.append(it)
    for it in items:
        it["scores"] = [
            jnp.where(causal, jnp.concatenate(
                [_dot_nt(qb[:, hd * dk:(hd + 1) * dk], kb[:, hd * dk:(hd + 1) * dk])
                 for qb, kb in zip(it["qb"], it["kb"])], axis=0), 0.0).astype(BF16)
            for hd in range(heads)]
    for it in items:
        it["intra"] = [_dot(it["scores"][hd], it["v"][:, hd * dv:(hd + 1) * dv]) for hd in range(heads)]
    for c in range(ts // cg):
        r0 = c * cg
        for it in [t for t in items if t["c"] == c]:
            b = it["b"]
            outs = []
            for hd in range(heads):
                ks = slice(hd * dk, (hd + 1) * dk)
                vs = slice(hd * dv, (hd + 1) * dv)
                st = st_ref[b * heads + hd]
                o = it["intra"][hd] + _dot_nt(it["q_in"][:, ks], st.astype(BF16))
                st_ref[b * heads + hd] = st * it["chunk_decay"][:, ks] + _dot_tn(it["v"][:, vs], it["k_out"][:, ks])
                outs.append(o * lax.rsqrt(jnp.mean(o * o, axis=-1, keepdims=True) + HEAD_NORM_EPS))
            o_all = jnp.concatenate(outs, axis=-1) * gnorm_ref[...]
            g = proj_ref[b, r0:r0 + cg, o_g:o_g + wb]
            y_ref[b, r0:r0 + cg, wa:wa + wb] = (o_all * _silu(g)).astype(y_ref.dtype)


def _even_mixer(h, mnorm, win, convw, convb, wgate, bgate, lrua, wgk, bgk, gnorm, *, nb, ts, wa, dk, dv, rank):
    bsz, seq, d = h.shape
    cols = win.shape[1]
    heads = GLA_HEADS
    wb = heads * dv
    consts = (mnorm, win, convw, convb, wgate, bgate, lrua, wgk, bgk, gnorm)
    kern = functools.partial(_even_mixer_kernel, nb=nb, ts=ts, wa=wa, dk=dk, dv=dv, rank=rank)
    return pl.pallas_call(
        kern,
        grid=(bsz // nb, seq // ts),
        in_specs=[pl.BlockSpec((nb, ts, d), lambda b, s: (b, s, 0))] + [_resident(t) for t in consts],
        out_specs=pl.BlockSpec((nb, ts, wa + wb), lambda b, s: (b, s, 0)),
        out_shape=jax.ShapeDtypeStruct((bsz, seq, wa + wb), BF16),
        scratch_shapes=[pltpu.VMEM((nb, ts, cols), F32), pltpu.VMEM((nb, 8, wa), F32),
                        pltpu.VMEM((nb, 1, wa), F32), pltpu.VMEM((nb * heads, dv, dk), F32)],
        compiler_params=pltpu.CompilerParams(dimension_semantics=("arbitrary", "arbitrary"),
                                             vmem_limit_bytes=VMEM_LIMIT),
        name="even_mixer",
    )(h, *consts)


def _odd_mixer_kernel(h_ref, pos_ref, mnorm_ref, win_ref, freq_ref, rnorm_ref, mu_ref, w0_ref, ww2_ref, a0_ref,
                      aw2_ref, gw2_ref, kk_ref, ka_ref, rk_ref, wnorm_ref, seg_ref,
                      y_ref, proj_ref, rstate_ref, dprev_ref, wstate_ref, *, nb, ts, wc, wd, rw, ra, rg):
    f32 = F32

    @pl.when(pl.program_id(1) == 0)
    def _():
        rstate_ref[...] = jnp.zeros_like(rstate_ref)
        dprev_ref[...] = jnp.zeros_like(dprev_ref)
        wstate_ref[...] = jnp.zeros_like(wstate_ref)

    _project(h_ref, mnorm_ref, win_ref, proj_ref)

    o_d = 4 * wc
    dcols = 3 * wd + rw + ra + rg
    hdim = RWKV_HEAD_DIM
    cm = MIX_CHUNK
    n = 2 * cm
    groups = wd // LANES
    seg = seg_ref[...]
    lane = lax.broadcasted_iota(jnp.int32, (cm, LANES), 1)
    first_head = lane < hdim

    def stack_heads(t):
        return jnp.concatenate([jnp.where(first_head, t, 0.0), jnp.where(first_head, 0.0, t)], axis=0)

    rin = lax.broadcasted_iota(jnp.int32, (n, n), 0)
    cin = lax.broadcasted_iota(jnp.int32, (n, n), 1)
    strict = (rin & (cm - 1)) > (cin & (cm - 1))
    incl = (rin & (cm - 1)) >= (cin & (cm - 1))
    eye = jnp.where(rin == cin, 1.0, 0.0).astype(f32)
    tri_incl = jnp.where(_tri(cm), 1.0, 0.0).astype(BF16)

    work = []
    dparts = []
    for b in range(nb):
        dpart = proj_ref[b, :, o_d:o_d + dcols]
        dshift = jnp.concatenate([dprev_ref[b], dpart], axis=0)[7:7 + ts]
        dprev_ref[b] = dpart[ts - 8:ts]
        dparts.append(dpart + mu_ref[...] * (dshift - dpart))
    dall = jnp.concatenate(dparts, axis=0)
    r_all = dall[:, 0:wd]
    k_all = dall[:, wd:2 * wd]
    v_all = dall[:, 2 * wd:3 * wd]
    w_lr = dall[:, 3 * wd:3 * wd + rw]
    a_lr = dall[:, 3 * wd + rw:3 * wd + rw + ra]
    g_lr = dall[:, 3 * wd + rw + ra:dcols]
    w_all = -_softplus(-(w0_ref[...] + _dot(jnp.tanh(w_lr).astype(BF16), ww2_ref[...]))) - 0.5
    log_decay_all = -jnp.exp(w_all)
    a_all = jax.nn.sigmoid(a0_ref[...] + _dot(a_lr.astype(BF16), aw2_ref[...]))
    g_all = _dot(jax.nn.sigmoid(g_lr).astype(BF16), gw2_ref[...])
    kk_all = k_all * kk_ref[...]
    k_all = k_all * (1.0 + (a_all - 1.0) * ka_ref[...])
    rows_all = nb * ts
    sums = _head_sums(jnp.concatenate([kk_all * kk_all, r_all * k_all * rk_ref[...]], axis=0), seg)
    kk_all = kk_all * lax.rsqrt(sums[:rows_all] + 1e-12)
    bonus_all = sums[rows_all:] * v_all
    kb_all = kk_all * a_all
    for b in range(nb):
        for ch in range(ts // cm):
            rs = slice(b * ts + ch * cm, b * ts + (ch + 1) * cm)
            ld = log_decay_all[rs]
            cum = _dot_split_rhs(tri_incl, ld, 2)
            cum_last = cum[cm - 1:cm]
            p_incl = jnp.exp(cum)
            p_inv = jnp.exp(-cum)
            a_t = -kk_all[rs] * jnp.exp(cum - ld)
            r_t = r_all[rs] * p_incl
            b_t = kb_all[rs] * p_inv
            k_t = k_all[rs] * p_inv
            p_out = jnp.exp(cum_last - cum)
            b_o = kb_all[rs] * p_out
            k_o = k_all[rs] * p_out
            p_last = jnp.exp(cum_last)
            for pr in range(groups):
                ls = slice(pr * LANES, (pr + 1) * LANES)
                l4 = jnp.concatenate([stack_heads(a_t[:, ls]), stack_heads(r_t[:, ls])], axis=0).astype(BF16)
                r4 = jnp.concatenate([stack_heads(b_t[:, ls]), stack_heads(k_t[:, ls])], axis=0).astype(BF16)
                o4 = jnp.concatenate([stack_heads(b_o[:, ls]), stack_heads(k_o[:, ls])], axis=0).astype(BF16)
                vbd = stack_heads(v_all[rs, ls]).astype(BF16)
                work.append(dict(b=b, ch=ch, pr=pr, l4=l4, r4=r4, o4=o4, vbd=vbd, p_last=p_last[:, ls]))
    for wk in work:
        big = _dot_nt(wk["l4"], wk["r4"])
        a_ab = jnp.where(strict, big[:n, :n], 0.0)
        wk["a_ak"] = jnp.where(strict, big[:n, n:], 0.0).astype(BF16)
        wk["a_r"] = jnp.concatenate([jnp.where(incl, big[n:, :n], 0.0), jnp.where(incl, big[n:, n:], 0.0)],
                                    axis=1).astype(BF16)
        wk["x"] = a_ab.astype(BF16)
        wk["tinv"] = eye + a_ab
    span = 2
    while span < cm:
        for wk in work:
            wk["x"] = _dot(wk["x"], wk["x"]).astype(BF16)
        for wk in work:
            wk["tinv"] = wk["tinv"] + _dot(wk["tinv"].astype(BF16), wk["x"])
        span *= 2

    c = ts
    dh = wc // RET_HEADS
    half = dh // 2
    rowi = lax.broadcasted_iota(jnp.int32, (c, c), 0)
    coli = lax.broadcasted_iota(jnp.int32, (c, c), 1)
    rel = (rowi - coli).astype(f32)
    rowd = lax.broadcasted_iota(jnp.int32, (c, dh), 0).astype(f32)
    ret = []
    for b in range(nb):
        ang = freq_ref[...] * pos_ref[b]
        cos, sin = jnp.cos(ang).T, jnp.sin(ang).T
        cos2 = jnp.concatenate([cos, cos], axis=-1)
        sin2 = jnp.concatenate([-sin, sin], axis=-1)
        for hd in range(RET_HEADS):
            log_gamma = math.log1p(-2.0 ** (-5.0 - hd))
            q = proj_ref[b, :, hd * dh:(hd + 1) * dh]
            k = proj_ref[b, :, wc + hd * dh:wc + (hd + 1) * dh]
            v = proj_ref[b, :, 2 * wc + hd * dh:2 * wc + (hd + 1) * dh].astype(BF16)
            q = (q * cos2 + pltpu.roll(q, half, 1) * sin2) * (dh ** -0.5)
            k = k * cos2 + pltpu.roll(k, half, 1) * sin2
            dmask = jnp.where(rel >= 0.0, jnp.exp(jnp.maximum(rel, 0.0) * log_gamma), 0.0)
            ret.append(dict(b=b, hd=hd, v=v, q=q.astype(BF16), k=k.astype(BF16), dmask=dmask,
                            q_in=(q * jnp.exp((rowd + 1.0) * log_gamma)).astype(BF16),
                            k_out=(k * jnp.exp((c - 1.0 - rowd) * log_gamma)).astype(BF16),
                            decay=math.exp(c * log_gamma)))
    for rt in ret:
        rt["scores"] = (_dot_nt(rt["q"], rt["k"]) * rt["dmask"]).astype(BF16)
    for rt in ret:
        st = rstate_ref[rt["b"] * RET_HEADS + rt["hd"]]
        o = _dot(rt["scores"], rt["v"]) + _dot(rt["q_in"], st.astype(BF16))
        rstate_ref[rt["b"] * RET_HEADS + rt["hd"]] = st * rt["decay"] + _dot_tn(rt["k_out"], rt["v"])
        o = o - jnp.mean(o, axis=-1, keepdims=True)
        rt["o"] = o * lax.rsqrt(jnp.mean(o * o, axis=-1, keepdims=True) + HEAD_NORM_EPS)
    for b in range(nb):
        g = proj_ref[b, :, 3 * wc:4 * wc]
        o_all = jnp.concatenate([rt["o"] for rt in ret if rt["b"] == b], axis=-1)
        y_ref[b, :, 0:wc] = (o_all * rnorm_ref[...] * _silu(g)).astype(y_ref.dtype)

    for ch in range(ts // cm):
        wks = [wk for wk in work if wk["ch"] == ch]
        sidx = [wk["b"] * groups + wk["pr"] for wk in wks]
        sts = [wstate_ref[i] for i in sidx]
        fss = [_dot_nt(wk["l4"], st.astype(BF16)) for wk, st in zip(wks, sts)]
        rhs = [(fs[:n] + _dot(wk["a_ak"], wk["vbd"])).astype(BF16) for wk, fs in zip(wks, fss)]
        us = [_dot(wk["tinv"].astype(BF16), rh) for wk, rh in zip(wks, rhs)]
        uvs = [jnp.concatenate([u.astype(BF16), wk["vbd"]], axis=0) for wk, u in zip(wks, us)]
        for i, st, wk, uv in zip(sidx, sts, wks, uvs):
            wstate_ref[i] = st * wk["p_last"] + _dot_tn(uv, wk["o4"])
        ybds = [fs[n:] + _dot(wk["a_r"], uv) for wk, fs, uv in zip(wks, fss, uvs)]
        y = jnp.concatenate(
            [jnp.concatenate([yb[:cm] + yb[cm:] for wk, yb in zip(wks, ybds) if wk["b"] == b], axis=-1)
             for b in range(nb)], axis=0)
        inv_h = 1.0 / hdim
        y = y - _head_sums(y, seg) * inv_h
        y = y * lax.rsqrt(_head_sums(y * y, seg) * inv_h + RWKV_LN_EPS)
        for b in range(nb):
            tile_rows = slice(b * ts + ch * cm, b * ts + (ch + 1) * cm)
            yb = (y[b * cm:(b + 1) * cm] * wnorm_ref[...] + bonus_all[tile_rows]) * g_all[tile_rows]
            y_ref[b, ch * cm:(ch + 1) * cm, wc:wc + wd] = yb.astype(y_ref.dtype)


def _odd_mixer(h, pos, mnorm, win, freq, rnorm, mu, w0, ww2, a0, aw2, gw2, kk, ka, rk, wnorm, seg, *, nb, ts, wc, wd, rw,
               ra, rg):
    bsz, seq, d = h.shape
    cols = win.shape[1]
    dh = wc // RET_HEADS
    consts = (mnorm, win, freq, rnorm, mu, w0, ww2, a0, aw2, gw2, kk, ka, rk, wnorm, seg)
    kern = functools.partial(_odd_mixer_kernel, nb=nb, ts=ts, wc=wc, wd=wd, rw=rw, ra=ra, rg=rg)
    dcols = 3 * wd + rw + ra + rg
    return pl.pallas_call(
        kern,
        grid=(bsz // nb, seq // ts),
        in_specs=[pl.BlockSpec((nb, ts, d), lambda b, s: (b, s, 0)),
                  pl.BlockSpec((nb, 1, ts), lambda b, s: (b, 0, s))] + [_resident(t) for t in consts],
        out_specs=pl.BlockSpec((nb, ts, wc + wd), lambda b, s: (b, s, 0)),
        out_shape=jax.ShapeDtypeStruct((bsz, seq, wc + wd), BF16),
        scratch_shapes=[pltpu.VMEM((nb, ts, cols), F32), pltpu.VMEM((nb * RET_HEADS, dh, dh), F32),
                        pltpu.VMEM((nb, 8, dcols), F32), pltpu.VMEM((nb * (wd // LANES), LANES, LANES), F32)],
        compiler_params=pltpu.CompilerParams(dimension_semantics=("arbitrary", "arbitrary"),
                                             vmem_limit_bytes=VMEM_LIMIT),
        name="odd_mixer",
    )(h, pos, *consts)


def _post_kernel(h_ref, y_ref, p_ref, wout_ref, fnorm_ref, wg_ref, wu_ref, wd_ref, pnorm_ref, pwg_ref, pbg_ref,
                 pwp_ref, onorm_ref, o_ref, *, final):
    half = h_ref.shape[0] // 2
    starts = (0, half)
    hs, xs = [], []
    for r0 in starts:
        h = h_ref[r0:r0 + half] + _dot(y_ref[r0:r0 + half], wout_ref[...])
        hs.append(h)
        xs.append(_rms(h, fnorm_ref[...]).astype(BF16))
    acts = [(_silu(_dot(xn, wg_ref[...])) * _dot(xn, wu_ref[...])).astype(BF16) for xn in xs]
    hs = [h + _dot(act, wd_ref[...]) for h, act in zip(hs, acts)]
    xgs = [_rms(h, pnorm_ref[...]).astype(BF16) for h in hs]
    for r0, h, xg in zip(starts, hs, xgs):
        gate = jax.nn.sigmoid(_dot(xg, pwg_ref[...]) + pbg_ref[...])
        h = h + gate * _dot(p_ref[r0:r0 + half].astype(BF16), pwp_ref[...])
        if final:
            h = _rms(h, onorm_ref[...])
        o_ref[r0:r0 + half] = h


def _post(h, y, p, wout, fnorm, wg, wu, wd, pnorm, pwg, pbg, pwp, onorm, *, layer, tm, final):
    m, d = h.shape
    rows = lambda width: pl.BlockSpec((tm, width), lambda i: (i, 0))
    resident = lambda t: pl.BlockSpec(t.shape, lambda i: (0, 0), pipeline_mode=pl.Buffered(1))
    of_layer = lambda t: pl.BlockSpec((None,) + t.shape[1:], lambda i: (layer, 0, 0), pipeline_mode=pl.Buffered(1))
    stacked = (fnorm, wg, wu, wd, pnorm, pwg, pbg, pwp)
    return pl.pallas_call(
        functools.partial(_post_kernel, final=final),
        grid=(m // tm,),
        in_specs=[rows(d), rows(y.shape[1]), pl.BlockSpec((None, tm, p.shape[2]), lambda i: (layer, i, 0)),
                  resident(wout)] + [of_layer(t) for t in stacked] + [resident(onorm)],
        out_specs=rows(d),
        out_shape=jax.ShapeDtypeStruct((m, d), F32),
        compiler_params=pltpu.CompilerParams(dimension_semantics=("arbitrary",), vmem_limit_bytes=VMEM_LIMIT),
        name="post_mixer",
    )(h, y, p, wout, *stacked, onorm)


def _block_diag(blocks):
    g, bi, bj = blocks.shape
    eye = jnp.eye(g, dtype=blocks.dtype)
    return (eye[:, None, :, None] * blocks[:, :, None, :]).reshape(g * bi, g * bj)


def kernel(x, p, positions, ev_w_in, ev_conv_w, ev_conv_b, ev_lru_wr, ev_lru_br, ev_lru_wi, ev_lru_bi, ev_lru_a, ev_gla_wgk, ev_gla_bgk, ev_gla_norm, ev_w_out, od_w_in, od_ret_norm, od_rwkv_mu, od_rwkv_w0, od_rwkv_ww2, od_rwkv_a0, od_rwkv_aw2, od_rwkv_gw2, od_rwkv_kk, od_rwkv_ka, od_rwkv_rk, od_rwkv_norm, od_w_out, mix_norm, ffn_norm, ffn_w_gate, ffn_w_up, ffn_w_down, ple_norm, ple_w_gate, ple_b_gate, ple_w_proj, final_norm):
    bsz, seq, d = x.shape
    depth = p.shape[0]
    m = bsz * seq
    row = lambda t: t.reshape(1, -1).astype(F32)
    rows3 = lambda t: t.reshape(t.shape[0], 1, -1).astype(F32)

    wa = ev_conv_w.shape[-1]
    rank = ev_gla_wgk.shape[1]
    wq = ev_gla_wgk.shape[2]
    wb = ev_gla_norm.shape[-1]
    dk, dv = wq // GLA_HEADS, wb // GLA_HEADS
    wc = od_ret_norm.shape[-1]
    wd = od_rwkv_w0.shape[-1]
    rw, ra, rg = od_rwkv_ww2.shape[1], od_rwkv_aw2.shape[1], od_rwkv_gw2.shape[1]
    nb = MIX_BATCH if bsz % MIX_BATCH == 0 else 1

    h = x.reshape(m, d)
    freq = (ROPE_BASE ** (-jnp.arange(wc // RET_HEADS // 2, dtype=F32) / (wc // RET_HEADS // 2))).reshape(-1, 1)
    pos = positions.astype(F32).reshape(bsz, 1, seq)
    seg = _block_diag(jnp.ones((LANES // RWKV_HEAD_DIM, RWKV_HEAD_DIM, RWKV_HEAD_DIM), BF16))
    assert 2 * RWKV_HEAD_DIM == LANES and wd % LANES == 0
    assert seq % EVEN_TILE == 0 and seq % ODD_TILE == 0 and m % POST_ROWS == 0

    for i in range(depth):
        j = i // 2
        if i % 2 == 0:
            w_in = ev_w_in[j]
            cols = w_in.shape[1]
            cols_pad = -(-cols // LANES) * LANES
            w_in = jnp.pad(w_in, ((0, 0), (0, cols_pad - cols))).astype(BF16)
            wgate = jnp.concatenate([_block_diag(ev_lru_wr[j]), _block_diag(ev_lru_wi[j])], axis=1).astype(BF16)
            bgate = jnp.concatenate([ev_lru_br[j], ev_lru_bi[j]]).reshape(1, -1)
            y = _even_mixer(h.reshape(bsz, seq, d), row(mix_norm[i]), w_in, ev_conv_w[j], row(ev_conv_b[j]), wgate, bgate,
                            row(ev_lru_a[j]), ev_gla_wgk[j], row(ev_gla_bgk[j]), row(ev_gla_norm[j]),
                            nb=nb, ts=EVEN_TILE, wa=wa, dk=dk, dv=dv, rank=rank)
            w_out = ev_w_out[j]
        else:
            y = _odd_mixer(h.reshape(bsz, seq, d), pos, row(mix_norm[i]), od_w_in[j].astype(BF16), freq,
                           row(od_ret_norm[j]), row(od_rwkv_mu[j]),
                           row(od_rwkv_w0[j]), od_rwkv_ww2[j].astype(BF16), row(od_rwkv_a0[j]),
                           od_rwkv_aw2[j].astype(BF16), od_rwkv_gw2[j].astype(BF16), row(od_rwkv_kk[j]),
                           row(od_rwkv_ka[j]), row(od_rwkv_rk[j]), row(od_rwkv_norm[j]), seg,
                           nb=nb, ts=ODD_TILE, wc=wc, wd=wd, rw=rw, ra=ra, rg=rg)
            w_out = od_w_out[j]
        h = _post(h, y.reshape(m, -1), p.reshape(depth, m, -1), w_out.astype(BF16), rows3(ffn_norm),
                  ffn_w_gate.astype(BF16), ffn_w_up.astype(BF16), ffn_w_down.astype(BF16),
                  rows3(ple_norm), ple_w_gate.astype(BF16), rows3(ple_b_gate), ple_w_proj.astype(BF16),
                  row(final_norm), layer=i, tm=POST_ROWS, final=(i == depth - 1))
    return h.reshape(bsz, seq, d)
```

```python
import functools
import math

import jax
import jax.numpy as jnp
from jax import lax
from jax.experimental import pallas as pl
from jax.experimental.pallas import tpu as pltpu

F32 = jnp.float32
BF16 = jnp.bfloat16

NORM_EPS = 1e-6
LANES = 128
SUBLANES = 8
CONV_W = 4
LRU_C = 8.0
GLA_HEADS = 4
GLA_GATE_NORM = 16.0
GLA_SUB = 16
RET_HEADS = 4
RET_CHUNK = 128
ROPE_BASE = 10000.0
RWKV_HEAD_DIM = 64
RWKV_LN_EPS = 64e-5
HEAD_NORM_EPS = 1e-5

MIX_CHUNK = 64
MIX_BATCH = 4
EVEN_TILE = 4 * MIX_CHUNK
ODD_TILE = RET_CHUNK
POST_ROWS = 512
VMEM_LIMIT = 56 * 1024 * 1024


def _log_sigmoid(x):
    return jnp.minimum(x, 0.0) - jnp.log1p(jnp.exp(-jnp.abs(x)))


def _silu(x):
    return x * jax.nn.sigmoid(x)


def _gelu_tanh(x):
    return 0.5 * x * (1.0 + jnp.tanh(math.sqrt(2.0 / math.pi) * (x + 0.044715 * (x * x * x))))


def _rms(x, g):
    return x * lax.rsqrt(jnp.mean(x * x, axis=-1, keepdims=True) + NORM_EPS) * g


def _dot(a, b, precision=None):
    return jnp.dot(a, b, preferred_element_type=F32, precision=precision)


def _dot_nt(a, b, precision=None):
    return lax.dot_general(a, b, (((1,), (1,)), ((), ())), preferred_element_type=F32, precision=precision)


def _dot_tn(a, b, precision=None):
    return lax.dot_general(a, b, (((0,), (0,)), ((), ())), preferred_element_type=F32, precision=precision)


def _head_sums(x, seg):
    return jnp.concatenate([_dot(x[:, g:g + LANES].astype(BF16), seg) for g in range(0, x.shape[1], LANES)], axis=-1)


def _dot_split_rhs(m, x, passes):
    acc = None
    for _ in range(passes):
        piece = x.astype(BF16)
        acc = _dot(m, piece) if acc is None else acc + _dot(m, piece)
        x = x - piece.astype(F32)
    return acc


def _tri(n, strict=False):
    row = lax.broadcasted_iota(jnp.int32, (n, n), 0)
    col = lax.broadcasted_iota(jnp.int32, (n, n), 1)
    return (row > col) if strict else (row >= col)


PROJ_GROUP = 512


def _project(h_ref, mnorm_ref, win_ref, proj_ref):
    nb, ts, d = h_ref.shape
    xn = _rms(h_ref[...].reshape(nb * ts, d), mnorm_ref[...]).astype(BF16)
    cols = proj_ref.shape[2]
    for c0 in range(0, cols, PROJ_GROUP):
        c1 = min(c0 + PROJ_GROUP, cols)
        proj_ref[:, :, c0:c1] = _dot(xn, win_ref[:, c0:c1]).reshape(nb, ts, c1 - c0)


def _resident(t):
    return pl.BlockSpec(t.shape, lambda b, s: (0,) * t.ndim, pipeline_mode=pl.Buffered(1))


def _rg_lru(proj_ref, y_ref, xtail_ref, hcar_ref, convw_ref, convb_ref, wgate_ref, bgate_ref, lrua_ref, *, nb, ts, wa):
    cw = convw_ref[...]
    xcs = []
    for b in range(nb):
        ax = proj_ref[b, :, 0:wa]
        xcat = jnp.concatenate([xtail_ref[b], ax], axis=0)
        xtail_ref[b] = ax[ts - 8:ts]
        xc = convb_ref[...] + cw[CONV_W - 1:CONV_W] * ax
        for s in range(1, CONV_W):
            xc = xc + cw[CONV_W - 1 - s:CONV_W - s] * xcat[8 - s:8 - s + ts]
        xcs.append(xc)
    gates_all = _dot(jnp.concatenate(xcs, axis=0).astype(BF16), wgate_ref[...]) + bgate_ref[...]
    log_sig_a = _log_sigmoid(lrua_ref[...])
    groups = ts // SUBLANES
    row = lax.broadcasted_iota(jnp.int32, (groups, SUBLANES, wa), 1)
    for b in range(nb):
        xc = xcs[b]
        gates = gates_all[b * ts:(b + 1) * ts]
        r = jax.nn.sigmoid(gates[:, :wa])
        i = jax.nn.sigmoid(gates[:, wa:])
        log_a = LRU_C * r * log_sig_a
        a = jnp.exp(log_a)
        t = jnp.tanh(log_a)
        u = jnp.sqrt(-2.0 * t / (1.0 - t)) * (i * xc)
        u = u.reshape(groups, SUBLANES, wa)
        a = a.reshape(groups, SUBLANES, wa)
        d = 1
        while d < SUBLANES:
            keep = row >= d
            u = jnp.where(keep, a * pltpu.roll(u, d, 1) + u, u)
            a = jnp.where(keep, a * pltpu.roll(a, d, 1), a)
            d *= 2
        carry = hcar_ref[b]
        pieces = []
        for j in range(groups):
            hj = u[j] + a[j] * carry
            carry = hj[SUBLANES - 1:SUBLANES]
            pieces.append(hj)
        hcar_ref[b] = carry
        h = jnp.concatenate(pieces, axis=0)
        y_ref[b, :, 0:wa] = (h * _gelu_tanh(proj_ref[b, :, wa:2 * wa])).astype(y_ref.dtype)


def _even_mixer_kernel(h_ref, mnorm_ref, win_ref, convw_ref, convb_ref, wgate_ref, bgate_ref, lrua_ref,
                       wgk_ref, bgk_ref, gnorm_ref, y_ref, proj_ref, xtail_ref, hcar_ref, st_ref,
                       *, nb, ts, wa, dk, dv, rank):
    heads = GLA_HEADS
    wq = heads * dk
    wb = heads * dv
    o_q = 2 * wa
    o_k, o_v = o_q + wq, o_q + 2 * wq
    o_g, o_gk = o_v + wb, o_v + 2 * wb
    step = pl.program_id(1)

    @pl.when(step == 0)
    def _():
        xtail_ref[...] = jnp.zeros_like(xtail_ref)
        hcar_ref[...] = jnp.zeros_like(hcar_ref)
        st_ref[...] = jnp.zeros_like(st_ref)

    _project(h_ref, mnorm_ref, win_ref, proj_ref)
    _rg_lru(proj_ref, y_ref, xtail_ref, hcar_ref, convw_ref, convb_ref, wgate_ref, bgate_ref, lrua_ref,
            nb=nb, ts=ts, wa=wa)

    cg = MIX_CHUNK
    tri_incl = jnp.where(_tri(cg), 1.0, 0.0).astype(BF16)
    causal = _tri(cg)
    rowc = lax.broadcasted_iota(jnp.int32, (cg, wq), 0)
    items = []
    gk = proj_ref[:, :, o_gk:o_gk + rank].reshape(nb * ts, rank)
    gk_hi = gk.astype(BF16)
    gk_lo = (gk - gk_hi.astype(F32)).astype(BF16)
    w_hi = wgk_ref[...].astype(BF16)
    w_lo = (wgk_ref[...] - w_hi.astype(F32)).astype(BF16)
    z_all = _dot(gk_hi, w_hi) + (_dot(gk_lo, w_hi) + _dot(gk_hi, w_lo)) + bgk_ref[...]
    log_f_all = _log_sigmoid(z_all) * (1.0 / GLA_GATE_NORM)
    for b in range(nb):
        for c in range(ts // cg):
            r0 = c * cg
            cum = _dot_split_rhs(tri_incl, log_f_all[b * ts + r0:b * ts + r0 + cg], 2)
            q = proj_ref[b, r0:r0 + cg, o_q:o_q + wq] * (dk ** -0.5)
            k = proj_ref[b, r0:r0 + cg, o_k:o_k + wq]
            v = proj_ref[b, r0:r0 + cg, o_v:o_v + wb].astype(BF16)
            cum_last = cum[cg - 1:cg]
            it = dict(b=b, c=c, v=v, q_in=(q * jnp.exp(cum)).astype(BF16),
                      k_out=(k * jnp.exp(cum_last - cum)).astype(BF16), chunk_decay=jnp.exp(cum_last), qb=[], kb=[])
            for ib in range(cg // GLA_SUB):
                b0 = ib * GLA_SUB
                base = cum[b0 - 1:b0] if ib > 0 else jnp.zeros((1, wq), F32)
                it["qb"].append((q[b0:b0 + GLA_SUB] * jnp.exp(cum[b0:b0 + GLA_SUB] - base)).astype(BF16))
                it["kb"].append(jnp.where(rowc < b0 + GLA_SUB, k * jnp.exp(base - cum), 0.0).astype(BF16))
            items.append(it)
    for it in items:
        it["scores"] = [
            jnp.where(causal, jnp.concatenate(
                [_dot_nt(qb[:, hd * dk:(hd + 1) * dk], kb[:, hd * dk:(hd + 1) * dk])
                 for qb, kb in zip(it["qb"], it["kb"])], axis=0), 0.0).astype(BF16)
            for hd in range(heads)]
    for it in items:
        it["intra"] = [_dot(it["scores"][hd], it["v"][:, hd * dv:(hd + 1) * dv]) for hd in range(heads)]
    for c in range(ts // cg):
        r0 = c * cg
        for it in [t for t in items if t["c"] == c]:
            b = it["b"]
            outs = []
            for hd in range(heads):
                ks = slice(hd * dk, (hd + 1) * dk)
                vs = slice(hd * dv, (hd + 1) * dv)
                st = st_ref[b * heads + hd]
                o = it["intra"][hd] + _dot_nt(it["q_in"][:, ks], st.astype(BF16))
                st_ref[b * heads + hd] = st * it["chunk_decay"][:, ks] + _dot_tn(it["v"][:, vs], it["k_out"][:, ks])
                outs.append(o * lax.rsqrt(jnp.mean(o * o, axis=-1, keepdims=True) + HEAD_NORM_EPS))
            o_all = jnp.concatenate(outs, axis=-1) * gnorm_ref[...]
            g = proj_ref[b, r0:r0 + cg, o_g:o_g + wb]
            y_ref[b, r0:r0 + cg, wa:wa + wb] = (o_all * _silu(g)).astype(y_ref.dtype)


def _even_mixer(h, mnorm, win, convw, convb, wgate, bgate, lrua, wgk, bgk, gnorm, *, nb, ts, wa, dk, dv, rank):
    bsz, seq, d = h.shape
    cols = win.shape[1]
    heads = GLA_HEADS
    wb = heads * dv
    consts = (mnorm, win, convw, convb, wgate, bgate, lrua, wgk, bgk, gnorm)
    kern = functools.partial(_even_mixer_kernel, nb=nb, ts=ts, wa=wa, dk=dk, dv=dv, rank=rank)
    return pl.pallas_call(
        kern,
        grid=(bsz // nb, seq // ts),
        in_specs=[pl.BlockSpec((nb, ts, d), lambda b, s: (b, s, 0))] + [_resident(t) for t in consts],
        out_specs=pl.BlockSpec((nb, ts, wa + wb), lambda b, s: (b, s, 0)),
        out_shape=jax.ShapeDtypeStruct((bsz, seq, wa + wb), BF16),
        scratch_shapes=[pltpu.VMEM((nb, ts, cols), F32), pltpu.VMEM((nb, 8, wa), F32),
                        pltpu.VMEM((nb, 1, wa), F32), pltpu.VMEM((nb * heads, dv, dk), F32)],
        compiler_params=pltpu.CompilerParams(dimension_semantics=("arbitrary", "arbitrary"),
                                             vmem_limit_bytes=VMEM_LIMIT),
        name="even_mixer",
    )(h, *consts)


def _odd_mixer_kernel(h_ref, pos_ref, mnorm_ref, win_ref, freq_ref, rnorm_ref, mu_ref, w0_ref, ww2_ref, a0_ref,
                      aw2_ref, gw2_ref, kk_ref, ka_ref, rk_ref, wnorm_ref, seg_ref,
                      y_ref, proj_ref, rstate_ref, dprev_ref, wstate_ref, *, nb, ts, wc, wd, rw, ra, rg):
    f32 = F32

    @pl.when(pl.program_id(1) == 0)
    def _():
        rstate_ref[...] = jnp.zeros_like(rstate_ref)
        dprev_ref[...] = jnp.zeros_like(dprev_ref)
        wstate_ref[...] = jnp.zeros_like(wstate_ref)

    _project(h_ref, mnorm_ref, win_ref, proj_ref)

    o_d = 4 * wc
    dcols = 3 * wd + rw + ra + rg
    hdim = RWKV_HEAD_DIM
    cm = MIX_CHUNK
    n = 2 * cm
    groups = wd // LANES
    seg = seg_ref[...]
    lane = lax.broadcasted_iota(jnp.int32, (cm, LANES), 1)
    first_head = lane < hdim

    def stack_heads(t):
        return jnp.concatenate([jnp.where(first_head, t, 0.0), jnp.where(first_head, 0.0, t)], axis=0)

    rin = lax.broadcasted_iota(jnp.int32, (n, n), 0)
    cin = lax.broadcasted_iota(jnp.int32, (n, n), 1)
    strict = (rin & (cm - 1)) > (cin & (cm - 1))
    incl = (rin & (cm - 1)) >= (cin & (cm - 1))
    eye = jnp.where(rin == cin, 1.0, 0.0).astype(f32)
    tri_incl = jnp.where(_tri(cm), 1.0, 0.0).astype(BF16)

    work = []
    dparts = []
    for b in range(nb):
        dpart = proj_ref[b, :, o_d:o_d + dcols]
        dshift = jnp.concatenate([dprev_ref[b], dpart], axis=0)[7:7 + ts]
        dprev_ref[b] = dpart[ts - 8:ts]
        dparts.append(dpart + mu_ref[...] * (dshift - dpart))
    dall = jnp.concatenate(dparts, axis=0)
    r_all = dall[:, 0:wd]
    k_all = dall[:, wd:2 * wd]
    v_all = dall[:, 2 * wd:3 * wd]
    w_lr = dall[:, 3 * wd:3 * wd + rw]
    a_lr = dall[:, 3 * wd + rw:3 * wd + rw + ra]
    g_lr = dall[:, 3 * wd + rw + ra:dcols]
    log_decay_all = -math.exp(-0.5) * jax.nn.sigmoid(w0_ref[...] + _dot(jnp.tanh(w_lr).astype(BF16), ww2_ref[...]))
    a_all = jax.nn.sigmoid(a0_ref[...] + _dot(a_lr.astype(BF16), aw2_ref[...]))
    g_all = _dot(jax.nn.sigmoid(g_lr).astype(BF16), gw2_ref[...])
    kk_all = k_all * kk_ref[...]
    k_all = k_all * (1.0 + (a_all - 1.0) * ka_ref[...])
    rows_all = nb * ts
    sums = _head_sums(jnp.concatenate([kk_all * kk_all, r_all * k_all * rk_ref[...]], axis=0), seg)
    kk_all = kk_all * lax.rsqrt(sums[:rows_all] + 1e-12)
    bonus_all = sums[rows_all:] * v_all
    kb_all = kk_all * a_all
    for b in range(nb):
        for ch in range(ts // cm):
            rs = slice(b * ts + ch * cm, b * ts + (ch + 1) * cm)
            ld = log_decay_all[rs]
            cum = _dot_split_rhs(tri_incl, ld, 2)
            cum_last = cum[cm - 1:cm]
            p_incl = jnp.exp(cum)
            p_inv = jnp.exp(-cum)
            a_t = -kk_all[rs] * jnp.exp(cum - ld)
            r_t = r_all[rs] * p_incl
            b_t = kb_all[rs] * p_inv
            k_t = k_all[rs] * p_inv
            p_out = jnp.exp(cum_last - cum)
            b_o = kb_all[rs] * p_out
            k_o = k_all[rs] * p_out
            p_last = jnp.exp(cum_last)
            for pr in range(groups):
                ls = slice(pr * LANES, (pr + 1) * LANES)
                l4 = jnp.concatenate([stack_heads(a_t[:, ls]), stack_heads(r_t[:, ls])], axis=0).astype(BF16)
                r4 = jnp.concatenate([stack_heads(b_t[:, ls]), stack_heads(k_t[:, ls])], axis=0).astype(BF16)
                o4 = jnp.concatenate([stack_heads(b_o[:, ls]), stack_heads(k_o[:, ls])], axis=0).astype(BF16)
                vbd = stack_heads(v_all[rs, ls]).astype(BF16)
                work.append(dict(b=b, ch=ch, pr=pr, l4=l4, r4=r4, o4=o4, vbd=vbd, p_last=p_last[:, ls]))
    for wk in work:
        big = _dot_nt(wk["l4"], wk["r4"])
        a_ab = jnp.where(strict, big[:n, :n], 0.0)
        wk["a_ak"] = jnp.where(strict, big[:n, n:], 0.0).astype(BF16)
        wk["a_r"] = jnp.concatenate([jnp.where(incl, big[n:, :n], 0.0), jnp.where(incl, big[n:, n:], 0.0)],
                                    axis=1).astype(BF16)
        wk["x"] = a_ab.astype(BF16)
        wk["tinv"] = eye + a_ab
    span = 2
    while span < cm:
        for wk in work:
            wk["x"] = _dot(wk["x"], wk["x"]).astype(BF16)
        for wk in work:
            wk["tinv"] = wk["tinv"] + _dot(wk["tinv"].astype(BF16), wk["x"])
        span *= 2

    c = ts
    dh = wc // RET_HEADS
    half = dh // 2
    rowi = lax.broadcasted_iota(jnp.int32, (c, c), 0)
    coli = lax.broadcasted_iota(jnp.int32, (c, c), 1)
    rel = (rowi - coli).astype(f32)
    rowd = lax.broadcasted_iota(jnp.int32, (c, dh), 0).astype(f32)
    ret = []
    log_gammas = [math.log1p(-2.0 ** (-5.0 - hd)) for hd in range(RET_HEADS)]
    dmasks = [jnp.where(rel >= 0.0, jnp.exp(jnp.maximum(rel, 0.0) * lg), 0.0) for lg in log_gammas]
    decay_in = [jnp.exp((rowd + 1.0) * lg) for lg in log_gammas]
    decay_out = [jnp.exp((c - 1.0 - rowd) * lg) for lg in log_gammas]
    for b in range(nb):
        ang = freq_ref[...] * pos_ref[b]
        cos, sin = jnp.cos(ang).T, jnp.sin(ang).T
        cos2 = jnp.concatenate([cos, cos], axis=-1)
        sin2 = jnp.concatenate([-sin, sin], axis=-1)
        for hd in range(RET_HEADS):
            q = proj_ref[b, :, hd * dh:(hd + 1) * dh]
            k = proj_ref[b, :, wc + hd * dh:wc + (hd + 1) * dh]
            v = proj_ref[b, :, 2 * wc + hd * dh:2 * wc + (hd + 1) * dh].astype(BF16)
            q = (q * cos2 + pltpu.roll(q, half, 1) * sin2) * (dh ** -0.5)
            k = k * cos2 + pltpu.roll(k, half, 1) * sin2
            ret.append(dict(b=b, hd=hd, v=v, q=q.astype(BF16), k=k.astype(BF16), dmask=dmasks[hd],
                            q_in=(q * decay_in[hd]).astype(BF16), k_out=(k * decay_out[hd]).astype(BF16),
                            decay=math.exp(c * log_gammas[hd])))
    for rt in ret:
        rt["scores"] = (_dot_nt(rt["q"], rt["k"]) * rt["dmask"]).astype(BF16)
    for rt in ret:
        st = rstate_ref[rt["b"] * RET_HEADS + rt["hd"]]
        o = _dot(rt["scores"], rt["v"]) + _dot(rt["q_in"], st.astype(BF16))
        rstate_ref[rt["b"] * RET_HEADS + rt["hd"]] = st * rt["decay"] + _dot_tn(rt["k_out"], rt["v"])
        o = o - jnp.mean(o, axis=-1, keepdims=True)
        rt["o"] = o * lax.rsqrt(jnp.mean(o * o, axis=-1, keepdims=True) + HEAD_NORM_EPS)
    for b in range(nb):
        g = proj_ref[b, :, 3 * wc:4 * wc]
        o_all = jnp.concatenate([rt["o"] for rt in ret if rt["b"] == b], axis=-1)
        y_ref[b, :, 0:wc] = (o_all * rnorm_ref[...] * _silu(g)).astype(y_ref.dtype)

    for ch in range(ts // cm):
        wks = [wk for wk in work if wk["ch"] == ch]
        sidx = [wk["b"] * groups + wk["pr"] for wk in wks]
        sts = [wstate_ref[i] for i in sidx]
        fss = [_dot_nt(wk["l4"], st.astype(BF16)) for wk, st in zip(wks, sts)]
        rhs = [(fs[:n] + _dot(wk["a_ak"], wk["vbd"])).astype(BF16) for wk, fs in zip(wks, fss)]
        us = [_dot(wk["tinv"].astype(BF16), rh) for wk, rh in zip(wks, rhs)]
        uvs = [jnp.concatenate([u.astype(BF16), wk["vbd"]], axis=0) for wk, u in zip(wks, us)]
        for i, st, wk, uv in zip(sidx, sts, wks, uvs):
            wstate_ref[i] = st * wk["p_last"] + _dot_tn(uv, wk["o4"])
        ybds = [fs[n:] + _dot(wk["a_r"], uv) for wk, fs, uv in zip(wks, fss, uvs)]
        y = jnp.concatenate(
            [jnp.concatenate([yb[:cm] + yb[cm:] for wk, yb in zip(wks, ybds) if wk["b"] == b], axis=-1)
             for b in range(nb)], axis=0)
        inv_h = 1.0 / hdim
        y = y - _head_sums(y, seg) * inv_h
        y = y * lax.rsqrt(_head_sums(y * y, seg) * inv_h + RWKV_LN_EPS)
        for b in range(nb):
            tile_rows = slice(b * ts + ch * cm, b * ts + (ch + 1) * cm)
            yb = (y[b * cm:(b + 1) * cm] * wnorm_ref[...] + bonus_all[tile_rows]) * g_all[tile_rows]
            y_ref[b, ch * cm:(ch + 1) * cm, wc:wc + wd] = yb.astype(y_ref.dtype)


def _odd_mixer(h, pos, mnorm, win, freq, rnorm, mu, w0, ww2, a0, aw2, gw2, kk, ka, rk, wnorm, seg, *, nb, ts, wc, wd, rw,
               ra, rg):
    bsz, seq, d = h.shape
    cols = win.shape[1]
    dh = wc // RET_HEADS
    consts = (mnorm, win, freq, rnorm, mu, w0, ww2, a0, aw2, gw2, kk, ka, rk, wnorm, seg)
    kern = functools.partial(_odd_mixer_kernel, nb=nb, ts=ts, wc=wc, wd=wd, rw=rw, ra=ra, rg=rg)
    dcols = 3 * wd + rw + ra + rg
    return pl.pallas_call(
        kern,
        grid=(bsz // nb, seq // ts),
        in_specs=[pl.BlockSpec((nb, ts, d), lambda b, s: (b, s, 0)),
                  pl.BlockSpec((nb, 1, ts), lambda b, s: (b, 0, s))] + [_resident(t) for t in consts],
        out_specs=pl.BlockSpec((nb, ts, wc + wd), lambda b, s: (b, s, 0)),
        out_shape=jax.ShapeDtypeStruct((bsz, seq, wc + wd), BF16),
        scratch_shapes=[pltpu.VMEM((nb, ts, cols), F32), pltpu.VMEM((nb * RET_HEADS, dh, dh), F32),
                        pltpu.VMEM((nb, 8, dcols), F32), pltpu.VMEM((nb * (wd // LANES), LANES, LANES), F32)],
        compiler_params=pltpu.CompilerParams(dimension_semantics=("arbitrary", "arbitrary"),
                                             vmem_limit_bytes=VMEM_LIMIT),
        name="odd_mixer",
    )(h, pos, *consts)


def _post_kernel(h_ref, y_ref, p_ref, wout_ref, fnorm_ref, wg_ref, wu_ref, wd_ref, pnorm_ref, pwg_ref, pbg_ref,
                 pwp_ref, onorm_ref, o_ref, *, final):
    half = h_ref.shape[0] // 2
    starts = (0, half)
    hs, xs = [], []
    for r0 in starts:
        h = h_ref[r0:r0 + half] + _dot(y_ref[r0:r0 + half], wout_ref[...])
        hs.append(h)
        xs.append(_rms(h, fnorm_ref[...]).astype(BF16))
    acts = [(_silu(_dot(xn, wg_ref[...])) * _dot(xn, wu_ref[...])).astype(BF16) for xn in xs]
    hs = [h + _dot(act, wd_ref[...]) for h, act in zip(hs, acts)]
    xgs = [_rms(h, pnorm_ref[...]).astype(BF16) for h in hs]
    for r0, h, xg in zip(starts, hs, xgs):
        gate = jax.nn.sigmoid(_dot(xg, pwg_ref[...]) + pbg_ref[...])
        h = h + gate * _dot(p_ref[r0:r0 + half].astype(BF16), pwp_ref[...])
        if final:
            h = _rms(h, onorm_ref[...])
        o_ref[r0:r0 + half] = h


def _post(h, y, p, wout, fnorm, wg, wu, wd, pnorm, pwg, pbg, pwp, onorm, *, layer, tm, final):
    m, d = h.shape
    rows = lambda width: pl.BlockSpec((tm, width), lambda i: (i, 0))
    resident = lambda t: pl.BlockSpec(t.shape, lambda i: (0, 0), pipeline_mode=pl.Buffered(1))
    of_layer = lambda t: pl.BlockSpec((None,) + t.shape[1:], lambda i: (layer, 0, 0), pipeline_mode=pl.Buffered(1))
    stacked = (fnorm, wg, wu, wd, pnorm, pwg, pbg, pwp)
    return pl.pallas_call(
        functools.partial(_post_kernel, final=final),
        grid=(m // tm,),
        in_specs=[rows(d), rows(y.shape[1]), pl.BlockSpec((None, tm, p.shape[2]), lambda i: (layer, i, 0)),
                  resident(wout)] + [of_layer(t) for t in stacked] + [resident(onorm)],
        out_specs=rows(d),
        out_shape=jax.ShapeDtypeStruct((m, d), F32),
        compiler_params=pltpu.CompilerParams(dimension_semantics=("arbitrary",), vmem_limit_bytes=VMEM_LIMIT),
        name="post_mixer",
    )(h, y, p, wout, *stacked, onorm)


def _block_diag(blocks):
    g, bi, bj = blocks.shape
    eye = jnp.eye(g, dtype=blocks.dtype)
    return (eye[:, None, :, None] * blocks[:, :, None, :]).reshape(g * bi, g * bj)


def kernel(x, p, positions, ev_w_in, ev_conv_w, ev_conv_b, ev_lru_wr, ev_lru_br, ev_lru_wi, ev_lru_bi, ev_lru_a, ev_gla_wgk, ev_gla_bgk, ev_gla_norm, ev_w_out, od_w_in, od_ret_norm, od_rwkv_mu, od_rwkv_w0, od_rwkv_ww2, od_rwkv_a0, od_rwkv_aw2, od_rwkv_gw2, od_rwkv_kk, od_rwkv_ka, od_rwkv_rk, od_rwkv_norm, od_w_out, mix_norm, ffn_norm, ffn_w_gate, ffn_w_up, ffn_w_down, ple_norm, ple_w_gate, ple_b_gate, ple_w_proj, final_norm):
    bsz, seq, d = x.shape
    depth = p.shape[0]
    m = bsz * seq
    row = lambda t: t.reshape(1, -1).astype(F32)
    rows3 = lambda t: t.reshape(t.shape[0], 1, -1).astype(F32)

    wa = ev_conv_w.shape[-1]
    rank = ev_gla_wgk.shape[1]
    wq = ev_gla_wgk.shape[2]
    wb = ev_gla_norm.shape[-1]
    dk, dv = wq // GLA_HEADS, wb // GLA_HEADS
    wc = od_ret_norm.shape[-1]
    wd = od_rwkv_w0.shape[-1]
    rw, ra, rg = od_rwkv_ww2.shape[1], od_rwkv_aw2.shape[1], od_rwkv_gw2.shape[1]
    nb = MIX_BATCH if bsz % MIX_BATCH == 0 else 1

    h = x.reshape(m, d)
    freq = (ROPE_BASE ** (-jnp.arange(wc // RET_HEADS // 2, dtype=F32) / (wc // RET_HEADS // 2))).reshape(-1, 1)
    pos = positions.astype(F32).reshape(bsz, 1, seq)
    seg = _block_diag(jnp.ones((LANES // RWKV_HEAD_DIM, RWKV_HEAD_DIM, RWKV_HEAD_DIM), BF16))
    assert 2 * RWKV_HEAD_DIM == LANES and wd % LANES == 0
    assert seq % EVEN_TILE == 0 and seq % ODD_TILE == 0 and m % POST_ROWS == 0

    for i in range(depth):
        j = i // 2
        if i % 2 == 0:
            w_in = ev_w_in[j]
            cols = w_in.shape[1]
            cols_pad = -(-cols // LANES) * LANES
            w_in = jnp.pad(w_in, ((0, 0), (0, cols_pad - cols))).astype(BF16)
            wgate = jnp.concatenate([_block_diag(ev_lru_wr[j]), _block_diag(ev_lru_wi[j])], axis=1).astype(BF16)
            bgate = jnp.concatenate([ev_lru_br[j], ev_lru_bi[j]]).reshape(1, -1)
            y = _even_mixer(h.reshape(bsz, seq, d), row(mix_norm[i]), w_in, ev_conv_w[j], row(ev_conv_b[j]), wgate, bgate,
                            row(ev_lru_a[j]), ev_gla_wgk[j], row(ev_gla_bgk[j]), row(ev_gla_norm[j]),
                            nb=nb, ts=EVEN_TILE, wa=wa, dk=dk, dv=dv, rank=rank)
            w_out = ev_w_out[j]
        else:
            y = _odd_mixer(h.reshape(bsz, seq, d), pos, row(mix_norm[i]), od_w_in[j].astype(BF16), freq,
                           row(od_ret_norm[j]), row(od_rwkv_mu[j]),
                           row(od_rwkv_w0[j]), od_rwkv_ww2[j].astype(BF16), row(od_rwkv_a0[j]),
                           od_rwkv_aw2[j].astype(BF16), od_rwkv_gw2[j].astype(BF16), row(od_rwkv_kk[j]),
                           row(od_rwkv_ka[j]), row(od_rwkv_rk[j]), row(od_rwkv_norm[j]), seg,
                           nb=nb, ts=ODD_TILE, wc=wc, wd=wd, rw=rw, ra=ra, rg=rg)
            w_out = od_w_out[j]
        h = _post(h, y.reshape(m, -1), p.reshape(depth, m, -1), w_out.astype(BF16), rows3(ffn_norm),
                  ffn_w_gate.astype(BF16), ffn_w_up.astype(BF16), ffn_w_down.astype(BF16),
                  rows3(ple_norm), ple_w_gate.astype(BF16), rows3(ple_b_gate), ple_w_proj.astype(BF16),
                  row(final_norm), layer=i, tm=POST_ROWS, final=(i == depth - 1))
    return h.reshape(bsz, seq, d)
```

```python
import functools
import math

import jax
import jax.numpy as jnp
from jax import lax
from jax.experimental import pallas as pl
from jax.experimental.pallas import tpu as pltpu

F32 = jnp.float32
BF16 = jnp.bfloat16

NORM_EPS = 1e-6
LANES = 128
SUBLANES = 8
CONV_W = 4
LRU_C = 8.0
GLA_HEADS = 4
GLA_GATE_NORM = 16.0
GLA_SUB = 16
RET_HEADS = 4
RET_CHUNK = 128
ROPE_BASE = 10000.0
RWKV_HEAD_DIM = 64
RWKV_LN_EPS = 64e-5
HEAD_NORM_EPS = 1e-5

MIX_CHUNK = 64
MIX_BATCH = 4
EVEN_TILE = 4 * MIX_CHUNK
ODD_TILE = RET_CHUNK
POST_ROWS = 512
VMEM_LIMIT = 56 * 1024 * 1024


def _log_sigmoid(x):
    return jnp.minimum(x, 0.0) - jnp.log1p(jnp.exp(-jnp.abs(x)))


def _sigmoid(x):
    return 0.5 * jnp.tanh(0.5 * x) + 0.5


def _silu(x):
    return x * _sigmoid(x)


def _gelu_tanh(x):
    return 0.5 * x * (1.0 + jnp.tanh(math.sqrt(2.0 / math.pi) * (x + 0.044715 * (x * x * x))))


def _rms(x, g):
    return x * lax.rsqrt(jnp.mean(x * x, axis=-1, keepdims=True) + NORM_EPS) * g


def _dot(a, b, precision=None):
    return jnp.dot(a, b, preferred_element_type=F32, precision=precision)


def _dot_nt(a, b, precision=None):
    return lax.dot_general(a, b, (((1,), (1,)), ((), ())), preferred_element_type=F32, precision=precision)


def _dot_tn(a, b, precision=None):
    return lax.dot_general(a, b, (((0,), (0,)), ((), ())), preferred_element_type=F32, precision=precision)


def _head_sums(x, seg):
    return jnp.concatenate([_dot(x[:, g:g + LANES].astype(BF16), seg) for g in range(0, x.shape[1], LANES)], axis=-1)


def _dot_split_rhs(m, x, passes):
    acc = None
    for _ in range(passes):
        piece = x.astype(BF16)
        acc = _dot(m, piece) if acc is None else acc + _dot(m, piece)
        x = x - piece.astype(F32)
    return acc


def _tri(n, strict=False):
    row = lax.broadcasted_iota(jnp.int32, (n, n), 0)
    col = lax.broadcasted_iota(jnp.int32, (n, n), 1)
    return (row > col) if strict else (row >= col)


PROJ_GROUP = 512


def _project(h_ref, mnorm_ref, win_ref, proj_ref):
    nb, ts, d = h_ref.shape
    xn = _rms(h_ref[...].reshape(nb * ts, d), mnorm_ref[...]).astype(BF16)
    cols = proj_ref.shape[2]
    for c0 in range(0, cols, PROJ_GROUP):
        c1 = min(c0 + PROJ_GROUP, cols)
        proj_ref[:, :, c0:c1] = _dot(xn, win_ref[:, c0:c1]).reshape(nb, ts, c1 - c0)


def _resident(t):
    return pl.BlockSpec(t.shape, lambda b, s: (0,) * t.ndim, pipeline_mode=pl.Buffered(1))


def _rg_lru(proj_ref, y_ref, xtail_ref, hcar_ref, convw_ref, convb_ref, wgate_ref, bgate_ref, lrua_ref, *, nb, ts, wa):
    cw = convw_ref[...]
    xcs = []
    for b in range(nb):
        ax = proj_ref[b, :, 0:wa]
        xcat = jnp.concatenate([xtail_ref[b], ax], axis=0)
        xtail_ref[b] = ax[ts - 8:ts]
        xc = convb_ref[...] + cw[CONV_W - 1:CONV_W] * ax
        for s in range(1, CONV_W):
            xc = xc + cw[CONV_W - 1 - s:CONV_W - s] * xcat[8 - s:8 - s + ts]
        xcs.append(xc)
    gates_all = _dot(jnp.concatenate(xcs, axis=0).astype(BF16), wgate_ref[...]) + bgate_ref[...]
    log_sig_a = _log_sigmoid(lrua_ref[...])
    groups = ts // SUBLANES
    row = lax.broadcasted_iota(jnp.int32, (groups, SUBLANES, wa), 1)
    for b in range(nb):
        xc = xcs[b]
        gates = gates_all[b * ts:(b + 1) * ts]
        r = _sigmoid(gates[:, :wa])
        i = _sigmoid(gates[:, wa:])
        log_a = LRU_C * r * log_sig_a
        a = jnp.exp(log_a)
        t = jnp.tanh(log_a)
        s2 = -2.0 * t / (1.0 - t)
        u = jnp.where(s2 > 0.0, s2 * lax.rsqrt(s2), 0.0) * (i * xc)
        u = u.reshape(groups, SUBLANES, wa)
        a = a.reshape(groups, SUBLANES, wa)
        d = 1
        while d < SUBLANES:
            keep = row >= d
            u = jnp.where(keep, a * pltpu.roll(u, d, 1) + u, u)
            a = jnp.where(keep, a * pltpu.roll(a, d, 1), a)
            d *= 2
        carry = hcar_ref[b]
        pieces = []
        for j in range(groups):
            hj = u[j] + a[j] * carry
            carry = hj[SUBLANES - 1:SUBLANES]
            pieces.append(hj)
        hcar_ref[b] = carry
        h = jnp.concatenate(pieces, axis=0)
        y_ref[b, :, 0:wa] = (h * _gelu_tanh(proj_ref[b, :, wa:2 * wa])).astype(y_ref.dtype)


def _even_mixer_kernel(h_ref, mnorm_ref, win_ref, convw_ref, convb_ref, wgate_ref, bgate_ref, lrua_ref,
                       wgk_ref, bgk_ref, gnorm_ref, y_ref, proj_ref, xtail_ref, hcar_ref, st_ref,
                       *, nb, ts, wa, dk, dv, rank):
    heads = GLA_HEADS
    wq = heads * dk
    wb = heads * dv
    o_q = 2 * wa
    o_k, o_v = o_q + wq, o_q + 2 * wq
    o_g, o_gk = o_v + wb, o_v + 2 * wb
    step = pl.program_id(1)

    @pl.when(step == 0)
    def _():
        xtail_ref[...] = jnp.zeros_like(xtail_ref)
        hcar_ref[...] = jnp.zeros_like(hcar_ref)
        st_ref[...] = jnp.zeros_like(st_ref)

    _project(h_ref, mnorm_ref, win_ref, proj_ref)
    _rg_lru(proj_ref, y_ref, xtail_ref, hcar_ref, convw_ref, convb_ref, wgate_ref, bgate_ref, lrua_ref,
            nb=nb, ts=ts, wa=wa)

    cg = MIX_CHUNK
    tri_incl = jnp.where(_tri(cg), 1.0, 0.0).astype(BF16)
    causal = _tri(cg)
    rowc = lax.broadcasted_iota(jnp.int32, (cg, wq), 0)
    items = []
    gk = proj_ref[:, :, o_gk:o_gk + rank].reshape(nb * ts, rank)
    gk_hi = gk.astype(BF16)
    gk_lo = (gk - gk_hi.astype(F32)).astype(BF16)
    w_hi = wgk_ref[...].astype(BF16)
    w_lo = (wgk_ref[...] - w_hi.astype(F32)).astype(BF16)
    z_all = _dot(gk_hi, w_hi) + (_dot(gk_lo, w_hi) + _dot(gk_hi, w_lo)) + bgk_ref[...]
    log_f_all = _log_sigmoid(z_all) * (1.0 / GLA_GATE_NORM)
    for b in range(nb):
        for c in range(ts // cg):
            r0 = c * cg
            cum = _dot_split_rhs(tri_incl, log_f_all[b * ts + r0:b * ts + r0 + cg], 2)
            q = proj_ref[b, r0:r0 + cg, o_q:o_q + wq] * (dk ** -0.5)
            k = proj_ref[b, r0:r0 + cg, o_k:o_k + wq]
            v = proj_ref[b, r0:r0 + cg, o_v:o_v + wb].astype(BF16)
            cum_last = cum[cg - 1:cg]
            it = dict(b=b, c=c, v=v, q_in=(q * jnp.exp(cum)).astype(BF16),
                      k_out=(k * jnp.exp(cum_last - cum)).astype(BF16), chunk_decay=jnp.exp(cum_last), qb=[], kb=[])
            for ib in range(cg // GLA_SUB):
                b0 = ib * GLA_SUB
                base = cum[b0 - 1:b0] if ib > 0 else jnp.zeros((1, wq), F32)
                it["qb"].append((q[b0:b0 + GLA_SUB] * jnp.exp(cum[b0:b0 + GLA_SUB] - base)).astype(BF16))
                it["kb"].append(jnp.where(rowc < b0 + GLA_SUB, k * jnp.exp(base - cum), 0.0).astype(BF16))
            items.append(it)
    for it in items:
        it["scores"] = [
            jnp.where(causal, jnp.concatenate(
                [_dot_nt(qb[:, hd * dk:(hd + 1) * dk], kb[:, hd * dk:(hd + 1) * dk])
                 for qb, kb in zip(it["qb"], it["kb"])], axis=0), 0.0).astype(BF16)
            for hd in range(heads)]
    for it in items:
        it["intra"] = [_dot(it["scores"][hd], it["v"][:, hd * dv:(hd + 1) * dv]) for hd in range(heads)]
    for c in range(ts // cg):
        r0 = c * cg
        for it in [t for t in items if t["c"] == c]:
            b = it["b"]
            outs = []
            for hd in range(heads):
                ks = slice(hd * dk, (hd + 1) * dk)
                vs = slice(hd * dv, (hd + 1) * dv)
                st = st_ref[b * heads + hd]
                o = it["intra"][hd] + _dot_nt(it["q_in"][:, ks], st.astype(BF16))
                st_ref[b * heads + hd] = st * it["chunk_decay"][:, ks] + _dot_tn(it["v"][:, vs], it["k_out"][:, ks])
                outs.append(o * lax.rsqrt(jnp.mean(o * o, axis=-1, keepdims=True) + HEAD_NORM_EPS))
            o_all = jnp.concatenate(outs, axis=-1) * gnorm_ref[...]
            g = proj_ref[b, r0:r0 + cg, o_g:o_g + wb]
            y_ref[b, r0:r0 + cg, wa:wa + wb] = (o_all * _silu(g)).astype(y_ref.dtype)


def _even_mixer(h, mnorm, win, convw, convb, wgate, bgate, lrua, wgk, bgk, gnorm, *, nb, ts, wa, dk, dv, rank):
    bsz, seq, d = h.shape
    cols = win.shape[1]
    heads = GLA_HEADS
    wb = heads * dv
    consts = (mnorm, win, convw, convb, wgate, bgate, lrua, wgk, bgk, gnorm)
    kern = functools.partial(_even_mixer_kernel, nb=nb, ts=ts, wa=wa, dk=dk, dv=dv, rank=rank)
    return pl.pallas_call(
        kern,
        grid=(bsz // nb, seq // ts),
        in_specs=[pl.BlockSpec((nb, ts, d), lambda b, s: (b, s, 0))] + [_resident(t) for t in consts],
        out_specs=pl.BlockSpec((nb, ts, wa + wb), lambda b, s: (b, s, 0)),
        out_shape=jax.ShapeDtypeStruct((bsz, seq, wa + wb), BF16),
        scratch_shapes=[pltpu.VMEM((nb, ts, cols), F32), pltpu.VMEM((nb, 8, wa), F32),
                        pltpu.VMEM((nb, 1, wa), F32), pltpu.VMEM((nb * heads, dv, dk), F32)],
        compiler_params=pltpu.CompilerParams(dimension_semantics=("arbitrary", "arbitrary"),
                                             vmem_limit_bytes=VMEM_LIMIT),
        name="even_mixer",
    )(h, *consts)


def _odd_mixer_kernel(h_ref, pos_ref, mnorm_ref, win_ref, freq_ref, rnorm_ref, mu_ref, w0_ref, ww2_ref, a0_ref,
                      aw2_ref, gw2_ref, kk_ref, ka_ref, rk_ref, wnorm_ref, seg_ref,
                      y_ref, proj_ref, rstate_ref, dprev_ref, wstate_ref, *, nb, ts, wc, wd, rw, ra, rg):
    f32 = F32

    @pl.when(pl.program_id(1) == 0)
    def _():
        rstate_ref[...] = jnp.zeros_like(rstate_ref)
        dprev_ref[...] = jnp.zeros_like(dprev_ref)
        wstate_ref[...] = jnp.zeros_like(wstate_ref)

    _project(h_ref, mnorm_ref, win_ref, proj_ref)

    o_d = 4 * wc
    dcols = 3 * wd + rw + ra + rg
    hdim = RWKV_HEAD_DIM
    cm = MIX_CHUNK
    n = 2 * cm
    groups = wd // LANES
    seg = seg_ref[...]
    lane = lax.broadcasted_iota(jnp.int32, (cm, LANES), 1)
    first_head = lane < hdim

    def stack_heads(t):
        return jnp.concatenate([jnp.where(first_head, t, 0.0), jnp.where(first_head, 0.0, t)], axis=0)

    def split_heads(t):
        return jnp.concatenate([t[:, :hdim], t[:, hdim:]], axis=0)

    rin = lax.broadcasted_iota(jnp.int32, (n, n), 0)
    cin = lax.broadcasted_iota(jnp.int32, (n, n), 1)
    same_head = (rin < cm) == (cin < cm)
    strict = same_head & ((rin & (cm - 1)) > (cin & (cm - 1)))
    incl = same_head & ((rin & (cm - 1)) >= (cin & (cm - 1)))
    eye = jnp.where(rin == cin, 1.0, 0.0).astype(f32)
    tri_incl = jnp.where(_tri(cm), 1.0, 0.0).astype(BF16)

    work = []
    dparts = []
    for b in range(nb):
        dpart = proj_ref[b, :, o_d:o_d + dcols]
        dshift = jnp.concatenate([dprev_ref[b], dpart], axis=0)[7:7 + ts]
        dprev_ref[b] = dpart[ts - 8:ts]
        dparts.append(dpart + mu_ref[...] * (dshift - dpart))
    dall = jnp.concatenate(dparts, axis=0)
    r_all = dall[:, 0:wd]
    k_all = dall[:, wd:2 * wd]
    v_all = dall[:, 2 * wd:3 * wd]
    w_lr = dall[:, 3 * wd:3 * wd + rw]
    a_lr = dall[:, 3 * wd + rw:3 * wd + rw + ra]
    g_lr = dall[:, 3 * wd + rw + ra:dcols]
    log_decay_all = -math.exp(-0.5) * _sigmoid(w0_ref[...] + _dot(jnp.tanh(w_lr).astype(BF16), ww2_ref[...]))
    a_all = _sigmoid(a0_ref[...] + _dot(a_lr.astype(BF16), aw2_ref[...]))
    g_all = _dot(_sigmoid(g_lr).astype(BF16), gw2_ref[...])
    kk_all = k_all * kk_ref[...]
    k_all = k_all * (1.0 + (a_all - 1.0) * ka_ref[...])
    rows_all = nb * ts
    sums = _head_sums(jnp.concatenate([kk_all * kk_all, r_all * k_all * rk_ref[...]], axis=0), seg)
    kk_all = kk_all * lax.rsqrt(sums[:rows_all] + 1e-12)
    bonus_all = sums[rows_all:] * v_all
    kb_all = kk_all * a_all
    for b in range(nb):
        for ch in range(ts // cm):
            rs = slice(b * ts + ch * cm, b * ts + (ch + 1) * cm)
            ld = log_decay_all[rs]
            cum = _dot_split_rhs(tri_incl, ld, 2)
            cum_last = cum[cm - 1:cm]
            p_incl = jnp.exp(cum)
            p_inv = jnp.exp(-cum)
            a_t = -kk_all[rs] * jnp.exp(cum - ld)
            r_t = r_all[rs] * p_incl
            b_t = kb_all[rs] * p_inv
            k_t = k_all[rs] * p_inv
            p_out = jnp.exp(cum_last - cum)
            b_o = kb_all[rs] * p_out
            k_o = k_all[rs] * p_out
            p_last = jnp.exp(cum_last)
            for pr in range(groups):
                ls = slice(pr * LANES, (pr + 1) * LANES)
                l4 = jnp.concatenate([stack_heads(a_t[:, ls]), stack_heads(r_t[:, ls])], axis=0).astype(BF16)
                lc = jnp.concatenate([split_heads(a_t[:, ls]), split_heads(r_t[:, ls])], axis=0).astype(BF16)
                rc = jnp.concatenate([split_heads(b_t[:, ls]), split_heads(k_t[:, ls])], axis=0).astype(BF16)
                o4 = jnp.concatenate([stack_heads(b_o[:, ls]), stack_heads(k_o[:, ls])], axis=0).astype(BF16)
                vbd = stack_heads(v_all[rs, ls]).astype(BF16)
                work.append(dict(b=b, ch=ch, pr=pr, l4=l4, lc=lc, rc=rc, o4=o4, vbd=vbd, p_last=p_last[:, ls]))
    for wk in work:
        big = _dot_nt(wk["lc"], wk["rc"])
        a_ab = jnp.where(strict, big[:n, :n], 0.0)
        wk["a_ak"] = jnp.where(strict, big[:n, n:], 0.0).astype(BF16)
        wk["a_r"] = jnp.concatenate([jnp.where(incl, big[n:, :n], 0.0), jnp.where(incl, big[n:, n:], 0.0)],
                                    axis=1).astype(BF16)
        wk["x"] = a_ab.astype(BF16)
        wk["tinv"] = eye + a_ab
    span = 2
    while span < cm:
        for wk in work:
            wk["x"] = _dot(wk["x"], wk["x"]).astype(BF16)
        for wk in work:
            wk["tinv"] = wk["tinv"] + _dot(wk["tinv"].astype(BF16), wk["x"])
        span *= 2

    c = ts
    dh = wc // RET_HEADS
    half = dh // 2
    rowi = lax.broadcasted_iota(jnp.int32, (c, c), 0)
    coli = lax.broadcasted_iota(jnp.int32, (c, c), 1)
    rel = (rowi - coli).astype(f32)
    rowd = lax.broadcasted_iota(jnp.int32, (c, dh), 0).astype(f32)
    ret = []
    log_gammas = [math.log1p(-2.0 ** (-5.0 - hd)) for hd in range(RET_HEADS)]
    dmasks = [jnp.where(rel >= 0.0, jnp.exp(jnp.maximum(rel, 0.0) * lg), 0.0) for lg in log_gammas]
    decay_in = [jnp.exp((rowd + 1.0) * lg) for lg in log_gammas]
    decay_out = [jnp.exp((c - 1.0 - rowd) * lg) for lg in log_gammas]
    for b in range(nb):
        ang = freq_ref[...] * pos_ref[b]
        cos, sin = jnp.cos(ang).T, jnp.sin(ang).T
        cos2 = jnp.concatenate([cos, cos], axis=-1)
        sin2 = jnp.concatenate([-sin, sin], axis=-1)
        for hd in range(RET_HEADS):
            q = proj_ref[b, :, hd * dh:(hd + 1) * dh]
            k = proj_ref[b, :, wc + hd * dh:wc + (hd + 1) * dh]
            v = proj_ref[b, :, 2 * wc + hd * dh:2 * wc + (hd + 1) * dh].astype(BF16)
            q = (q * cos2 + pltpu.roll(q, half, 1) * sin2) * (dh ** -0.5)
            k = k * cos2 + pltpu.roll(k, half, 1) * sin2
            ret.append(dict(b=b, hd=hd, v=v, q=q.astype(BF16), k=k.astype(BF16), dmask=dmasks[hd],
                            q_in=(q * decay_in[hd]).astype(BF16), k_out=(k * decay_out[hd]).astype(BF16),
                            decay=math.exp(c * log_gammas[hd])))
    for rt in ret:
        rt["scores"] = (_dot_nt(rt["q"], rt["k"]) * rt["dmask"]).astype(BF16)
    for rt in ret:
        st = rstate_ref[rt["b"] * RET_HEADS + rt["hd"]]
        o = _dot(rt["scores"], rt["v"]) + _dot(rt["q_in"], st.astype(BF16))
        rstate_ref[rt["b"] * RET_HEADS + rt["hd"]] = st * rt["decay"] + _dot_tn(rt["k_out"], rt["v"])
        o = o - jnp.mean(o, axis=-1, keepdims=True)
        rt["o"] = o * lax.rsqrt(jnp.mean(o * o, axis=-1, keepdims=True) + HEAD_NORM_EPS)
    for b in range(nb):
        g = proj_ref[b, :, 3 * wc:4 * wc]
        o_all = jnp.concatenate([rt["o"] for rt in ret if rt["b"] == b], axis=-1)
        y_ref[b, :, 0:wc] = (o_all * rnorm_ref[...] * _silu(g)).astype(y_ref.dtype)

    for ch in range(ts // cm):
        wks = [wk for wk in work if wk["ch"] == ch]
        sidx = [wk["b"] * groups + wk["pr"] for wk in wks]
        sts = [wstate_ref[i] for i in sidx]
        fss = [_dot_nt(wk["l4"], st.astype(BF16)) for wk, st in zip(wks, sts)]
        rhs = [(fs[:n] + _dot(wk["a_ak"], wk["vbd"])).astype(BF16) for wk, fs in zip(wks, fss)]
        us = [_dot(wk["tinv"].astype(BF16), rh) for wk, rh in zip(wks, rhs)]
        uvs = [jnp.concatenate([u.astype(BF16), wk["vbd"]], axis=0) for wk, u in zip(wks, us)]
        for i, st, wk, uv in zip(sidx, sts, wks, uvs):
            wstate_ref[i] = st * wk["p_last"] + _dot_tn(uv, wk["o4"])
        ybds = [fs[n:] + _dot(wk["a_r"], uv) for wk, fs, uv in zip(wks, fss, uvs)]
        y = jnp.concatenate(
            [jnp.concatenate([yb[:cm] + yb[cm:] for wk, yb in zip(wks, ybds) if wk["b"] == b], axis=-1)
             for b in range(nb)], axis=0)
        inv_h = 1.0 / hdim
        y = y - _head_sums(y, seg) * inv_h
        y = y * lax.rsqrt(_head_sums(y * y, seg) * inv_h + RWKV_LN_EPS)
        for b in range(nb):
            tile_rows = slice(b * ts + ch * cm, b * ts + (ch + 1) * cm)
            yb = (y[b * cm:(b + 1) * cm] * wnorm_ref[...] + bonus_all[tile_rows]) * g_all[tile_rows]
            y_ref[b, ch * cm:(ch + 1) * cm, wc:wc + wd] = yb.astype(y_ref.dtype)


def _odd_mixer(h, pos, mnorm, win, freq, rnorm, mu, w0, ww2, a0, aw2, gw2, kk, ka, rk, wnorm, seg, *, nb, ts, wc, wd, rw,
               ra, rg):
    bsz, seq, d = h.shape
    cols = win.shape[1]
    dh = wc // RET_HEADS
    consts = (mnorm, win, freq, rnorm, mu, w0, ww2, a0, aw2, gw2, kk, ka, rk, wnorm, seg)
    kern = functools.partial(_odd_mixer_kernel, nb=nb, ts=ts, wc=wc, wd=wd, rw=rw, ra=ra, rg=rg)
    dcols = 3 * wd + rw + ra + rg
    return pl.pallas_call(
        kern,
        grid=(bsz // nb, seq // ts),
        in_specs=[pl.BlockSpec((nb, ts, d), lambda b, s: (b, s, 0)),
                  pl.BlockSpec((nb, 1, ts), lambda b, s: (b, 0, s))] + [_resident(t) for t in consts],
        out_specs=pl.BlockSpec((nb, ts, wc + wd), lambda b, s: (b, s, 0)),
        out_shape=jax.ShapeDtypeStruct((bsz, seq, wc + wd), BF16),
        scratch_shapes=[pltpu.VMEM((nb, ts, cols), F32), pltpu.VMEM((nb * RET_HEADS, dh, dh), F32),
                        pltpu.VMEM((nb, 8, dcols), F32), pltpu.VMEM((nb * (wd // LANES), LANES, LANES), F32)],
        compiler_params=pltpu.CompilerParams(dimension_semantics=("arbitrary", "arbitrary"),
                                             vmem_limit_bytes=VMEM_LIMIT),
        name="odd_mixer",
    )(h, pos, *consts)


def _post_kernel(h_ref, y_ref, p_ref, wout_ref, fnorm_ref, wg_ref, wu_ref, wd_ref, pnorm_ref, pwg_ref, pbg_ref,
                 pwp_ref, onorm_ref, o_ref, *, final):
    half = h_ref.shape[0] // 2
    starts = (0, half)
    hs, xs = [], []
    for r0 in starts:
        h = h_ref[r0:r0 + half] + _dot(y_ref[r0:r0 + half], wout_ref[...])
        hs.append(h)
        xs.append(_rms(h, fnorm_ref[...]).astype(BF16))
    acts = [(_silu(_dot(xn, wg_ref[...])) * _dot(xn, wu_ref[...])).astype(BF16) for xn in xs]
    hs = [h + _dot(act, wd_ref[...]) for h, act in zip(hs, acts)]
    xgs = [_rms(h, pnorm_ref[...]).astype(BF16) for h in hs]
    for r0, h, xg in zip(starts, hs, xgs):
        gate = _sigmoid(_dot(xg, pwg_ref[...]) + pbg_ref[...])
        h = h + gate * _dot(p_ref[r0:r0 + half].astype(BF16), pwp_ref[...])
        if final:
            h = _rms(h, onorm_ref[...])
        o_ref[r0:r0 + half] = h


def _post(h, y, p, wout, fnorm, wg, wu, wd, pnorm, pwg, pbg, pwp, onorm, *, layer, tm, final):
    m, d = h.shape
    rows = lambda width: pl.BlockSpec((tm, width), lambda i: (i, 0))
    resident = lambda t: pl.BlockSpec(t.shape, lambda i: (0, 0), pipeline_mode=pl.Buffered(1))
    of_layer = lambda t: pl.BlockSpec((None,) + t.shape[1:], lambda i: (layer, 0, 0), pipeline_mode=pl.Buffered(1))
    stacked = (fnorm, wg, wu, wd, pnorm, pwg, pbg, pwp)
    return pl.pallas_call(
        functools.partial(_post_kernel, final=final),
        grid=(m // tm,),
        in_specs=[rows(d), rows(y.shape[1]), pl.BlockSpec((None, tm, p.shape[2]), lambda i: (layer, i, 0)),
                  resident(wout)] + [of_layer(t) for t in stacked] + [resident(onorm)],
        out_specs=rows(d),
        out_shape=jax.ShapeDtypeStruct((m, d), F32),
        compiler_params=pltpu.CompilerParams(dimension_semantics=("arbitrary",), vmem_limit_bytes=VMEM_LIMIT),
        name="post_mixer",
    )(h, y, p, wout, *stacked, onorm)


def _block_diag(blocks):
    g, bi, bj = blocks.shape
    eye = jnp.eye(g, dtype=blocks.dtype)
    return (eye[:, None, :, None] * blocks[:, :, None, :]).reshape(g * bi, g * bj)


def kernel(x, p, positions, ev_w_in, ev_conv_w, ev_conv_b, ev_lru_wr, ev_lru_br, ev_lru_wi, ev_lru_bi, ev_lru_a, ev_gla_wgk, ev_gla_bgk, ev_gla_norm, ev_w_out, od_w_in, od_ret_norm, od_rwkv_mu, od_rwkv_w0, od_rwkv_ww2, od_rwkv_a0, od_rwkv_aw2, od_rwkv_gw2, od_rwkv_kk, od_rwkv_ka, od_rwkv_rk, od_rwkv_norm, od_w_out, mix_norm, ffn_norm, ffn_w_gate, ffn_w_up, ffn_w_down, ple_norm, ple_w_gate, ple_b_gate, ple_w_proj, final_norm):
    bsz, seq, d = x.shape
    depth = p.shape[0]
    m = bsz * seq
    row = lambda t: t.reshape(1, -1).astype(F32)
    rows3 = lambda t: t.reshape(t.shape[0], 1, -1).astype(F32)

    wa = ev_conv_w.shape[-1]
    rank = ev_gla_wgk.shape[1]
    wq = ev_gla_wgk.shape[2]
    wb = ev_gla_norm.shape[-1]
    dk, dv = wq // GLA_HEADS, wb // GLA_HEADS
    wc = od_ret_norm.shape[-1]
    wd = od_rwkv_w0.shape[-1]
    rw, ra, rg = od_rwkv_ww2.shape[1], od_rwkv_aw2.shape[1], od_rwkv_gw2.shape[1]
    nb = MIX_BATCH if bsz % MIX_BATCH == 0 else 1

    h = x.reshape(m, d)
    freq = (ROPE_BASE ** (-jnp.arange(wc // RET_HEADS // 2, dtype=F32) / (wc // RET_HEADS // 2))).reshape(-1, 1)
    pos = positions.astype(F32).reshape(bsz, 1, seq)
    seg = _block_diag(jnp.ones((LANES // RWKV_HEAD_DIM, RWKV_HEAD_DIM, RWKV_HEAD_DIM), BF16))
    assert 2 * RWKV_HEAD_DIM == LANES and wd % LANES == 0
    assert seq % EVEN_TILE == 0 and seq % ODD_TILE == 0 and m % POST_ROWS == 0

    for i in range(depth):
        j = i // 2
        if i % 2 == 0:
            w_in = ev_w_in[j]
            cols = w_in.shape[1]
            cols_pad = -(-cols // LANES) * LANES
            w_in = jnp.pad(w_in, ((0, 0), (0, cols_pad - cols))).astype(BF16)
            wgate = jnp.concatenate([_block_diag(ev_lru_wr[j]), _block_diag(ev_lru_wi[j])], axis=1).astype(BF16)
            bgate = jnp.concatenate([ev_lru_br[j], ev_lru_bi[j]]).reshape(1, -1)
            y = _even_mixer(h.reshape(bsz, seq, d), row(mix_norm[i]), w_in, ev_conv_w[j], row(ev_conv_b[j]), wgate, bgate,
                            row(ev_lru_a[j]), ev_gla_wgk[j], row(ev_gla_bgk[j]), row(ev_gla_norm[j]),
                            nb=nb, ts=EVEN_TILE, wa=wa, dk=dk, dv=dv, rank=rank)
            w_out = ev_w_out[j]
        else:
            y = _odd_mixer(h.reshape(bsz, seq, d), pos, row(mix_norm[i]), od_w_in[j].astype(BF16), freq,
                           row(od_ret_norm[j]), row(od_rwkv_mu[j]),
                           row(od_rwkv_w0[j]), od_rwkv_ww2[j].astype(BF16), row(od_rwkv_a0[j]),
                           od_rwkv_aw2[j].astype(BF16), od_rwkv_gw2[j].astype(BF16), row(od_rwkv_kk[j]),
                           row(od_rwkv_ka[j]), row(od_rwkv_rk[j]), row(od_rwkv_norm[j]), seg,
                           nb=nb, ts=ODD_TILE, wc=wc, wd=wd, rw=rw, ra=ra, rg=rg)
            w_out = od_w_out[j]
        h = _post(h, y.reshape(m, -1), p.reshape(depth, m, -1), w_out.astype(BF16), rows3(ffn_norm),
                  ffn_w_gate.astype(BF16), ffn_w_up.astype(BF16), ffn_w_down.astype(BF16),
                  rows3(ple_norm), ple_w_gate.astype(BF16), rows3(ple_b_gate), ple_w_proj.astype(BF16),
                  row(final_norm), layer=i, tm=POST_ROWS, final=(i == depth - 1))
    return h.reshape(bsz, seq, d)
```

```python
import functools
import math

import jax
import jax.numpy as jnp
from jax import lax
from jax.experimental import pallas as pl
from jax.experimental.pallas import tpu as pltpu

F32 = jnp.float32
BF16 = jnp.bfloat16

NORM_EPS = 1e-6
LANES = 128
SUBLANES = 8
CONV_W = 4
LRU_C = 8.0
GLA_HEADS = 4
GLA_GATE_NORM = 16.0
GLA_SUB = 16
RET_HEADS = 4
RET_CHUNK = 128
ROPE_BASE = 10000.0
RWKV_HEAD_DIM = 64
RWKV_LN_EPS = 64e-5
HEAD_NORM_EPS = 1e-5

MIX_CHUNK = 64
MIX_BATCH = 4
EVEN_TILE = 4 * MIX_CHUNK
ODD_TILE = RET_CHUNK
POST_ROWS = 512
VMEM_LIMIT = 56 * 1024 * 1024


def _log_sigmoid(x):
    return jnp.minimum(x, 0.0) - jnp.log1p(jnp.exp(-jnp.abs(x)))


def _sigmoid(x):
    return 0.5 * jnp.tanh(0.5 * x) + 0.5


def _silu(x):
    return x * _sigmoid(x)


def _gelu_tanh(x):
    return 0.5 * x * (1.0 + jnp.tanh(math.sqrt(2.0 / math.pi) * (x + 0.044715 * (x * x * x))))


def _rms(x, g):
    return x * lax.rsqrt(jnp.mean(x * x, axis=-1, keepdims=True) + NORM_EPS) * g


def _dot(a, b, precision=None):
    return jnp.dot(a, b, preferred_element_type=F32, precision=precision)


def _dot_nt(a, b, precision=None):
    return lax.dot_general(a, b, (((1,), (1,)), ((), ())), preferred_element_type=F32, precision=precision)


def _dot_tn(a, b, precision=None):
    return lax.dot_general(a, b, (((0,), (0,)), ((), ())), preferred_element_type=F32, precision=precision)


def _head_sums(x, seg):
    return jnp.concatenate([_dot(x[:, g:g + LANES].astype(BF16), seg) for g in range(0, x.shape[1], LANES)], axis=-1)


def _dot_split_rhs(m, x, passes):
    acc = None
    for _ in range(passes):
        piece = x.astype(BF16)
        acc = _dot(m, piece) if acc is None else acc + _dot(m, piece)
        x = x - piece.astype(F32)
    return acc


def _tri(n, strict=False):
    row = lax.broadcasted_iota(jnp.int32, (n, n), 0)
    col = lax.broadcasted_iota(jnp.int32, (n, n), 1)
    return (row > col) if strict else (row >= col)


PROJ_GROUP = 512


def _project(h_ref, mnorm_ref, win_ref, proj_ref):
    nb, ts, d = h_ref.shape
    xn = _rms(h_ref[...].reshape(nb * ts, d), mnorm_ref[...]).astype(BF16)
    cols = proj_ref.shape[2]
    for c0 in range(0, cols, PROJ_GROUP):
        c1 = min(c0 + PROJ_GROUP, cols)
        proj_ref[:, :, c0:c1] = _dot(xn, win_ref[:, c0:c1]).reshape(nb, ts, c1 - c0)


def _out_project(o_ref, h_ref, y_ref, wout_ref, c0, c1, first):
    nb, ts, d = h_ref.shape
    part = _dot(y_ref[:, :, c0:c1].reshape(nb * ts, c1 - c0), wout_ref[c0:c1, :]).reshape(nb, ts, d)
    o_ref[...] = (h_ref[...] if first else o_ref[...]) + part


def _resident(t):
    return pl.BlockSpec(t.shape, lambda b, s: (0,) * t.ndim, pipeline_mode=pl.Buffered(1))


def _rg_lru(proj_ref, y_ref, xtail_ref, hcar_ref, convw_ref, convb_ref, wgate_ref, bgate_ref, lrua_ref, *, nb, ts, wa):
    cw = convw_ref[...]
    xcs = []
    for b in range(nb):
        ax = proj_ref[b, :, 0:wa]
        xcat = jnp.concatenate([xtail_ref[b], ax], axis=0)
        xtail_ref[b] = ax[ts - 8:ts]
        xc = convb_ref[...] + cw[CONV_W - 1:CONV_W] * ax
        for s in range(1, CONV_W):
            xc = xc + cw[CONV_W - 1 - s:CONV_W - s] * xcat[8 - s:8 - s + ts]
        xcs.append(xc)
    gates_all = _dot(jnp.concatenate(xcs, axis=0).astype(BF16), wgate_ref[...]) + bgate_ref[...]
    log_sig_a = _log_sigmoid(lrua_ref[...])
    groups = ts // SUBLANES
    row = lax.broadcasted_iota(jnp.int32, (groups, SUBLANES, wa), 1)
    for b in range(nb):
        xc = xcs[b]
        gates = gates_all[b * ts:(b + 1) * ts]
        r = _sigmoid(gates[:, :wa])
        i = _sigmoid(gates[:, wa:])
        log_a = LRU_C * r * log_sig_a
        a = jnp.exp(log_a)
        t = jnp.tanh(log_a)
        s2 = -2.0 * t / (1.0 - t)
        u = jnp.where(s2 > 0.0, s2 * lax.rsqrt(s2), 0.0) * (i * xc)
        u = u.reshape(groups, SUBLANES, wa)
        a = a.reshape(groups, SUBLANES, wa)
        d = 1
        while d < SUBLANES:
            keep = row >= d
            u = jnp.where(keep, a * pltpu.roll(u, d, 1) + u, u)
            a = jnp.where(keep, a * pltpu.roll(a, d, 1), a)
            d *= 2
        carry = hcar_ref[b]
        pieces = []
        for j in range(groups):
            hj = u[j] + a[j] * carry
            carry = hj[SUBLANES - 1:SUBLANES]
            pieces.append(hj)
        hcar_ref[b] = carry
        h = jnp.concatenate(pieces, axis=0)
        y_ref[b, :, 0:wa] = (h * _gelu_tanh(proj_ref[b, :, wa:2 * wa])).astype(y_ref.dtype)


def _even_mixer_kernel(h_ref, mnorm_ref, win_ref, convw_ref, convb_ref, wgate_ref, bgate_ref, lrua_ref,
                       wgk_ref, bgk_ref, gnorm_ref, y_ref, proj_ref, xtail_ref, hcar_ref, st_ref,
                       *, nb, ts, wa, dk, dv, rank):
    heads = GLA_HEADS
    wq = heads * dk
    wb = heads * dv
    o_q = 2 * wa
    o_k, o_v = o_q + wq, o_q + 2 * wq
    o_g, o_gk = o_v + wb, o_v + 2 * wb
    step = pl.program_id(1)

    @pl.when(step == 0)
    def _():
        xtail_ref[...] = jnp.zeros_like(xtail_ref)
        hcar_ref[...] = jnp.zeros_like(hcar_ref)
        st_ref[...] = jnp.zeros_like(st_ref)

    _project(h_ref, mnorm_ref, win_ref, proj_ref)
    _rg_lru(proj_ref, y_ref, xtail_ref, hcar_ref, convw_ref, convb_ref, wgate_ref, bgate_ref, lrua_ref,
            nb=nb, ts=ts, wa=wa)

    cg = MIX_CHUNK
    tri_incl = jnp.where(_tri(cg), 1.0, 0.0).astype(BF16)
    causal = _tri(cg)
    rowc = lax.broadcasted_iota(jnp.int32, (cg, wq), 0)
    items = []
    gk = proj_ref[:, :, o_gk:o_gk + rank].reshape(nb * ts, rank)
    gk_hi = gk.astype(BF16)
    gk_lo = (gk - gk_hi.astype(F32)).astype(BF16)
    w_hi = wgk_ref[...].astype(BF16)
    w_lo = (wgk_ref[...] - w_hi.astype(F32)).astype(BF16)
    z_all = _dot(gk_hi, w_hi) + (_dot(gk_lo, w_hi) + _dot(gk_hi, w_lo)) + bgk_ref[...]
    log_f_all = _log_sigmoid(z_all) * (1.0 / GLA_GATE_NORM)
    for b in range(nb):
        for c in range(ts // cg):
            r0 = c * cg
            cum = _dot_split_rhs(tri_incl, log_f_all[b * ts + r0:b * ts + r0 + cg], 2)
            q = proj_ref[b, r0:r0 + cg, o_q:o_q + wq] * (dk ** -0.5)
            k = proj_ref[b, r0:r0 + cg, o_k:o_k + wq]
            v = proj_ref[b, r0:r0 + cg, o_v:o_v + wb].astype(BF16)
            cum_last = cum[cg - 1:cg]
            it = dict(b=b, c=c, v=v, q_in=(q * jnp.exp(cum)).astype(BF16),
                      k_out=(k * jnp.exp(cum_last - cum)).astype(BF16), chunk_decay=jnp.exp(cum_last), qb=[], kb=[])
            for ib in range(cg // GLA_SUB):
                b0 = ib * GLA_SUB
                base = cum[b0 - 1:b0] if ib > 0 else jnp.zeros((1, wq), F32)
                it["qb"].append((q[b0:b0 + GLA_SUB] * jnp.exp(cum[b0:b0 + GLA_SUB] - base)).astype(BF16))
                it["kb"].append(jnp.where(rowc < b0 + GLA_SUB, k * jnp.exp(base - cum), 0.0).astype(BF16))
            items.append(it)
    for it in items:
        it["scores"] = [
            jnp.where(causal, jnp.concatenate(
                [_dot_nt(qb[:, hd * dk:(hd + 1) * dk], kb[:, hd * dk:(hd + 1) * dk])
                 for qb, kb in zip(it["qb"], it["kb"])], axis=0), 0.0).astype(BF16)
            for hd in range(heads)]
    for it in items:
        it["intra"] = [_dot(it["scores"][hd], it["v"][:, hd * dv:(hd + 1) * dv]) for hd in range(heads)]
    for c in range(ts // cg):
        r0 = c * cg
        for it in [t for t in items if t["c"] == c]:
            b = it["b"]
            outs = []
            for hd in range(heads):
                ks = slice(hd * dk, (hd + 1) * dk)
                vs = slice(hd * dv, (hd + 1) * dv)
                st = st_ref[b * heads + hd]
                o = it["intra"][hd] + _dot_nt(it["q_in"][:, ks], st.astype(BF16))
                st_ref[b * heads + hd] = st * it["chunk_decay"][:, ks] + _dot_tn(it["v"][:, vs], it["k_out"][:, ks])
                outs.append(o * lax.rsqrt(jnp.mean(o * o, axis=-1, keepdims=True) + HEAD_NORM_EPS))
            o_all = jnp.concatenate(outs, axis=-1) * gnorm_ref[...]
            g = proj_ref[b, r0:r0 + cg, o_g:o_g + wb]
            y_ref[b, r0:r0 + cg, wa:wa + wb] = (o_all * _silu(g)).astype(y_ref.dtype)


def _even_mixer(h, mnorm, win, convw, convb, wgate, bgate, lrua, wgk, bgk, gnorm, *, nb, ts, wa, dk, dv, rank):
    bsz, seq, d = h.shape
    cols = win.shape[1]
    heads = GLA_HEADS
    wb = heads * dv
    consts = (mnorm, win, convw, convb, wgate, bgate, lrua, wgk, bgk, gnorm)
    kern = functools.partial(_even_mixer_kernel, nb=nb, ts=ts, wa=wa, dk=dk, dv=dv, rank=rank)
    return pl.pallas_call(
        kern,
        grid=(bsz // nb, seq // ts),
        in_specs=[pl.BlockSpec((nb, ts, d), lambda b, s: (b, s, 0))] + [_resident(t) for t in consts],
        out_specs=pl.BlockSpec((nb, ts, wa + wb), lambda b, s: (b, s, 0)),
        out_shape=jax.ShapeDtypeStruct((bsz, seq, wa + wb), BF16),
        scratch_shapes=[pltpu.VMEM((nb, ts, cols), F32), pltpu.VMEM((nb, 8, wa), F32),
                        pltpu.VMEM((nb, 1, wa), F32), pltpu.VMEM((nb * heads, dv, dk), F32)],
        compiler_params=pltpu.CompilerParams(dimension_semantics=("arbitrary", "arbitrary"),
                                             vmem_limit_bytes=VMEM_LIMIT),
        name="even_mixer",
    )(h, *consts)


def _odd_mixer_kernel(h_ref, pos_ref, mnorm_ref, win_ref, freq_ref, rnorm_ref, mu_ref, w0_ref, ww2_ref, a0_ref,
                      aw2_ref, gw2_ref, kk_ref, ka_ref, rk_ref, wnorm_ref, seg_ref, wout_ref,
                      o_ref, y_ref, proj_ref, rstate_ref, dprev_ref, wstate_ref, *, nb, ts, wc, wd, rw, ra, rg):
    f32 = F32

    @pl.when(pl.program_id(1) == 0)
    def _():
        rstate_ref[...] = jnp.zeros_like(rstate_ref)
        dprev_ref[...] = jnp.zeros_like(dprev_ref)
        wstate_ref[...] = jnp.zeros_like(wstate_ref)

    _project(h_ref, mnorm_ref, win_ref, proj_ref)

    o_d = 4 * wc
    dcols = 3 * wd + rw + ra + rg
    hdim = RWKV_HEAD_DIM
    cm = MIX_CHUNK
    n = 2 * cm
    groups = wd // LANES
    seg = seg_ref[...]
    lane = lax.broadcasted_iota(jnp.int32, (cm, LANES), 1)
    first_head = lane < hdim

    def stack_heads(t):
        return jnp.concatenate([jnp.where(first_head, t, 0.0), jnp.where(first_head, 0.0, t)], axis=0)

    def split_heads(t):
        return jnp.concatenate([t[:, :hdim], t[:, hdim:]], axis=0)

    rin = lax.broadcasted_iota(jnp.int32, (n, n), 0)
    cin = lax.broadcasted_iota(jnp.int32, (n, n), 1)
    same_head = (rin < cm) == (cin < cm)
    strict = same_head & ((rin & (cm - 1)) > (cin & (cm - 1)))
    incl = same_head & ((rin & (cm - 1)) >= (cin & (cm - 1)))
    eye = jnp.where(rin == cin, 1.0, 0.0).astype(f32)
    tri_incl = jnp.where(_tri(cm), 1.0, 0.0).astype(BF16)

    work = []
    dparts = []
    for b in range(nb):
        dpart = proj_ref[b, :, o_d:o_d + dcols]
        dshift = jnp.concatenate([dprev_ref[b], dpart], axis=0)[7:7 + ts]
        dprev_ref[b] = dpart[ts - 8:ts]
        dparts.append(dpart + mu_ref[...] * (dshift - dpart))
    dall = jnp.concatenate(dparts, axis=0)
    r_all = dall[:, 0:wd]
    k_all = dall[:, wd:2 * wd]
    v_all = dall[:, 2 * wd:3 * wd]
    w_lr = dall[:, 3 * wd:3 * wd + rw]
    a_lr = dall[:, 3 * wd + rw:3 * wd + rw + ra]
    g_lr = dall[:, 3 * wd + rw + ra:dcols]
    log_decay_all = -math.exp(-0.5) * _sigmoid(w0_ref[...] + _dot(jnp.tanh(w_lr).astype(BF16), ww2_ref[...]))
    a_all = _sigmoid(a0_ref[...] + _dot(a_lr.astype(BF16), aw2_ref[...]))
    g_all = _dot(_sigmoid(g_lr).astype(BF16), gw2_ref[...])
    kk_all = k_all * kk_ref[...]
    k_all = k_all * (1.0 + (a_all - 1.0) * ka_ref[...])
    rows_all = nb * ts
    sums = _head_sums(jnp.concatenate([kk_all * kk_all, r_all * k_all * rk_ref[...]], axis=0), seg)
    kk_all = kk_all * lax.rsqrt(sums[:rows_all] + 1e-12)
    bonus_all = sums[rows_all:] * v_all
    kb_all = kk_all * a_all
    for b in range(nb):
        for ch in range(ts // cm):
            rs = slice(b * ts + ch * cm, b * ts + (ch + 1) * cm)
            ld = log_decay_all[rs]
            cum = _dot_split_rhs(tri_incl, ld, 2)
            cum_last = cum[cm - 1:cm]
            p_incl = jnp.exp(cum)
            p_inv = jnp.exp(-cum)
            a_t = -kk_all[rs] * jnp.exp(cum - ld)
            r_t = r_all[rs] * p_incl
            b_t = kb_all[rs] * p_inv
            k_t = k_all[rs] * p_inv
            p_out = jnp.exp(cum_last - cum)
            b_o = kb_all[rs] * p_out
            k_o = k_all[rs] * p_out
            p_last = jnp.exp(cum_last)
            for pr in range(groups):
                ls = slice(pr * LANES, (pr + 1) * LANES)
                l4 = jnp.concatenate([stack_heads(a_t[:, ls]), stack_heads(r_t[:, ls])], axis=0).astype(BF16)
                lc = jnp.concatenate([split_heads(a_t[:, ls]), split_heads(r_t[:, ls])], axis=0).astype(BF16)
                rc = jnp.concatenate([split_heads(b_t[:, ls]), split_heads(k_t[:, ls])], axis=0).astype(BF16)
                o4 = jnp.concatenate([stack_heads(b_o[:, ls]), stack_heads(k_o[:, ls])], axis=0).astype(BF16)
                vbd = stack_heads(v_all[rs, ls]).astype(BF16)
                work.append(dict(b=b, ch=ch, pr=pr, l4=l4, lc=lc, rc=rc, o4=o4, vbd=vbd, p_last=p_last[:, ls]))
    for wk in work:
        big = _dot_nt(wk["lc"], wk["rc"])
        a_ab = jnp.where(strict, big[:n, :n], 0.0)
        wk["a_ak"] = jnp.where(strict, big[:n, n:], 0.0).astype(BF16)
        wk["a_r"] = jnp.concatenate([jnp.where(incl, big[n:, :n], 0.0), jnp.where(incl, big[n:, n:], 0.0)],
                                    axis=1).astype(BF16)
        wk["x"] = a_ab.astype(BF16)
        wk["tinv"] = eye + a_ab
    span = 2
    while span < cm:
        for wk in work:
            wk["x"] = _dot(wk["x"], wk["x"]).astype(BF16)
        for wk in work:
            wk["tinv"] = wk["tinv"] + _dot(wk["tinv"].astype(BF16), wk["x"])
        span *= 2

    c = ts
    dh = wc // RET_HEADS
    half = dh // 2
    rowi = lax.broadcasted_iota(jnp.int32, (c, c), 0)
    coli = lax.broadcasted_iota(jnp.int32, (c, c), 1)
    rel = (rowi - coli).astype(f32)
    rowd = lax.broadcasted_iota(jnp.int32, (c, dh), 0).astype(f32)
    ret = []
    log_gammas = [math.log1p(-2.0 ** (-5.0 - hd)) for hd in range(RET_HEADS)]
    dmasks = [jnp.where(rel >= 0.0, jnp.exp(jnp.maximum(rel, 0.0) * lg), 0.0) for lg in log_gammas]
    decay_in = [jnp.exp((rowd + 1.0) * lg) for lg in log_gammas]
    decay_out = [jnp.exp((c - 1.0 - rowd) * lg) for lg in log_gammas]
    for b in range(nb):
        ang = freq_ref[...] * pos_ref[b]
        cos, sin = jnp.cos(ang).T, jnp.sin(ang).T
        cos2 = jnp.concatenate([cos, cos], axis=-1)
        sin2 = jnp.concatenate([-sin, sin], axis=-1)
        for hd in range(RET_HEADS):
            q = proj_ref[b, :, hd * dh:(hd + 1) * dh]
            k = proj_ref[b, :, wc + hd * dh:wc + (hd + 1) * dh]
            v = proj_ref[b, :, 2 * wc + hd * dh:2 * wc + (hd + 1) * dh].astype(BF16)
            q = (q * cos2 + pltpu.roll(q, half, 1) * sin2) * (dh ** -0.5)
            k = k * cos2 + pltpu.roll(k, half, 1) * sin2
            ret.append(dict(b=b, hd=hd, v=v, q=q.astype(BF16), k=k.astype(BF16), dmask=dmasks[hd],
                            q_in=(q * decay_in[hd]).astype(BF16), k_out=(k * decay_out[hd]).astype(BF16),
                            decay=math.exp(c * log_gammas[hd])))
    for rt in ret:
        rt["scores"] = (_dot_nt(rt["q"], rt["k"]) * rt["dmask"]).astype(BF16)
    for rt in ret:
        st = rstate_ref[rt["b"] * RET_HEADS + rt["hd"]]
        o = _dot(rt["scores"], rt["v"]) + _dot(rt["q_in"], st.astype(BF16))
        rstate_ref[rt["b"] * RET_HEADS + rt["hd"]] = st * rt["decay"] + _dot_tn(rt["k_out"], rt["v"])
        o = o - jnp.mean(o, axis=-1, keepdims=True)
        rt["o"] = o * lax.rsqrt(jnp.mean(o * o, axis=-1, keepdims=True) + HEAD_NORM_EPS)
    for b in range(nb):
        g = proj_ref[b, :, 3 * wc:4 * wc]
        o_all = jnp.concatenate([rt["o"] for rt in ret if rt["b"] == b], axis=-1)
        y_ref[b, :, 0:wc] = (o_all * rnorm_ref[...] * _silu(g)).astype(y_ref.dtype)
    _out_project(o_ref, h_ref, y_ref, wout_ref, 0, wc, first=True)

    for ch in range(ts // cm):
        wks = [wk for wk in work if wk["ch"] == ch]
        sidx = [wk["b"] * groups + wk["pr"] for wk in wks]
        sts = [wstate_ref[i] for i in sidx]
        fss = [_dot_nt(wk["l4"], st.astype(BF16)) for wk, st in zip(wks, sts)]
        rhs = [(fs[:n] + _dot(wk["a_ak"], wk["vbd"])).astype(BF16) for wk, fs in zip(wks, fss)]
        us = [_dot(wk["tinv"].astype(BF16), rh) for wk, rh in zip(wks, rhs)]
        uvs = [jnp.concatenate([u.astype(BF16), wk["vbd"]], axis=0) for wk, u in zip(wks, us)]
        for i, st, wk, uv in zip(sidx, sts, wks, uvs):
            wstate_ref[i] = st * wk["p_last"] + _dot_tn(uv, wk["o4"])
        ybds = [fs[n:] + _dot(wk["a_r"], uv) for wk, fs, uv in zip(wks, fss, uvs)]
        y = jnp.concatenate(
            [jnp.concatenate([yb[:cm] + yb[cm:] for wk, yb in zip(wks, ybds) if wk["b"] == b], axis=-1)
             for b in range(nb)], axis=0)
        inv_h = 1.0 / hdim
        y = y - _head_sums(y, seg) * inv_h
        y = y * lax.rsqrt(_head_sums(y * y, seg) * inv_h + RWKV_LN_EPS)
        for b in range(nb):
            tile_rows = slice(b * ts + ch * cm, b * ts + (ch + 1) * cm)
            yb = (y[b * cm:(b + 1) * cm] * wnorm_ref[...] + bonus_all[tile_rows]) * g_all[tile_rows]
            y_ref[b, ch * cm:(ch + 1) * cm, wc:wc + wd] = yb.astype(y_ref.dtype)
    _out_project(o_ref, h_ref, y_ref, wout_ref, wc, wc + wd, first=False)


def _odd_mixer(h, pos, mnorm, win, freq, rnorm, mu, w0, ww2, a0, aw2, gw2, kk, ka, rk, wnorm, seg, wout, *, nb, ts, wc, wd, rw,
               ra, rg):
    bsz, seq, d = h.shape
    cols = win.shape[1]
    dh = wc // RET_HEADS
    consts = (mnorm, win, freq, rnorm, mu, w0, ww2, a0, aw2, gw2, kk, ka, rk, wnorm, seg, wout)
    kern = functools.partial(_odd_mixer_kernel, nb=nb, ts=ts, wc=wc, wd=wd, rw=rw, ra=ra, rg=rg)
    dcols = 3 * wd + rw + ra + rg
    return pl.pallas_call(
        kern,
        grid=(bsz // nb, seq // ts),
        in_specs=[pl.BlockSpec((nb, ts, d), lambda b, s: (b, s, 0)),
                  pl.BlockSpec((nb, 1, ts), lambda b, s: (b, 0, s))] + [_resident(t) for t in consts],
        out_specs=pl.BlockSpec((nb, ts, d), lambda b, s: (b, s, 0)),
        out_shape=jax.ShapeDtypeStruct((bsz, seq, d), F32),
        scratch_shapes=[pltpu.VMEM((nb, ts, wc + wd), BF16), pltpu.VMEM((nb, ts, cols), F32), pltpu.VMEM((nb * RET_HEADS, dh, dh), F32),
                        pltpu.VMEM((nb, 8, dcols), F32), pltpu.VMEM((nb * (wd // LANES), LANES, LANES), F32)],
        compiler_params=pltpu.CompilerParams(dimension_semantics=("arbitrary", "arbitrary"),
                                             vmem_limit_bytes=VMEM_LIMIT),
        name="odd_mixer",
    )(h, pos, *consts)


def _post_kernel(*refs, final, with_out_proj):
    if with_out_proj:
        h_ref, y_ref, wout_ref, *refs = refs
    else:
        h_ref, *refs = refs
    p_ref, fnorm_ref, wg_ref, wu_ref, wd_ref, pnorm_ref, pwg_ref, pbg_ref, pwp_ref, onorm_ref, o_ref = refs
    half = h_ref.shape[0] // 2
    starts = (0, half)
    hs, xs = [], []
    for r0 in starts:
        h = h_ref[r0:r0 + half]
        if with_out_proj:
            h = h + _dot(y_ref[r0:r0 + half], wout_ref[...])
        hs.append(h)
        xs.append(_rms(h, fnorm_ref[...]).astype(BF16))
    acts = [(_silu(_dot(xn, wg_ref[...])) * _dot(xn, wu_ref[...])).astype(BF16) for xn in xs]
    hs = [h + _dot(act, wd_ref[...]) for h, act in zip(hs, acts)]
    xgs = [_rms(h, pnorm_ref[...]).astype(BF16) for h in hs]
    for r0, h, xg in zip(starts, hs, xgs):
        gate = _sigmoid(_dot(xg, pwg_ref[...]) + pbg_ref[...])
        h = h + gate * _dot(p_ref[r0:r0 + half].astype(BF16), pwp_ref[...])
        if final:
            h = _rms(h, onorm_ref[...])
        o_ref[r0:r0 + half] = h


def _post(h, y, wout, p, fnorm, wg, wu, wd, pnorm, pwg, pbg, pwp, onorm, *, layer, tm, final):
    m, d = h.shape
    rows = lambda width: pl.BlockSpec((tm, width), lambda i: (i, 0))
    resident = lambda t: pl.BlockSpec(t.shape, lambda i: (0, 0), pipeline_mode=pl.Buffered(1))
    of_layer = lambda t: pl.BlockSpec((None,) + t.shape[1:], lambda i: (layer, 0, 0), pipeline_mode=pl.Buffered(1))
    stacked = (fnorm, wg, wu, wd, pnorm, pwg, pbg, pwp)
    return pl.pallas_call(
        functools.partial(_post_kernel, final=final, with_out_proj=y is not None),
        grid=(m // tm,),
        in_specs=[rows(d)] + ([rows(y.shape[1]), resident(wout)] if y is not None else [])
        + [pl.BlockSpec((None, tm, p.shape[2]), lambda i: (layer, i, 0))]
        + [of_layer(t) for t in stacked] + [resident(onorm)],
        out_specs=rows(d),
        out_shape=jax.ShapeDtypeStruct((m, d), F32),
        compiler_params=pltpu.CompilerParams(dimension_semantics=("arbitrary",), vmem_limit_bytes=VMEM_LIMIT),
        name="post_mixer",
    )(h, *((y, wout) if y is not None else ()), p, *stacked, onorm)


def _block_diag(blocks):
    g, bi, bj = blocks.shape
    eye = jnp.eye(g, dtype=blocks.dtype)
    return (eye[:, None, :, None] * blocks[:, :, None, :]).reshape(g * bi, g * bj)


def kernel(x, p, positions, ev_w_in, ev_conv_w, ev_conv_b, ev_lru_wr, ev_lru_br, ev_lru_wi, ev_lru_bi, ev_lru_a, ev_gla_wgk, ev_gla_bgk, ev_gla_norm, ev_w_out, od_w_in, od_ret_norm, od_rwkv_mu, od_rwkv_w0, od_rwkv_ww2, od_rwkv_a0, od_rwkv_aw2, od_rwkv_gw2, od_rwkv_kk, od_rwkv_ka, od_rwkv_rk, od_rwkv_norm, od_w_out, mix_norm, ffn_norm, ffn_w_gate, ffn_w_up, ffn_w_down, ple_norm, ple_w_gate, ple_b_gate, ple_w_proj, final_norm):
    bsz, seq, d = x.shape
    depth = p.shape[0]
    m = bsz * seq
    row = lambda t: t.reshape(1, -1).astype(F32)
    rows3 = lambda t: t.reshape(t.shape[0], 1, -1).astype(F32)

    wa = ev_conv_w.shape[-1]
    rank = ev_gla_wgk.shape[1]
    wq = ev_gla_wgk.shape[2]
    wb = ev_gla_norm.shape[-1]
    dk, dv = wq // GLA_HEADS, wb // GLA_HEADS
    wc = od_ret_norm.shape[-1]
    wd = od_rwkv_w0.shape[-1]
    rw, ra, rg = od_rwkv_ww2.shape[1], od_rwkv_aw2.shape[1], od_rwkv_gw2.shape[1]
    nb = MIX_BATCH if bsz % MIX_BATCH == 0 else 1

    h = x.reshape(m, d)
    freq = (ROPE_BASE ** (-jnp.arange(wc // RET_HEADS // 2, dtype=F32) / (wc // RET_HEADS // 2))).reshape(-1, 1)
    pos = positions.astype(F32).reshape(bsz, 1, seq)
    seg = _block_diag(jnp.ones((LANES // RWKV_HEAD_DIM, RWKV_HEAD_DIM, RWKV_HEAD_DIM), BF16))
    assert 2 * RWKV_HEAD_DIM == LANES and wd % LANES == 0
    assert seq % EVEN_TILE == 0 and seq % ODD_TILE == 0 and m % POST_ROWS == 0

    for i in range(depth):
        j = i // 2
        if i % 2 == 0:
            w_in = ev_w_in[j]
            cols = w_in.shape[1]
            cols_pad = -(-cols // LANES) * LANES
            w_in = jnp.pad(w_in, ((0, 0), (0, cols_pad - cols))).astype(BF16)
            wgate = jnp.concatenate([_block_diag(ev_lru_wr[j]), _block_diag(ev_lru_wi[j])], axis=1).astype(BF16)
            bgate = jnp.concatenate([ev_lru_br[j], ev_lru_bi[j]]).reshape(1, -1)
            y = _even_mixer(h.reshape(bsz, seq, d), row(mix_norm[i]), w_in, ev_conv_w[j], row(ev_conv_b[j]), wgate, bgate,
                            row(ev_lru_a[j]), ev_gla_wgk[j], row(ev_gla_bgk[j]), row(ev_gla_norm[j]),
                            nb=nb, ts=EVEN_TILE, wa=wa, dk=dk, dv=dv, rank=rank)
            y, w_out = y.reshape(m, -1), ev_w_out[j].astype(BF16)
        else:
            h = _odd_mixer(h.reshape(bsz, seq, d), pos, row(mix_norm[i]), od_w_in[j].astype(BF16), freq,
                           row(od_ret_norm[j]), row(od_rwkv_mu[j]),
                           row(od_rwkv_w0[j]), od_rwkv_ww2[j].astype(BF16), row(od_rwkv_a0[j]),
                           od_rwkv_aw2[j].astype(BF16), od_rwkv_gw2[j].astype(BF16), row(od_rwkv_kk[j]),
                           row(od_rwkv_ka[j]), row(od_rwkv_rk[j]), row(od_rwkv_norm[j]), seg, od_w_out[j].astype(BF16),
                           nb=nb, ts=ODD_TILE, wc=wc, wd=wd, rw=rw, ra=ra, rg=rg)
            h, y, w_out = h.reshape(m, d), None, None
        h = _post(h, y, w_out, p.reshape(depth, m, -1), rows3(ffn_norm),
                  ffn_w_gate.astype(BF16), ffn_w_up.astype(BF16), ffn_w_down.astype(BF16),
                  rows3(ple_norm), ple_w_gate.astype(BF16), rows3(ple_b_gate), ple_w_proj.astype(BF16),
                  row(final_norm), layer=i, tm=POST_ROWS, final=(i == depth - 1))
    return h.reshape(bsz, seq, d)
```

```python
import functools
import math

import jax
import jax.numpy as jnp
from jax import lax
from jax.experimental import pallas as pl
from jax.experimental.pallas import tpu as pltpu

F32 = jnp.float32
BF16 = jnp.bfloat16

NORM_EPS = 1e-6
LANES = 128
SUBLANES = 8
CONV_W = 4
LRU_C = 8.0
GLA_HEADS = 4
GLA_GATE_NORM = 16.0
GLA_SUB = 16
RET_HEADS = 4
RET_CHUNK = 128
ROPE_BASE = 10000.0
RWKV_HEAD_DIM = 64
RWKV_LN_EPS = 64e-5
HEAD_NORM_EPS = 1e-5

MIX_CHUNK = 64
MIX_BATCH = 4
EVEN_TILE = 4 * MIX_CHUNK
ODD_TILE = RET_CHUNK
POST_ROWS = 512
VMEM_LIMIT = 56 * 1024 * 1024


def _log_sigmoid(x):
    return jnp.minimum(x, 0.0) - jnp.log1p(jnp.exp(-jnp.abs(x)))


def _sigmoid(x):
    return 0.5 * jnp.tanh(0.5 * x) + 0.5


def _silu(x):
    return x * _sigmoid(x)


def _gelu_tanh(x):
    return 0.5 * x * (1.0 + jnp.tanh(math.sqrt(2.0 / math.pi) * (x + 0.044715 * (x * x * x))))


def _rms(x, g):
    return x * lax.rsqrt(jnp.mean(x * x, axis=-1, keepdims=True) + NORM_EPS) * g


def _dot(a, b, precision=None):
    return jnp.dot(a, b, preferred_element_type=F32, precision=precision)


def _dot_nt(a, b, precision=None):
    return lax.dot_general(a, b, (((1,), (1,)), ((), ())), preferred_element_type=F32, precision=precision)


def _dot_tn(a, b, precision=None):
    return lax.dot_general(a, b, (((0,), (0,)), ((), ())), preferred_element_type=F32, precision=precision)


def _head_sums(x, seg):
    return jnp.concatenate([_dot(x[:, g:g + LANES].astype(BF16), seg) for g in range(0, x.shape[1], LANES)], axis=-1)


def _dot_split_rhs(m, x, passes):
    acc = None
    for _ in range(passes):
        piece = x.astype(BF16)
        acc = _dot(m, piece) if acc is None else acc + _dot(m, piece)
        x = x - piece.astype(F32)
    return acc


def _tri(n, strict=False):
    row = lax.broadcasted_iota(jnp.int32, (n, n), 0)
    col = lax.broadcasted_iota(jnp.int32, (n, n), 1)
    return (row > col) if strict else (row >= col)


PROJ_GROUP = 512


def _project(h_ref, mnorm_ref, win_ref, proj_ref):
    nb, ts, d = h_ref.shape
    xn = _rms(h_ref[...].reshape(nb * ts, d), mnorm_ref[...]).astype(BF16)
    cols = proj_ref.shape[2]
    for c0 in range(0, cols, PROJ_GROUP):
        c1 = min(c0 + PROJ_GROUP, cols)
        proj_ref[:, :, c0:c1] = _dot(xn, win_ref[:, c0:c1]).reshape(nb, ts, c1 - c0)


def _out_project(o_ref, h_ref, y_ref, wout_ref):
    nb, ts, d = h_ref.shape
    o_ref[...] = h_ref[...] + _dot(y_ref[...].reshape(nb * ts, y_ref.shape[2]), wout_ref[...]).reshape(nb, ts, d)


def _resident(t):
    return pl.BlockSpec(t.shape, lambda b, s: (0,) * t.ndim, pipeline_mode=pl.Buffered(1))


def _rg_lru(proj_ref, y_ref, xtail_ref, hcar_ref, convw_ref, convb_ref, wgate_ref, bgate_ref, lrua_ref, *, nb, ts, wa):
    cw = convw_ref[...]
    xcs = []
    for b in range(nb):
        ax = proj_ref[b, :, 0:wa]
        xcat = jnp.concatenate([xtail_ref[b], ax], axis=0)
        xtail_ref[b] = ax[ts - 8:ts]
        xc = convb_ref[...] + cw[CONV_W - 1:CONV_W] * ax
        for s in range(1, CONV_W):
            xc = xc + cw[CONV_W - 1 - s:CONV_W - s] * xcat[8 - s:8 - s + ts]
        xcs.append(xc)
    gates_all = _dot(jnp.concatenate(xcs, axis=0).astype(BF16), wgate_ref[...]) + bgate_ref[...]
    log_sig_a = _log_sigmoid(lrua_ref[...])
    groups = ts // SUBLANES
    row = lax.broadcasted_iota(jnp.int32, (groups, SUBLANES, wa), 1)
    for b in range(nb):
        xc = xcs[b]
        gates = gates_all[b * ts:(b + 1) * ts]
        r = _sigmoid(gates[:, :wa])
        i = _sigmoid(gates[:, wa:])
        log_a = LRU_C * r * log_sig_a
        a = jnp.exp(log_a)
        t = jnp.tanh(log_a)
        s2 = -2.0 * t / (1.0 - t)
        u = jnp.where(s2 > 0.0, s2 * lax.rsqrt(s2), 0.0) * (i * xc)
        u = u.reshape(groups, SUBLANES, wa)
        a = a.reshape(groups, SUBLANES, wa)
        d = 1
        while d < SUBLANES:
            keep = row >= d
            u = jnp.where(keep, a * pltpu.roll(u, d, 1) + u, u)
            a = jnp.where(keep, a * pltpu.roll(a, d, 1), a)
            d *= 2
        carry = hcar_ref[b]
        pieces = []
        for j in range(groups):
            hj = u[j] + a[j] * carry
            carry = hj[SUBLANES - 1:SUBLANES]
            pieces.append(hj)
        hcar_ref[b] = carry
        h = jnp.concatenate(pieces, axis=0)
        y_ref[b, :, 0:wa] = (h * _gelu_tanh(proj_ref[b, :, wa:2 * wa])).astype(y_ref.dtype)


def _even_mixer_kernel(h_ref, mnorm_ref, win_ref, convw_ref, convb_ref, wgate_ref, bgate_ref, lrua_ref,
                       wgk_ref, bgk_ref, gnorm_ref, y_ref, proj_ref, xtail_ref, hcar_ref, st_ref,
                       *, nb, ts, wa, dk, dv, rank):
    heads = GLA_HEADS
    wq = heads * dk
    wb = heads * dv
    o_q = 2 * wa
    o_k, o_v = o_q + wq, o_q + 2 * wq
    o_g, o_gk = o_v + wb, o_v + 2 * wb
    step = pl.program_id(1)

    @pl.when(step == 0)
    def _():
        xtail_ref[...] = jnp.zeros_like(xtail_ref)
        hcar_ref[...] = jnp.zeros_like(hcar_ref)
        st_ref[...] = jnp.zeros_like(st_ref)

    _project(h_ref, mnorm_ref, win_ref, proj_ref)
    _rg_lru(proj_ref, y_ref, xtail_ref, hcar_ref, convw_ref, convb_ref, wgate_ref, bgate_ref, lrua_ref,
            nb=nb, ts=ts, wa=wa)

    cg = MIX_CHUNK
    tri_incl = jnp.where(_tri(cg), 1.0, 0.0).astype(BF16)
    causal = _tri(cg)
    rowc = lax.broadcasted_iota(jnp.int32, (cg, wq), 0)
    items = []
    gk = proj_ref[:, :, o_gk:o_gk + rank].reshape(nb * ts, rank)
    gk_hi = gk.astype(BF16)
    gk_lo = (gk - gk_hi.astype(F32)).astype(BF16)
    w_hi = wgk_ref[...].astype(BF16)
    w_lo = (wgk_ref[...] - w_hi.astype(F32)).astype(BF16)
    z_all = _dot(gk_hi, w_hi) + (_dot(gk_lo, w_hi) + _dot(gk_hi, w_lo)) + bgk_ref[...]
    log_f_all = _log_sigmoid(z_all) * (1.0 / GLA_GATE_NORM)
    for b in range(nb):
        for c in range(ts // cg):
            r0 = c * cg
            cum = _dot_split_rhs(tri_incl, log_f_all[b * ts + r0:b * ts + r0 + cg], 2)
            q = proj_ref[b, r0:r0 + cg, o_q:o_q + wq] * (dk ** -0.5)
            k = proj_ref[b, r0:r0 + cg, o_k:o_k + wq]
            v = proj_ref[b, r0:r0 + cg, o_v:o_v + wb].astype(BF16)
            cum_last = cum[cg - 1:cg]
            it = dict(b=b, c=c, v=v, q_in=(q * jnp.exp(cum)).astype(BF16),
                      k_out=(k * jnp.exp(cum_last - cum)).astype(BF16), chunk_decay=jnp.exp(cum_last), qb=[], kb=[])
            for ib in range(cg // GLA_SUB):
                b0 = ib * GLA_SUB
                base = cum[b0 - 1:b0] if ib > 0 else jnp.zeros((1, wq), F32)
                it["qb"].append((q[b0:b0 + GLA_SUB] * jnp.exp(cum[b0:b0 + GLA_SUB] - base)).astype(BF16))
                it["kb"].append(jnp.where(rowc < b0 + GLA_SUB, k * jnp.exp(base - cum), 0.0).astype(BF16))
            items.append(it)
    for it in items:
        it["scores"] = [
            jnp.where(causal, jnp.concatenate(
                [_dot_nt(qb[:, hd * dk:(hd + 1) * dk], kb[:, hd * dk:(hd + 1) * dk])
                 for qb, kb in zip(it["qb"], it["kb"])], axis=0), 0.0).astype(BF16)
            for hd in range(heads)]
    for it in items:
        it["intra"] = [_dot(it["scores"][hd], it["v"][:, hd * dv:(hd + 1) * dv]) for hd in range(heads)]
    for c in range(ts // cg):
        r0 = c * cg
        for it in [t for t in items if t["c"] == c]:
            b = it["b"]
            outs = []
            for hd in range(heads):
                ks = slice(hd * dk, (hd + 1) * dk)
                vs = slice(hd * dv, (hd + 1) * dv)
                st = st_ref[b * heads + hd]
                o = it["intra"][hd] + _dot_nt(it["q_in"][:, ks], st.astype(BF16))
                st_ref[b * heads + hd] = st * it["chunk_decay"][:, ks] + _dot_tn(it["v"][:, vs], it["k_out"][:, ks])
                outs.append(o * lax.rsqrt(jnp.mean(o * o, axis=-1, keepdims=True) + HEAD_NORM_EPS))
            o_all = jnp.concatenate(outs, axis=-1) * gnorm_ref[...]
            g = proj_ref[b, r0:r0 + cg, o_g:o_g + wb]
            y_ref[b, r0:r0 + cg, wa:wa + wb] = (o_all * _silu(g)).astype(y_ref.dtype)


def _even_mixer(h, mnorm, win, convw, convb, wgate, bgate, lrua, wgk, bgk, gnorm, *, nb, ts, wa, dk, dv, rank):
    bsz, seq, d = h.shape
    cols = win.shape[1]
    heads = GLA_HEADS
    wb = heads * dv
    consts = (mnorm, win, convw, convb, wgate, bgate, lrua, wgk, bgk, gnorm)
    kern = functools.partial(_even_mixer_kernel, nb=nb, ts=ts, wa=wa, dk=dk, dv=dv, rank=rank)
    return pl.pallas_call(
        kern,
        grid=(bsz // nb, seq // ts),
        in_specs=[pl.BlockSpec((nb, ts, d), lambda b, s: (b, s, 0))] + [_resident(t) for t in consts],
        out_specs=pl.BlockSpec((nb, ts, wa + wb), lambda b, s: (b, s, 0)),
        out_shape=jax.ShapeDtypeStruct((bsz, seq, wa + wb), BF16),
        scratch_shapes=[pltpu.VMEM((nb, ts, cols), F32), pltpu.VMEM((nb, 8, wa), F32),
                        pltpu.VMEM((nb, 1, wa), F32), pltpu.VMEM((nb * heads, dv, dk), F32)],
        compiler_params=pltpu.CompilerParams(dimension_semantics=("arbitrary", "arbitrary"),
                                             vmem_limit_bytes=VMEM_LIMIT),
        name="even_mixer",
    )(h, *consts)


def _odd_mixer_kernel(h_ref, pos_ref, mnorm_ref, win_ref, freq_ref, rnorm_ref, mu_ref, w0_ref, ww2_ref, a0_ref,
                      aw2_ref, gw2_ref, kk_ref, ka_ref, rk_ref, wnorm_ref, seg_ref, wout_ref,
                      o_ref, y_ref, proj_ref, rstate_ref, dprev_ref, wstate_ref, *, nb, ts, wc, wd, rw, ra, rg):
    f32 = F32

    @pl.when(pl.program_id(1) == 0)
    def _():
        rstate_ref[...] = jnp.zeros_like(rstate_ref)
        dprev_ref[...] = jnp.zeros_like(dprev_ref)
        wstate_ref[...] = jnp.zeros_like(wstate_ref)

    _project(h_ref, mnorm_ref, win_ref, proj_ref)

    o_d = 4 * wc
    dcols = 3 * wd + rw + ra + rg
    hdim = RWKV_HEAD_DIM
    cm = MIX_CHUNK
    n = 2 * cm
    groups = wd // LANES
    seg = seg_ref[...]
    lane = lax.broadcasted_iota(jnp.int32, (cm, LANES), 1)
    first_head = lane < hdim

    def stack_heads(t):
        return jnp.concatenate([jnp.where(first_head, t, 0.0), jnp.where(first_head, 0.0, t)], axis=0)

    def split_heads(t):
        return jnp.concatenate([t[:, :hdim], t[:, hdim:]], axis=0)

    rin = lax.broadcasted_iota(jnp.int32, (n, n), 0)
    cin = lax.broadcasted_iota(jnp.int32, (n, n), 1)
    same_head = (rin < cm) == (cin < cm)
    strict = same_head & ((rin & (cm - 1)) > (cin & (cm - 1)))
    incl = same_head & ((rin & (cm - 1)) >= (cin & (cm - 1)))
    eye = jnp.where(rin == cin, 1.0, 0.0).astype(f32)
    tri_incl = jnp.where(_tri(cm), 1.0, 0.0).astype(BF16)

    work = []
    dparts = []
    for b in range(nb):
        dpart = proj_ref[b, :, o_d:o_d + dcols]
        dshift = jnp.concatenate([dprev_ref[b], dpart], axis=0)[7:7 + ts]
        dprev_ref[b] = dpart[ts - 8:ts]
        dparts.append(dpart + mu_ref[...] * (dshift - dpart))
    dall = jnp.concatenate(dparts, axis=0)
    r_all = dall[:, 0:wd]
    k_all = dall[:, wd:2 * wd]
    v_all = dall[:, 2 * wd:3 * wd]
    w_lr = dall[:, 3 * wd:3 * wd + rw]
    a_lr = dall[:, 3 * wd + rw:3 * wd + rw + ra]
    g_lr = dall[:, 3 * wd + rw + ra:dcols]
    log_decay_all = -math.exp(-0.5) * _sigmoid(w0_ref[...] + _dot(jnp.tanh(w_lr).astype(BF16), ww2_ref[...]))
    a_all = _sigmoid(a0_ref[...] + _dot(a_lr.astype(BF16), aw2_ref[...]))
    g_all = _dot(_sigmoid(g_lr).astype(BF16), gw2_ref[...])
    kk_all = k_all * kk_ref[...]
    k_all = k_all * (1.0 + (a_all - 1.0) * ka_ref[...])
    rows_all = nb * ts
    sums = _head_sums(jnp.concatenate([kk_all * kk_all, r_all * k_all * rk_ref[...]], axis=0), seg)
    kk_all = kk_all * lax.rsqrt(sums[:rows_all] + 1e-12)
    bonus_all = sums[rows_all:] * v_all
    kb_all = kk_all * a_all
    for b in range(nb):
        for ch in range(ts // cm):
            rs = slice(b * ts + ch * cm, b * ts + (ch + 1) * cm)
            ld = log_decay_all[rs]
            cum = _dot_split_rhs(tri_incl, ld, 2)
            cum_last = cum[cm - 1:cm]
            p_incl = jnp.exp(cum)
            p_inv = jnp.exp(-cum)
            a_t = -kk_all[rs] * jnp.exp(cum - ld)
            r_t = r_all[rs] * p_incl
            b_t = kb_all[rs] * p_inv
            k_t = k_all[rs] * p_inv
            p_out = jnp.exp(cum_last - cum)
            b_o = kb_all[rs] * p_out
            k_o = k_all[rs] * p_out
            p_last = jnp.exp(cum_last)
            for pr in range(groups):
                ls = slice(pr * LANES, (pr + 1) * LANES)
                l4 = jnp.concatenate([stack_heads(a_t[:, ls]), stack_heads(r_t[:, ls])], axis=0).astype(BF16)
                lc = jnp.concatenate([split_heads(a_t[:, ls]), split_heads(r_t[:, ls])], axis=0).astype(BF16)
                rc = jnp.concatenate([split_heads(b_t[:, ls]), split_heads(k_t[:, ls])], axis=0).astype(BF16)
                o4 = jnp.concatenate([stack_heads(b_o[:, ls]), stack_heads(k_o[:, ls])], axis=0).astype(BF16)
                vbd = stack_heads(v_all[rs, ls]).astype(BF16)
                work.append(dict(b=b, ch=ch, pr=pr, l4=l4, lc=lc, rc=rc, o4=o4, vbd=vbd, p_last=p_last[:, ls]))
    for wk in work:
        big = _dot_nt(wk["lc"], wk["rc"])
        a_ab = jnp.where(strict, big[:n, :n], 0.0)
        wk["a_ak"] = jnp.where(strict, big[:n, n:], 0.0).astype(BF16)
        wk["a_r"] = jnp.concatenate([jnp.where(incl, big[n:, :n], 0.0), jnp.where(incl, big[n:, n:], 0.0)],
                                    axis=1).astype(BF16)
        wk["x"] = a_ab.astype(BF16)
        wk["tinv"] = eye + a_ab
    span = 2
    while span < cm:
        for wk in work:
            wk["x"] = _dot(wk["x"], wk["x"]).astype(BF16)
        for wk in work:
            wk["tinv"] = wk["tinv"] + _dot(wk["tinv"].astype(BF16), wk["x"])
        span *= 2

    c = ts
    dh = wc // RET_HEADS
    half = dh // 2
    rowi = lax.broadcasted_iota(jnp.int32, (c, c), 0)
    coli = lax.broadcasted_iota(jnp.int32, (c, c), 1)
    rel = (rowi - coli).astype(f32)
    rowd = lax.broadcasted_iota(jnp.int32, (c, dh), 0).astype(f32)
    ret = []
    log_gammas = [math.log1p(-2.0 ** (-5.0 - hd)) for hd in range(RET_HEADS)]
    dmasks = [jnp.where(rel >= 0.0, jnp.exp(jnp.maximum(rel, 0.0) * lg), 0.0) for lg in log_gammas]
    decay_in = [jnp.exp((rowd + 1.0) * lg) for lg in log_gammas]
    decay_out = [jnp.exp((c - 1.0 - rowd) * lg) for lg in log_gammas]
    for b in range(nb):
        ang = freq_ref[...] * pos_ref[b]
        cos, sin = jnp.cos(ang).T, jnp.sin(ang).T
        cos2 = jnp.concatenate([cos, cos], axis=-1)
        sin2 = jnp.concatenate([-sin, sin], axis=-1)
        for hd in range(RET_HEADS):
            q = proj_ref[b, :, hd * dh:(hd + 1) * dh]
            k = proj_ref[b, :, wc + hd * dh:wc + (hd + 1) * dh]
            v = proj_ref[b, :, 2 * wc + hd * dh:2 * wc + (hd + 1) * dh].astype(BF16)
            q = (q * cos2 + pltpu.roll(q, half, 1) * sin2) * (dh ** -0.5)
            k = k * cos2 + pltpu.roll(k, half, 1) * sin2
            ret.append(dict(b=b, hd=hd, v=v, q=q.astype(BF16), k=k.astype(BF16), dmask=dmasks[hd],
                            q_in=(q * decay_in[hd]).astype(BF16), k_out=(k * decay_out[hd]).astype(BF16),
                            decay=math.exp(c * log_gammas[hd])))
    for rt in ret:
        rt["scores"] = (_dot_nt(rt["q"], rt["k"]) * rt["dmask"]).astype(BF16)
    for rt in ret:
        st = rstate_ref[rt["b"] * RET_HEADS + rt["hd"]]
        o = _dot(rt["scores"], rt["v"]) + _dot(rt["q_in"], st.astype(BF16))
        rstate_ref[rt["b"] * RET_HEADS + rt["hd"]] = st * rt["decay"] + _dot_tn(rt["k_out"], rt["v"])
        o = o - jnp.mean(o, axis=-1, keepdims=True)
        rt["o"] = o * lax.rsqrt(jnp.mean(o * o, axis=-1, keepdims=True) + HEAD_NORM_EPS)
    for b in range(nb):
        g = proj_ref[b, :, 3 * wc:4 * wc]
        o_all = jnp.concatenate([rt["o"] for rt in ret if rt["b"] == b], axis=-1)
        y_ref[b, :, 0:wc] = (o_all * rnorm_ref[...] * _silu(g)).astype(y_ref.dtype)

    for ch in range(ts // cm):
        wks = [wk for wk in work if wk["ch"] == ch]
        sidx = [wk["b"] * groups + wk["pr"] for wk in wks]
        sts = [wstate_ref[i] for i in sidx]
        fss = [_dot_nt(wk["l4"], st.astype(BF16)) for wk, st in zip(wks, sts)]
        rhs = [(fs[:n] + _dot(wk["a_ak"], wk["vbd"])).astype(BF16) for wk, fs in zip(wks, fss)]
        us = [_dot(wk["tinv"].astype(BF16), rh) for wk, rh in zip(wks, rhs)]
        uvs = [jnp.concatenate([u.astype(BF16), wk["vbd"]], axis=0) for wk, u in zip(wks, us)]
        for i, st, wk, uv in zip(sidx, sts, wks, uvs):
            wstate_ref[i] = st * wk["p_last"] + _dot_tn(uv, wk["o4"])
        ybds = [fs[n:] + _dot(wk["a_r"], uv) for wk, fs, uv in zip(wks, fss, uvs)]
        y = jnp.concatenate(
            [jnp.concatenate([yb[:cm] + yb[cm:] for wk, yb in zip(wks, ybds) if wk["b"] == b], axis=-1)
             for b in range(nb)], axis=0)
        inv_h = 1.0 / hdim
        y = y - _head_sums(y, seg) * inv_h
        y = y * lax.rsqrt(_head_sums(y * y, seg) * inv_h + RWKV_LN_EPS)
        for b in range(nb):
            tile_rows = slice(b * ts + ch * cm, b * ts + (ch + 1) * cm)
            yb = (y[b * cm:(b + 1) * cm] * wnorm_ref[...] + bonus_all[tile_rows]) * g_all[tile_rows]
            y_ref[b, ch * cm:(ch + 1) * cm, wc:wc + wd] = yb.astype(y_ref.dtype)
    _out_project(o_ref, h_ref, y_ref, wout_ref)


def _odd_mixer(h, pos, mnorm, win, freq, rnorm, mu, w0, ww2, a0, aw2, gw2, kk, ka, rk, wnorm, seg, wout, *, nb, ts, wc, wd, rw,
               ra, rg):
    bsz, seq, d = h.shape
    cols = win.shape[1]
    dh = wc // RET_HEADS
    consts = (mnorm, win, freq, rnorm, mu, w0, ww2, a0, aw2, gw2, kk, ka, rk, wnorm, seg, wout)
    kern = functools.partial(_odd_mixer_kernel, nb=nb, ts=ts, wc=wc, wd=wd, rw=rw, ra=ra, rg=rg)
    dcols = 3 * wd + rw + ra + rg
    return pl.pallas_call(
        kern,
        grid=(bsz // nb, seq // ts),
        in_specs=[pl.BlockSpec((nb, ts, d), lambda b, s: (b, s, 0)),
                  pl.BlockSpec((nb, 1, ts), lambda b, s: (b, 0, s))] + [_resident(t) for t in consts],
        out_specs=pl.BlockSpec((nb, ts, d), lambda b, s: (b, s, 0)),
        out_shape=jax.ShapeDtypeStruct((bsz, seq, d), F32),
        scratch_shapes=[pltpu.VMEM((nb, ts, wc + wd), BF16), pltpu.VMEM((nb, ts, cols), F32), pltpu.VMEM((nb * RET_HEADS, dh, dh), F32),
                        pltpu.VMEM((nb, 8, dcols), F32), pltpu.VMEM((nb * (wd // LANES), LANES, LANES), F32)],
        compiler_params=pltpu.CompilerParams(dimension_semantics=("arbitrary", "arbitrary"),
                                             vmem_limit_bytes=VMEM_LIMIT),
        name="odd_mixer",
    )(h, pos, *consts)


def _post_kernel(*refs, final, with_out_proj):
    if with_out_proj:
        h_ref, y_ref, wout_ref, *refs = refs
    else:
        h_ref, *refs = refs
    p_ref, fnorm_ref, wg_ref, wu_ref, wd_ref, pnorm_ref, pwg_ref, pbg_ref, pwp_ref, onorm_ref, o_ref = refs
    half = h_ref.shape[0] // 2
    starts = (0, half)
    hs, xs = [], []
    for r0 in starts:
        h = h_ref[r0:r0 + half]
        if with_out_proj:
            h = h + _dot(y_ref[r0:r0 + half], wout_ref[...])
        hs.append(h)
        xs.append(_rms(h, fnorm_ref[...]).astype(BF16))
    acts = [(_silu(_dot(xn, wg_ref[...])) * _dot(xn, wu_ref[...])).astype(BF16) for xn in xs]
    hs = [h + _dot(act, wd_ref[...]) for h, act in zip(hs, acts)]
    xgs = [_rms(h, pnorm_ref[...]).astype(BF16) for h in hs]
    for r0, h, xg in zip(starts, hs, xgs):
        gate = _sigmoid(_dot(xg, pwg_ref[...]) + pbg_ref[...])
        h = h + gate * _dot(p_ref[r0:r0 + half].astype(BF16), pwp_ref[...])
        if final:
            h = _rms(h, onorm_ref[...])
        o_ref[r0:r0 + half] = h


def _post(h, y, wout, p, fnorm, wg, wu, wd, pnorm, pwg, pbg, pwp, onorm, *, layer, tm, final):
    m, d = h.shape
    rows = lambda width: pl.BlockSpec((tm, width), lambda i: (i, 0))
    resident = lambda t: pl.BlockSpec(t.shape, lambda i: (0, 0), pipeline_mode=pl.Buffered(1))
    of_layer = lambda t: pl.BlockSpec((None,) + t.shape[1:], lambda i: (layer, 0, 0), pipeline_mode=pl.Buffered(1))
    stacked = (fnorm, wg, wu, wd, pnorm, pwg, pbg, pwp)
    return pl.pallas_call(
        functools.partial(_post_kernel, final=final, with_out_proj=y is not None),
        grid=(m // tm,),
        in_specs=[rows(d)] + ([rows(y.shape[1]), resident(wout)] if y is not None else [])
        + [pl.BlockSpec((None, tm, p.shape[2]), lambda i: (layer, i, 0))]
        + [of_layer(t) for t in stacked] + [resident(onorm)],
        out_specs=rows(d),
        out_shape=jax.ShapeDtypeStruct((m, d), F32),
        compiler_params=pltpu.CompilerParams(dimension_semantics=("arbitrary",), vmem_limit_bytes=VMEM_LIMIT),
        name="post_mixer",
    )(h, *((y, wout) if y is not None else ()), p, *stacked, onorm)


def _block_diag(blocks):
    g, bi, bj = blocks.shape
    eye = jnp.eye(g, dtype=blocks.dtype)
    return (eye[:, None, :, None] * blocks[:, :, None, :]).reshape(g * bi, g * bj)


def kernel(x, p, positions, ev_w_in, ev_conv_w, ev_conv_b, ev_lru_wr, ev_lru_br, ev_lru_wi, ev_lru_bi, ev_lru_a, ev_gla_wgk, ev_gla_bgk, ev_gla_norm, ev_w_out, od_w_in, od_ret_norm, od_rwkv_mu, od_rwkv_w0, od_rwkv_ww2, od_rwkv_a0, od_rwkv_aw2, od_rwkv_gw2, od_rwkv_kk, od_rwkv_ka, od_rwkv_rk, od_rwkv_norm, od_w_out, mix_norm, ffn_norm, ffn_w_gate, ffn_w_up, ffn_w_down, ple_norm, ple_w_gate, ple_b_gate, ple_w_proj, final_norm):
    bsz, seq, d = x.shape
    depth = p.shape[0]
    m = bsz * seq
    row = lambda t: t.reshape(1, -1).astype(F32)
    rows3 = lambda t: t.reshape(t.shape[0], 1, -1).astype(F32)

    wa = ev_conv_w.shape[-1]
    rank = ev_gla_wgk.shape[1]
    wq = ev_gla_wgk.shape[2]
    wb = ev_gla_norm.shape[-1]
    dk, dv = wq // GLA_HEADS, wb // GLA_HEADS
    wc = od_ret_norm.shape[-1]
    wd = od_rwkv_w0.shape[-1]
    rw, ra, rg = od_rwkv_ww2.shape[1], od_rwkv_aw2.shape[1], od_rwkv_gw2.shape[1]
    nb = MIX_BATCH if bsz % MIX_BATCH == 0 else 1

    h = x.reshape(m, d)
    freq = (ROPE_BASE ** (-jnp.arange(wc // RET_HEADS // 2, dtype=F32) / (wc // RET_HEADS // 2))).reshape(-1, 1)
    pos = positions.astype(F32).reshape(bsz, 1, seq)
    seg = _block_diag(jnp.ones((LANES // RWKV_HEAD_DIM, RWKV_HEAD_DIM, RWKV_HEAD_DIM), BF16))
    assert 2 * RWKV_HEAD_DIM == LANES and wd % LANES == 0
    assert seq % EVEN_TILE == 0 and seq % ODD_TILE == 0 and m % POST_ROWS == 0

    for i in range(depth):
        j = i // 2
        if i % 2 == 0:
            w_in = ev_w_in[j]
            cols = w_in.shape[1]
            cols_pad = -(-cols // LANES) * LANES
            w_in = jnp.pad(w_in, ((0, 0), (0, cols_pad - cols))).astype(BF16)
            wgate = jnp.concatenate([_block_diag(ev_lru_wr[j]), _block_diag(ev_lru_wi[j])], axis=1).astype(BF16)
            bgate = jnp.concatenate([ev_lru_br[j], ev_lru_bi[j]]).reshape(1, -1)
            y = _even_mixer(h.reshape(bsz, seq, d), row(mix_norm[i]), w_in, ev_conv_w[j], row(ev_conv_b[j]), wgate, bgate,
                            row(ev_lru_a[j]), ev_gla_wgk[j], row(ev_gla_bgk[j]), row(ev_gla_norm[j]),
                            nb=nb, ts=EVEN_TILE, wa=wa, dk=dk, dv=dv, rank=rank)
            y, w_out = y.reshape(m, -1), ev_w_out[j].astype(BF16)
        else:
            h = _odd_mixer(h.reshape(bsz, seq, d), pos, row(mix_norm[i]), od_w_in[j].astype(BF16), freq,
                           row(od_ret_norm[j]), row(od_rwkv_mu[j]),
                           row(od_rwkv_w0[j]), od_rwkv_ww2[j].astype(BF16), row(od_rwkv_a0[j]),
                           od_rwkv_aw2[j].astype(BF16), od_rwkv_gw2[j].astype(BF16), row(od_rwkv_kk[j]),
                           row(od_rwkv_ka[j]), row(od_rwkv_rk[j]), row(od_rwkv_norm[j]), seg, od_w_out[j].astype(BF16),
                           nb=nb, ts=ODD_TILE, wc=wc, wd=wd, rw=rw, ra=ra, rg=rg)
            h, y, w_out = h.reshape(m, d), None, None
        h = _post(h, y, w_out, p.reshape(depth, m, -1), rows3(ffn_norm),
                  ffn_w_gate.astype(BF16), ffn_w_up.astype(BF16), ffn_w_down.astype(BF16),
                  rows3(ple_norm), ple_w_gate.astype(BF16), rows3(ple_b_gate), ple_w_proj.astype(BF16),
                  row(final_norm), layer=i, tm=POST_ROWS, final=(i == depth - 1))
    return h.reshape(bsz, seq, d)
```
